```python
import math
import jax, jax.numpy as jnp
from jax import lax
import numpy as np

D_MODEL = 1024
BATCH = 2
SEQ = 8192
DEPTH = 2
DEC_BATCH = 128
DEC_SEQ = 1
PAST_LEN = 2048
PAGE_SIZE = 128

N_HEADS = 16
N_KV = 4
HEAD_DIM = 64
Q_PER_KV = N_HEADS // N_KV
L_CMP = 32
L_SEL = 64
TOP_N = 16
WINDOW = 512
CMP_HID = 128
Q_BLOCK = 128
SCALE = HEAD_DIM ** -0.5
N_BUCKETS = 32
MAX_DISTANCE = 128
D_INNER = 2 * D_MODEL
SSM_HEADDIM = 64
SSM_HEADS = D_INNER // SSM_HEADDIM
SSM_GROUPS = 4
D_STATE = 128
SSM_CONV = 4
SSM_CHUNK = 128
CONV_DIM = D_INNER + 2 * SSM_GROUPS * D_STATE
D_FF = 2816
FFN_CONV = 3
EPS = 1e-6
NEG_INF = -1e30

Q_COLS = N_HEADS * HEAD_DIM
KV_COLS = 6 * N_KV * HEAD_DIM
NSA_GATE_COLS = 3 * N_HEADS
MERGE_COLS = 2 * D_MODEL
P_IN = Q_COLS + KV_COLS + NSA_GATE_COLS + D_INNER + CONV_DIM + SSM_HEADS + MERGE_COLS
SPLITS = (Q_COLS,
          Q_COLS + KV_COLS,
          Q_COLS + KV_COLS + NSA_GATE_COLS,
          Q_COLS + KV_COLS + NSA_GATE_COLS + D_INNER,
          Q_COLS + KV_COLS + NSA_GATE_COLS + D_INNER + CONV_DIM,
          Q_COLS + KV_COLS + NSA_GATE_COLS + D_INNER + CONV_DIM + SSM_HEADS)

kernel_name = "nsa_mamba2_convffn_hybrid_step"


def rms_norm(x, g):
    xf = x.astype(jnp.float32)
    y = xf * lax.rsqrt(jnp.mean(xf * xf, axis=-1, keepdims=True) + EPS)
    return (y * g.astype(jnp.float32)).astype(x.dtype)


def t5_bucket(dist):
    n = jnp.maximum(dist, 0)
    max_exact = N_BUCKETS // 2
    nf = jnp.maximum(n, max_exact).astype(jnp.float32)
    large = max_exact + (jnp.log(nf / max_exact) / math.log(MAX_DISTANCE / max_exact)
                         * (N_BUCKETS - max_exact)).astype(jnp.int32)
    return jnp.where(n < max_exact, n, jnp.minimum(large, N_BUCKETS - 1))


def causal_dwconv(x, prev, w, b):
    K = w.shape[0]
    T = x.shape[1]
    xp = jnp.concatenate([prev.astype(x.dtype), x], axis=1)
    y = b
    for k in range(K):
        y = y + w[k] * xp[:, k:k + T]
    return y, xp[:, T:]


def compress(k, pe, w1, w2):
    B, Tk, G, D = k.shape
    nc = Tk // L_CMP
    kb = k.reshape(B, nc, L_CMP, G, D) + pe[None, None, :, None, :]
    flat = kb.transpose(0, 1, 3, 2, 4).reshape(B, nc, G, L_CMP * D)
    return jax.nn.silu(flat @ w1) @ w2


def nsa_compressed(q, k, v, q_pos, rel_bias, pe, w1, w2):
    kc = compress(k, pe[0], w1[0], w2[0])
    vc = compress(v, pe[1], w1[1], w2[1])
    nc = kc.shape[1]
    k_end = jnp.arange(nc) * L_CMP + (L_CMP - 1)
    dist = q_pos[:, None] - k_end[None, :]
    bias = rel_bias[t5_bucket(dist)].reshape(dist.shape + (N_KV, Q_PER_KV)).transpose(2, 3, 0, 1)
    logits = jnp.einsum('bqgrd,bcgd->bgrqc', q, kc).astype(jnp.float32) * SCALE + bias.astype(jnp.float32)
    valid = dist >= 0
    p = jnp.where(valid, jax.nn.softmax(jnp.where(valid, logits, NEG_INF), axis=-1), 0.0)
    o = jnp.einsum('bgrqc,bcgd->bqgrd', p.astype(vc.dtype), vc)
    return o, p


def nsa_select_idx(p_cmp, q_pos, n_blocks):
    imp = p_cmp.sum(axis=2)
    B, G, Tq, nc = imp.shape
    imp = imp.reshape(B, G, Tq, n_blocks, L_SEL // L_CMP).sum(-1)
    blk = jnp.arange(n_blocks)
    forced = (blk[None] == (q_pos // L_SEL)[:, None]) | (blk[None] == 0)
    valid = blk[None] * L_SEL <= q_pos[:, None]
    score = jnp.where(forced, jnp.inf, jnp.where(valid, imp, -jnp.inf))
    _, idx = lax.top_k(score, min(TOP_N, n_blocks))
    return idx


def nsa_selected_block(q, q_pos, idx, kb, vb, rel_bias):
    B, G = idx.shape[:2]
    bi = jnp.arange(B)[:, None, None, None]
    gi = jnp.arange(G)[None, :, None, None]
    kg = kb[bi, gi, idx]
    vg = vb[bi, gi, idx]
    k_pos = idx[..., None] * L_SEL + jnp.arange(L_SEL)
    dist = q_pos[None, None, :, None, None] - k_pos
    rb = rel_bias.reshape(N_BUCKETS, N_KV, Q_PER_KV).transpose(1, 0, 2)
    bias = rb[gi[..., None], t5_bucket(dist)].transpose(0, 1, 5, 2, 3, 4)
    logits = jnp.einsum('bqgrd,bgqnld->bgrqnl', q, kg).astype(jnp.float32) * SCALE + bias.astype(jnp.float32)
    logits = jnp.where((dist >= 0)[:, :, None], logits, NEG_INF)
    logits = logits.reshape(logits.shape[:4] + (-1,))
    p = jax.nn.softmax(logits, axis=-1)
    vg = vg.reshape(vg.shape[:3] + (-1, vg.shape[-1]))
    return jnp.einsum('bgrqk,bgqkd->bqgrd', p.astype(vg.dtype), vg)


def nsa_selected(q, k, v, q_pos, idx, rel_bias):
    B, Tq = q.shape[:2]
    nb = k.shape[1] // L_SEL
    kb = k.reshape(B, nb, L_SEL, N_KV, HEAD_DIM).transpose(0, 3, 1, 2, 4)
    vb = v.reshape(B, nb, L_SEL, N_KV, HEAD_DIM).transpose(0, 3, 1, 2, 4)
    qb = Q_BLOCK if Tq % Q_BLOCK == 0 else Tq
    nq = Tq // qb
    qs = q.reshape(B, nq, qb, N_KV, Q_PER_KV, HEAD_DIM).transpose(1, 0, 2, 3, 4, 5)
    ps = q_pos.reshape(nq, qb)
    ids = idx.reshape(B, N_KV, nq, qb, -1).transpose(2, 0, 1, 3, 4)
    o = lax.map(lambda a: nsa_selected_block(a[0], a[1], a[2], kb, vb, rel_bias), (qs, ps, ids))
    return o.transpose(1, 0, 2, 3, 4, 5).reshape(B, Tq, N_KV, Q_PER_KV, HEAD_DIM)


def window_attn(q, k, v, q_pos, k_pos, rel_bias):
    dist = q_pos[:, None] - k_pos[None, :]
    mask = (dist >= 0) & (dist <= WINDOW) & (k_pos[None, :] >= 0)
    bias = rel_bias[t5_bucket(dist)].reshape(dist.shape + (N_KV, Q_PER_KV)).transpose(2, 3, 0, 1)
    logits = jnp.einsum('bqgrd,bkgd->bgrqk', q, k).astype(jnp.float32) * SCALE + bias.astype(jnp.float32)
    p = jax.nn.softmax(jnp.where(mask, logits, NEG_INF), axis=-1)
    return jnp.einsum('bgrqk,bkgd->bqgrd', p.astype(v.dtype), v)


def window_prompt(q, k, v, rel_bias):
    B, T = q.shape[:2]
    qb = Q_BLOCK if T % Q_BLOCK == 0 else T
    nq = T // qb
    pad = jnp.zeros((B, WINDOW) + k.shape[2:], k.dtype)
    kp = jnp.concatenate([pad, k], axis=1)
    vp = jnp.concatenate([pad, v], axis=1)
    qs = q.reshape(B, nq, qb, N_KV, Q_PER_KV, HEAD_DIM).transpose(1, 0, 2, 3, 4, 5)
    starts = jnp.arange(nq) * qb

    def blk(a):
        qi, s = a
        kw = lax.dynamic_slice_in_dim(kp, s, qb + WINDOW, axis=1)
        vw = lax.dynamic_slice_in_dim(vp, s, qb + WINDOW, axis=1)
        return window_attn(qi, kw, vw, s + jnp.arange(qb), s - WINDOW + jnp.arange(qb + WINDOW), rel_bias)

    o = lax.map(blk, (qs, starts))
    return o.transpose(1, 0, 2, 3, 4, 5).reshape(B, T, N_KV, Q_PER_KV, HEAD_DIM)


def ssd(x, dt, A, Bm, Cm, h0):
    f32 = jnp.float32
    Bsz, T, H, P = x.shape
    G, N = Bm.shape[2], Bm.shape[3]
    R = H // G
    Lc = math.gcd(T, SSM_CHUNK)
    nc = T // Lc
    xc = x.astype(f32).reshape(Bsz, nc, Lc, G, R, P)
    dtc = dt.astype(f32).reshape(Bsz, nc, Lc, G, R)
    Bc = Bm.astype(f32).reshape(Bsz, nc, Lc, G, N)
    Cc = Cm.astype(f32).reshape(Bsz, nc, Lc, G, N)
    a_cs = jnp.cumsum(dtc * A.astype(f32).reshape(G, R), axis=2)
    seg = a_cs[:, :, :, None] - a_cs[:, :, None, :]
    causal = jnp.tril(jnp.ones((Lc, Lc), bool))[:, :, None, None]
    decay = jnp.exp(jnp.where(causal, seg, -jnp.inf))
    cb = jnp.einsum('bclgn,bcsgn->bclsg', Cc, Bc)
    w = cb[..., None] * decay * dtc[:, :, None]
    y_diag = jnp.einsum('bclsgr,bcsgrp->bclgrp', w, xc)
    decay_end = jnp.exp(a_cs[:, :, -1:] - a_cs)
    st = jnp.einsum('bclgn,bclgrp->bcgrpn', Bc, (decay_end * dtc)[..., None] * xc)
    chunk_decay = jnp.exp(a_cs[:, :, -1])

    def step(h, inp):
        s_c, d_c = inp
        return h * d_c[..., None, None] + s_c, h

    h_last, h_prev = lax.scan(step, h0.astype(f32).reshape(Bsz, G, R, P, N),
                              (st.transpose(1, 0, 2, 3, 4, 5), chunk_decay.transpose(1, 0, 2, 3)))
    h_prev = h_prev.transpose(1, 0, 2, 3, 4, 5)
    y_off = jnp.einsum('bclgn,bcgrpn->bclgrp', Cc, h_prev) * jnp.exp(a_cs)[..., None]
    y = (y_diag + y_off).reshape(Bsz, T, H, P)
    return y, h_last.reshape(Bsz, H, P, N)


def mixer(h, l, pos0, past_kv, win_buf, conv_prev, ssm_h0, W):
    B, T, _ = h.shape
    proj = h @ W['w_in'][l]
    q, kv, nsa_g, z, xbc, dt_raw, mg = jnp.split(proj, SPLITS, axis=-1)
    q = q.reshape(B, T, N_KV, Q_PER_KV, HEAD_DIM)
    kv = kv.reshape(B, T, 6, N_KV, HEAD_DIM)
    q_pos = pos0 + jnp.arange(T)
    rel_bias = W['rel_bias']
    new_rows = kv[:, :, :4]
    parts = [new_rows] if past_kv is None else [past_kv.astype(kv.dtype), new_rows]
    pad = (-(pos0 + T)) % L_SEL
    if pad:
        parts.append(jnp.zeros((B, pad) + new_rows.shape[2:], kv.dtype))
    full = jnp.concatenate(parts, axis=1) if len(parts) > 1 else new_rows
    o_cmp, p_cmp = nsa_compressed(q, full[:, :, 0], full[:, :, 1], q_pos, rel_bias,
                                  W['cmp_pe'][l], W['cmp_w1'][l], W['cmp_w2'][l])
    idx = nsa_select_idx(p_cmp, q_pos, full.shape[1] // L_SEL)
    o_sel = nsa_selected(q, full[:, :, 2], full[:, :, 3], q_pos, idx, rel_bias)
    win_new = kv[:, :, 4:]
    if win_buf is None:
        o_win = window_prompt(q, win_new[:, :, 0], win_new[:, :, 1], rel_bias)
        win_state = win_new[:, T - min(WINDOW, T):]
    else:
        Lb = win_buf.shape[1]
        wk = jnp.concatenate([win_buf.astype(kv.dtype), win_new], axis=1)
        k_pos = pos0 - Lb + jnp.arange(Lb + T)
        o_win = window_attn(q, wk[:, :, 0], wk[:, :, 1], q_pos, k_pos, rel_bias)
        win_state = wk[:, T:]
    g = jax.nn.sigmoid(nsa_g).reshape(B, T, 3, N_KV, Q_PER_KV)[..., None]
    o_nsa = (g[:, :, 0] * o_cmp + g[:, :, 1] * o_sel + g[:, :, 2] * o_win).reshape(B, T, Q_COLS)
    xbc, conv_state = causal_dwconv(xbc, conv_prev, W['ssm_conv_w'][l], W['ssm_conv_b'][l])
    xbc = jax.nn.silu(xbc)
    xs, bm, cm = jnp.split(xbc, (D_INNER, D_INNER + SSM_GROUPS * D_STATE), axis=-1)
    dt = jax.nn.softplus((dt_raw + W['dt_bias'][l]).astype(jnp.float32))
    a = -jnp.exp(W['a_log'][l].astype(jnp.float32))
    xh = xs.reshape(B, T, SSM_HEADS, SSM_HEADDIM)
    y, ssm_state = ssd(xh, dt, a, bm.reshape(B, T, SSM_GROUPS, D_STATE),
                       cm.reshape(B, T, SSM_GROUPS, D_STATE), ssm_h0)
    y = (y + W['d_skip'][l].astype(jnp.float32)[:, None] * xh.astype(jnp.float32)).astype(h.dtype)
    y = (y.reshape(B, T, D_INNER) * jax.nn.silu(z)).reshape(B, T, SSM_GROUPS, D_INNER // SSM_GROUPS)
    y = rms_norm(y, W['ssm_norm_g'][l].reshape(SSM_GROUPS, -1)).reshape(B, T, D_INNER)
    pa = o_nsa @ W['w_a'][l]
    pb = y @ W['w_b'][l]
    ga, gb = jnp.split(jax.nn.sigmoid(mg), 2, axis=-1)
    out = (ga * pa + gb * pb) @ W['w_o'][l]
    return out, new_rows, win_state, conv_state, ssm_state


def conv_ffn(h, l, ffn_prev, W):
    gate, up = jnp.split(h @ W['w_gu'][l], 2, axis=-1)
    gate, ffn_state = causal_dwconv(gate, ffn_prev, W['ffn_conv_w'][l], W['ffn_conv_b'][l])
    return (jax.nn.silu(gate) * up) @ W['w_down'][l], ffn_state


def layer_step(x, c, l, pos0, past_kv, win_buf, conv_prev, ssm_h0, ffn_prev, W):
    mod = c @ W['w_ada'][l] + W['b_ada'][l]
    sh1, sc1, g1, sh2, sc2, g2 = [m[:, None] for m in jnp.split(mod, 6, axis=-1)]
    h = rms_norm(x, W['norm1_g'][l]) * (1 + sc1) + sh1
    m_out, kv_rows, win_state, conv_state, ssm_state = mixer(h, l, pos0, past_kv, win_buf, conv_prev, ssm_h0, W)
    x = x + g1 * m_out
    h = rms_norm(x, W['norm2_g'][l]) * (1 + sc2) + sh2
    f_out, ffn_state = conv_ffn(h, l, ffn_prev, W)
    x = x + g2 * f_out
    return x, (kv_rows, win_state, conv_state, ssm_state, ffn_state)


def setup_inputs(seed: int = 0) -> dict:
    key = jax.random.key(seed)
    ks = iter(jax.random.split(key, 40))
    nrm = lambda shape, s=1.0: jax.random.normal(next(ks), shape, jnp.float32) * s
    n_pages = PAST_LEN // PAGE_SIZE
    n_used = DEC_BATCH * n_pages
    n_pool = n_used + (n_used + 3) // 4
    win_len = min(WINDOW, PAST_LEN)
    page_table = jax.random.permutation(next(ks), n_pool)[:n_used].reshape(DEC_BATCH, n_pages).astype(jnp.int32)
    dt0 = jnp.exp(jax.random.uniform(next(ks), (DEPTH, SSM_HEADS), jnp.float32,
                                     math.log(1e-3), math.log(1e-1)))
    return {
        'x_prompt': nrm((BATCH, SEQ, D_MODEL)),
        'x_sample': nrm((DEC_BATCH, DEC_SEQ, D_MODEL)),
        'cache_kv': nrm((DEPTH, n_pool, PAGE_SIZE, 4, N_KV, HEAD_DIM)),
        'cache_win_kv': nrm((DEPTH, DEC_BATCH, win_len, 2, N_KV, HEAD_DIM)),
        'state_ssm_conv': nrm((DEPTH, DEC_BATCH, SSM_CONV - 1, CONV_DIM)),
        'state_ssm': nrm((DEPTH, DEC_BATCH, SSM_HEADS, SSM_HEADDIM, D_STATE), 0.1),
        'state_ffn_conv': nrm((DEPTH, DEC_BATCH, FFN_CONV - 1, D_FF)),
        'page_table': page_table,
        'c_prompt': nrm((BATCH, D_MODEL)),
        'c_sample': nrm((DEC_BATCH, D_MODEL)),
        'w_ada': nrm((DEPTH, D_MODEL, 6 * D_MODEL), 0.5 * D_MODEL ** -0.5),
        'b_ada': nrm((DEPTH, 6 * D_MODEL), 0.02),
        'norm1_g': 1.0 + nrm((DEPTH, D_MODEL), 0.02),
        'norm2_g': 1.0 + nrm((DEPTH, D_MODEL), 0.02),
        'final_g': 1.0 + nrm((D_MODEL,), 0.02),
        'w_in': nrm((DEPTH, D_MODEL, P_IN), D_MODEL ** -0.5),
        'rel_bias': nrm((N_BUCKETS, N_HEADS), 0.5),
        'cmp_pe': nrm((DEPTH, 2, L_CMP, HEAD_DIM), 0.1),
        'cmp_w1': nrm((DEPTH, 2, L_CMP * HEAD_DIM, CMP_HID), (L_CMP * HEAD_DIM) ** -0.5),
        'cmp_w2': nrm((DEPTH, 2, CMP_HID, HEAD_DIM), 2.0 * CMP_HID ** -0.5),
        'ssm_conv_w': nrm((DEPTH, SSM_CONV, CONV_DIM), SSM_CONV ** -0.5),
        'ssm_conv_b': nrm((DEPTH, CONV_DIM), 0.02),
        'dt_bias': dt0 + jnp.log(-jnp.expm1(-dt0)),
        'a_log': jnp.log(jax.random.uniform(next(ks), (DEPTH, SSM_HEADS), jnp.float32, 1.0, 16.0)),
        'd_skip': 1.0 + nrm((DEPTH, SSM_HEADS), 0.02),
        'ssm_norm_g': 1.0 + nrm((DEPTH, D_INNER), 0.02),
        'w_a': nrm((DEPTH, Q_COLS, D_MODEL), Q_COLS ** -0.5),
        'w_b': nrm((DEPTH, D_INNER, D_MODEL), D_INNER ** -0.5),
        'w_o': nrm((DEPTH, D_MODEL, D_MODEL), D_MODEL ** -0.5),
        'w_gu': nrm((DEPTH, D_MODEL, 2 * D_FF), D_MODEL ** -0.5),
        'ffn_conv_w': nrm((DEPTH, FFN_CONV, D_FF), FFN_CONV ** -0.5),
        'ffn_conv_b': nrm((DEPTH, D_FF), 0.02),
        'w_down': nrm((DEPTH, D_FF, D_MODEL), D_FF ** -0.5),
    }


def reference(x_prompt, x_sample, cache_kv, cache_win_kv, state_ssm_conv, state_ssm, state_ffn_conv,
              page_table, c_prompt, c_sample, w_ada, b_ada, norm1_g, norm2_g, final_g, w_in, rel_bias,
              cmp_pe, cmp_w1, cmp_w2, ssm_conv_w, ssm_conv_b, dt_bias, a_log, d_skip, ssm_norm_g,
              w_a, w_b, w_o, w_gu, ffn_conv_w, ffn_conv_b, w_down):
    W = dict(w_ada=w_ada, b_ada=b_ada, norm1_g=norm1_g, norm2_g=norm2_g, w_in=w_in, rel_bias=rel_bias,
             cmp_pe=cmp_pe, cmp_w1=cmp_w1, cmp_w2=cmp_w2, ssm_conv_w=ssm_conv_w, ssm_conv_b=ssm_conv_b,
             dt_bias=dt_bias, a_log=a_log, d_skip=d_skip, ssm_norm_g=ssm_norm_g, w_a=w_a, w_b=w_b, w_o=w_o,
             w_gu=w_gu, ffn_conv_w=ffn_conv_w, ffn_conv_b=ffn_conv_b, w_down=w_down)
    Bp = x_prompt.shape[0]
    Bs = x_sample.shape[0]
    past_len = page_table.shape[1] * cache_kv.shape[2]
    xp, xs = x_prompt, x_sample
    st_p, st_s = [], []
    for l in range(DEPTH):
        xp, sp = layer_step(
            xp, c_prompt, l, 0, None, None,
            jnp.zeros((Bp, SSM_CONV - 1, CONV_DIM), xp.dtype),
            jnp.zeros((Bp, SSM_HEADS, SSM_HEADDIM, D_STATE), jnp.float32),
            jnp.zeros((Bp, FFN_CONV - 1, D_FF), xp.dtype), W)
        st_p.append(sp)
        past = cache_kv[l, page_table].reshape(Bs, past_len, 4, N_KV, HEAD_DIM)
        xs, ss = layer_step(xs, c_sample, l, past_len, past, cache_win_kv[l], state_ssm_conv[l],
                            state_ssm[l], state_ffn_conv[l], W)
        st_s.append(ss)
    y_prompt = rms_norm(xp, final_g)
    y_sample = rms_norm(xs, final_g)
    kv_p, win_p, conv_p, ssm_p, ffn_p = [jnp.stack([s[i] for s in st_p]) for i in range(5)]
    kv_s, win_s, conv_s, ssm_s, ffn_s = [jnp.stack([s[i] for s in st_s]) for i in range(5)]
    return (y_prompt, y_sample, kv_p, win_p, conv_p, ssm_p, ffn_p, kv_s, win_s, conv_s, ssm_s, ffn_s)
```

```python
import functools
import math

import jax
import jax.numpy as jnp
from jax import lax
from jax.experimental import pallas as pl
from jax.experimental.pallas import tpu as pltpu

D_MODEL = 1024
N_HEADS = 16
N_KV = 4
HEAD_DIM = 64
Q_PER_KV = N_HEADS // N_KV
L_CMP = 32
L_SEL = 64
TOP_N = 16
WINDOW = 512
CMP_HID = 128
Q_BLOCK = 128
SCALE = HEAD_DIM ** -0.5
N_BUCKETS = 32
MAX_DISTANCE = 128
D_INNER = 2 * D_MODEL
SSM_HEADDIM = 64
SSM_HEADS = D_INNER // SSM_HEADDIM
SSM_GROUPS = 4
D_STATE = 128
SSM_CONV = 4
SSM_CHUNK = 128
CONV_DIM = D_INNER + 2 * SSM_GROUPS * D_STATE
D_FF = 2816
FFN_CONV = 3
EPS = 1e-6
NEG_INF = -1e30
DEPTH = 2

Q_COLS = N_HEADS * HEAD_DIM
KV_COLS = 6 * N_KV * HEAD_DIM
NSA_GATE_COLS = 3 * N_HEADS
OFF_KV = Q_COLS
OFF_G = OFF_KV + KV_COLS
OFF_Z = OFF_G + NSA_GATE_COLS
OFF_XBC = OFF_Z + D_INNER
OFF_DT = OFF_XBC + CONV_DIM
OFF_MG = OFF_DT + SSM_HEADS
P_IN = OFF_MG + 2 * D_MODEL

LANES = 128
SUBLANES = 8
TQ = 128
FAR_BUCKET_DIST = 113
VMEM_LIMIT = 48 * 1024 * 1024

BF16 = jnp.bfloat16
F32 = jnp.float32
HIGHEST = lax.Precision.HIGHEST


def _cparams(sem):
    return pltpu.CompilerParams(dimension_semantics=sem, vmem_limit_bytes=VMEM_LIMIT)


def _silu(x):
    return x * jax.nn.sigmoid(x)


def _dot(a, b):
    return jnp.dot(a, b, preferred_element_type=F32)


def _dot_nt(a, b):
    return lax.dot_general(a, b, (((1,), (1,)), ((), ())), preferred_element_type=F32)


def _dot_tn(a, b):
    return lax.dot_general(a, b, (((0,), (0,)), ((), ())), preferred_element_type=F32)


def _mm_body(a_ref, w_ref, *refs, epilogue, n_extra):
    acc = _dot(a_ref[...], w_ref[...])
    extras = [r[...] for r in refs[:n_extra]]
    o_ref = refs[n_extra]
    o_ref[...] = epilogue(acc, *extras).astype(o_ref.dtype)


def _mm(a, w, *, tm, tn, name, out_dtype=F32, epilogue=None, extras=(), rows_per_batch=None):
    M, K = a.shape
    N = w.shape[1]
    assert M % tm == 0 and N % tn == 0, (M, N, tm, tn)
    in_specs = [pl.BlockSpec((tm, K), lambda i, j: (i, 0)),
                pl.BlockSpec((K, tn), lambda i, j: (0, j))]
    ops = [a, w]
    for kind, arr in extras:
        if kind == 'col':
            in_specs.append(pl.BlockSpec((1, tn), lambda i, j: (0, j)))
        elif kind == 'full':
            in_specs.append(pl.BlockSpec((tm, tn), lambda i, j: (i, j)))
        else:
            assert kind == 'batch' and rows_per_batch % tm == 0
            tpb = rows_per_batch // tm
            in_specs.append(pl.BlockSpec((None, 1, tn), lambda i, j, tpb=tpb: (i // tpb, 0, j)))
        ops.append(arr)
    if epilogue is None:
        epilogue = lambda acc: acc
    return pl.pallas_call(
        functools.partial(_mm_body, epilogue=epilogue, n_extra=len(extras)),
        grid=(M // tm, N // tn),
        in_specs=in_specs,
        out_specs=pl.BlockSpec((tm, tn), lambda i, j: (i, j)),
        out_shape=jax.ShapeDtypeStruct((M, N), out_dtype),
        compiler_params=_cparams(("parallel", "parallel")),
        name=name,
    )(*ops)


def _norm_mod_body(x_ref, g_ref, sc_ref, sh_ref, o_ref):
    x = x_ref[...]
    y = x * lax.rsqrt(jnp.mean(x * x, axis=-1, keepdims=True) + EPS) * g_ref[...]
    o_ref[...] = (y * (1.0 + sc_ref[...]) + sh_ref[...]).astype(o_ref.dtype)


def _norm_mod(x, g, sc, sh, *, tm, rows_per_batch, name):
    M, D = x.shape
    if sc.ndim == 3:
        tpb = rows_per_batch // tm
        mod_spec = pl.BlockSpec((None, 1, D), lambda i: (i // tpb, 0, 0))
    else:
        mod_spec = pl.BlockSpec((tm, D), lambda i: (i, 0))
    return pl.pallas_call(
        _norm_mod_body,
        grid=(M // tm,),
        in_specs=[pl.BlockSpec((tm, D), lambda i: (i, 0)), pl.BlockSpec((1, D), lambda i: (0, 0)),
                  mod_spec, mod_spec],
        out_specs=pl.BlockSpec((tm, D), lambda i: (i, 0)),
        out_shape=jax.ShapeDtypeStruct((M, D), BF16),
        compiler_params=_cparams(("parallel",)),
        name=name,
    )(x, g.reshape(1, D), sc, sh)


def _final_norm_body(x_ref, g_ref, o_ref):
    x = x_ref[...]
    o_ref[...] = x * lax.rsqrt(jnp.mean(x * x, axis=-1, keepdims=True) + EPS) * g_ref[...]


def _final_norm(x, g, *, tm, name):
    M, D = x.shape
    return pl.pallas_call(
        _final_norm_body,
        grid=(M // tm,),
        in_specs=[pl.BlockSpec((tm, D), lambda i: (i, 0)), pl.BlockSpec((1, D), lambda i: (0, 0))],
        out_specs=pl.BlockSpec((tm, D), lambda i: (i, 0)),
        out_shape=jax.ShapeDtypeStruct((M, D), F32),
        compiler_params=_cparams(("parallel",)),
        name=name,
    )(x, g.reshape(1, D))


def _merge_body(o_ref, y_ref, wa_ref, wb_ref, ga_ref, gb_ref, u_ref):
    pa = _dot(o_ref[...], wa_ref[...])
    pb = _dot(y_ref[...], wb_ref[...])
    u_ref[...] = (ga_ref[...] * pa + gb_ref[...] * pb).astype(u_ref.dtype)


def _merge(o_nsa, y, w_a, w_b, gates, *, tm, tn, name):
    M = o_nsa.shape[0]
    N = w_a.shape[1]
    nj = N // tn
    return pl.pallas_call(
        _merge_body,
        grid=(M // tm, nj),
        in_specs=[pl.BlockSpec((tm, o_nsa.shape[1]), lambda i, j: (i, 0)),
                  pl.BlockSpec((tm, y.shape[1]), lambda i, j: (i, 0)),
                  pl.BlockSpec((w_a.shape[0], tn), lambda i, j: (0, j)),
                  pl.BlockSpec((w_b.shape[0], tn), lambda i, j: (0, j)),
                  pl.BlockSpec((tm, tn), lambda i, j: (i, j)),
                  pl.BlockSpec((tm, tn), lambda i, j, nj=nj: (i, j + nj))],
        out_specs=pl.BlockSpec((tm, tn), lambda i, j: (i, j)),
        out_shape=jax.ShapeDtypeStruct((M, N), BF16),
        compiler_params=_cparams(("parallel", "parallel")),
        name=name,
    )(o_nsa, y, w_a, w_b, gates, gates)


def _ffn_up_body(h_ref, wg_ref, wu_ref, cw_ref, cb_ref, o_ref, pad_ref, *, tm, tiles_per_batch):
    i = pl.program_id(1)
    h = h_ref[...]
    gate = _dot(h, wg_ref[...])
    up = _dot(h, wu_ref[...])

    @pl.when(i % tiles_per_batch == 0)
    def _():
        pad_ref[0:SUBLANES, :] = jnp.zeros((SUBLANES, pad_ref.shape[1]), F32)

    pad_ref[SUBLANES:SUBLANES + tm, :] = gate
    y = cb_ref[...]
    for k in range(FFN_CONV):
        off = SUBLANES - (FFN_CONV - 1) + k
        y = y + cw_ref[k:k + 1, :] * pad_ref[off:off + tm, :]
    pad_ref[0:SUBLANES, :] = pad_ref[tm:tm + SUBLANES, :]
    o_ref[...] = (_silu(y) * up).astype(o_ref.dtype)


def _ffn_up(h, w_gu, conv_w, conv_b, *, tm, tn, rows_per_batch, name):
    M, K = h.shape
    nj = D_FF // tn
    return pl.pallas_call(
        functools.partial(_ffn_up_body, tm=tm, tiles_per_batch=rows_per_batch // tm),
        grid=(nj, M // tm),
        in_specs=[pl.BlockSpec((tm, K), lambda j, i: (i, 0)),
                  pl.BlockSpec((K, tn), lambda j, i: (0, j)),
                  pl.BlockSpec((K, tn), lambda j, i, nj=nj: (0, j + nj)),
                  pl.BlockSpec((FFN_CONV, tn), lambda j, i: (0, j)),
                  pl.BlockSpec((1, tn), lambda j, i: (0, j))],
        out_specs=pl.BlockSpec((tm, tn), lambda j, i: (i, j)),
        out_shape=jax.ShapeDtypeStruct((M, D_FF), BF16),
        scratch_shapes=[pltpu.VMEM((tm + SUBLANES, tn), F32)],
        compiler_params=_cparams(("arbitrary", "arbitrary")),
        name=name,
    )(h, w_gu, w_gu, conv_w, conv_b.reshape(1, D_FF))


def _compress_body(x_ref, pe_ref, w1_ref, w2_ref, o_ref):
    xb = (x_ref[...] + pe_ref[...]).astype(BF16)
    hid = _silu(_dot(xb, w1_ref[...]))
    o_ref[...] = _dot(hid.astype(BF16), w2_ref[...])


def _compress(x, pe, w1, w2, *, tm, name):
    rows, kdim = x.shape
    return pl.pallas_call(
        _compress_body,
        grid=(rows // tm,),
        in_specs=[pl.BlockSpec((tm, kdim), lambda i: (i, 0)), pl.BlockSpec((1, kdim), lambda i: (0, 0)),
                  pl.BlockSpec((kdim, CMP_HID), lambda i: (0, 0)),
                  pl.BlockSpec((CMP_HID, HEAD_DIM), lambda i: (0, 0))],
        out_specs=pl.BlockSpec((tm, HEAD_DIM), lambda i: (i, 0)),
        out_shape=jax.ShapeDtypeStruct((rows, HEAD_DIM), F32),
        compiler_params=_cparams(("parallel",)),
        name=name,
    )(x, pe.reshape(1, kdim), w1, w2)


def _softmax_step(k_tile, vt_tile, addend, q, m_ref, l_ref, acc_ref):
    s = _dot(k_tile, q) + addend
    m_old = m_ref[...]
    m_new = jnp.maximum(m_old, jnp.max(s, axis=0, keepdims=True))
    p = jnp.exp(s - m_new)
    alpha = jnp.exp(m_old - m_new)
    l_ref[...] = l_ref[...] * alpha + jnp.sum(p, axis=0, keepdims=True)
    acc_ref[...] = acc_ref[...] * alpha + _dot(vt_tile, p.astype(BF16))
    m_ref[...] = m_new


def _nsa_prompt_body(q_ref, ksel_ref, vsel_ref, kwin_ref, vwin_ref, kc_ref, vct_ref, gate_ref,
                     tz_ref, band_ref, cfar_ref, o_ref, m_ref, l_ref, acc_ref, sel0_ref, selc_ref,
                     *, n_blocks, n_cmp, top_n):
    qi = pl.program_id(2)
    q0 = qi * TQ
    q = q_ref[...]
    width = Q_PER_KV * TQ
    cfar = cfar_ref[...]

    half = n_cmp // 2
    row = lax.broadcasted_iota(jnp.int32, (n_cmp, 16), 0)
    cidx = jnp.where(row < half, 2 * row, 2 * (row - half) + 1)
    crel = cidx - (TQ // L_CMP) * qi
    cat = jnp.where(crel <= -5, 8, jnp.where(crel >= 4, 9, crel + 4))
    place = (cat == lax.broadcasted_iota(jnp.int32, (n_cmp, 16), 1)).astype(F32)
    bias_c = jnp.dot(place, band_ref[...], precision=HIGHEST, preferred_element_type=F32)
    s = _dot(kc_ref[...], q) + bias_c
    m = jnp.max(s, axis=0, keepdims=True)
    e = jnp.exp(s - m)
    denom = jnp.sum(e, axis=0, keepdims=True)
    anyvalid = (m > 0.5 * NEG_INF).astype(F32)
    p = e * (anyvalid / denom)
    o_cmp = _dot(vct_ref[...], p.astype(BF16))
    psum = p[:, 0:TQ]
    for r in range(1, Q_PER_KV):
        psum = psum + p[:, r * TQ:(r + 1) * TQ]
    imp = psum[0:half, :] + psum[half:n_cmp, :]

    blk = lax.broadcasted_iota(jnp.int32, (n_blocks, TQ), 0)
    blk_f = blk.astype(F32)
    qpos = q0 + lax.broadcasted_iota(jnp.int32, (n_blocks, TQ), 1)
    forced = (blk == 0) | (blk == lax.shift_right_logical(qpos, 6))
    valid = blk * L_SEL <= qpos
    score = jnp.where(forced, 1e30, jnp.where(valid, imp, -1.0))
    picked = jnp.zeros((n_blocks, TQ), jnp.bool_)
    for _ in range(top_n):
        mx = jnp.max(score, axis=0, keepdims=True)
        first = jnp.min(jnp.where(score == mx, blk_f, float(n_blocks)), axis=0, keepdims=True)
        hit = blk_f == first
        picked = picked | hit
        score = jnp.where(hit, -2.0, score)
    seladd = jnp.where(picked, 0.0, NEG_INF)
    seladd = jnp.concatenate([seladd] * Q_PER_KV, axis=1)
    sel0_ref[...] = seladd
    selc_ref[...] = seladd + cfar

    def reset():
        m_ref[...] = jnp.full((1, width), NEG_INF, F32)
        l_ref[...] = jnp.zeros((1, width), F32)
        acc_ref[...] = jnp.zeros((HEAD_DIM, width), F32)

    def sel_rows(ref, kt):
        r0 = ref[pl.ds(2 * kt, 1), :]
        r1 = ref[pl.ds(2 * kt + 1, 1), :]
        half_t = TQ // 2
        return jnp.concatenate([jnp.broadcast_to(r0, (half_t, width)),
                                jnp.broadcast_to(r1, (half_t, width))], axis=0)

    reset()

    def far_body(kt, carry):
        _softmax_step(ksel_ref[kt], vsel_ref[kt], sel_rows(selc_ref, kt), q, m_ref, l_ref, acc_ref)
        return carry

    lax.fori_loop(0, jnp.maximum(qi - 1, 0), far_body, 0)

    @pl.when(qi >= 1)
    def _():
        kt = qi - 1
        _softmax_step(ksel_ref[kt], vsel_ref[kt], sel_rows(sel0_ref, kt) + tz_ref[1], q,
                      m_ref, l_ref, acc_ref)

    _softmax_step(ksel_ref[qi], vsel_ref[qi], sel_rows(sel0_ref, qi) + tz_ref[0], q,
                  m_ref, l_ref, acc_ref)
    o_sel = acc_ref[...] * (1.0 / l_ref[...])

    reset()
    for d in range(WINDOW // TQ, -1, -1):
        @pl.when(qi >= d)
        def _(d=d):
            kt = qi - d
            _softmax_step(kwin_ref[kt], vwin_ref[kt], tz_ref[d], q, m_ref, l_ref, acc_ref)
    o_win = acc_ref[...] * (1.0 / l_ref[...])

    g = jax.nn.sigmoid(gate_ref[...])
    o = g[0:1, :] * o_cmp + g[1:2, :] * o_sel + g[2:3, :] * o_win
    for r in range(Q_PER_KV):
        o_ref[r * HEAD_DIM:(r + 1) * HEAD_DIM, :] = o[:, r * TQ:(r + 1) * TQ].astype(o_ref.dtype)


def _nsa_prompt(q2, ksel, vsel, kwin, vwin, kc, vct, gates, tz, band, cfar, *, name):
    B, G, nq, D, width = q2.shape
    nkt = ksel.shape[2]
    n_cmp = kc.shape[2]
    n_blocks = n_cmp // (L_SEL // L_CMP)
    T = nq * TQ
    kv_spec = pl.BlockSpec((None, None, nkt, TQ, D), lambda b, g, i: (b, g, 0, 0, 0))
    vt_spec = pl.BlockSpec((None, None, nkt, D, TQ), lambda b, g, i: (b, g, 0, 0, 0))
    return pl.pallas_call(
        functools.partial(_nsa_prompt_body, n_blocks=n_blocks, n_cmp=n_cmp,
                          top_n=min(TOP_N, n_blocks)),
        grid=(B, G, nq),
        in_specs=[pl.BlockSpec((None, None, None, D, width), lambda b, g, i: (b, g, i, 0, 0)),
                  kv_spec, vt_spec, kv_spec, vt_spec,
                  pl.BlockSpec((None, None, n_cmp, D), lambda b, g, i: (b, g, 0, 0)),
                  pl.BlockSpec((None, None, D, n_cmp), lambda b, g, i: (b, g, 0, 0)),
                  pl.BlockSpec((None, None, None, 3, width), lambda b, g, i: (b, g, i, 0, 0)),
                  pl.BlockSpec((None, WINDOW // TQ + 1, TQ, width), lambda b, g, i: (g, 0, 0, 0)),
                  pl.BlockSpec((None, 16, width), lambda b, g, i: (g, 0, 0)),
                  pl.BlockSpec((None, 1, width), lambda b, g, i: (g, 0, 0))],
        out_specs=pl.BlockSpec((None, Q_PER_KV * D, TQ), lambda b, g, i: (b, g, i)),
        out_shape=jax.ShapeDtypeStruct((B, G * Q_PER_KV * D, T), BF16),
        scratch_shapes=[pltpu.VMEM((1, width), F32), pltpu.VMEM((1, width), F32),
                        pltpu.VMEM((D, width), F32), pltpu.VMEM((n_blocks, width), F32),
                        pltpu.VMEM((n_blocks, width), F32)],
        compiler_params=_cparams(("arbitrary", "arbitrary", "arbitrary")),
        name=name,
    )(q2, ksel, vsel, kwin, vwin, kc, vct, gates, tz, band, cfar)


def _ssd_body(xbc_ref, dt_ref, z_ref, cw_ref, cb_ref, dtb_ref, alog_ref, dskip_ref, ng_ref, exp_ref,
              y_ref, st_ref, pad_ref, x_scr, bc_scr, acst_scr, dtt_scr, y_scr):
    c = pl.program_id(1)
    L = SSM_CHUNK
    heads_per_group = SSM_HEADS // SSM_GROUPS
    n_pairs = SSM_HEADS // 2

    @pl.when(c == 0)
    def _():
        pad_ref[0:SUBLANES, :] = jnp.zeros((SUBLANES, CONV_DIM), F32)
        st_ref[...] = jnp.zeros(st_ref.shape, F32)

    pad_ref[SUBLANES:SUBLANES + L, :] = xbc_ref[...]
    conv = cb_ref[...]
    for k in range(SSM_CONV):
        off = SUBLANES - (SSM_CONV - 1) + k
        conv = conv + cw_ref[k:k + 1, :] * pad_ref[off:off + L, :]
    pad_ref[0:SUBLANES, :] = pad_ref[L:L + SUBLANES, :]
    xbc = _silu(conv)
    for hp in range(n_pairs):
        x_scr[hp] = xbc[:, hp * LANES:(hp + 1) * LANES]
    for g in range(2 * SSM_GROUPS):
        bc_scr[g] = xbc[:, D_INNER + g * D_STATE:D_INNER + (g + 1) * D_STATE].astype(BF16)

    x = dt_ref[:, 0:SSM_HEADS] + dtb_ref[...]
    dt = jnp.maximum(x, 0.0) + jnp.log1p(jnp.exp(-jnp.abs(x)))
    a = dt * (-jnp.exp(alog_ref[...]))
    tri = lax.broadcasted_iota(jnp.int32, (L, L), 0) >= lax.broadcasted_iota(jnp.int32, (L, L), 1)
    a_cs = jnp.dot(tri.astype(F32), a, precision=HIGHEST, preferred_element_type=F32)
    acst_scr[...] = a_cs.T
    dtt_scr[...] = dt.T

    def pair_body(hp, carry):
        g = hp // (heads_per_group // 2)
        b_g = bc_scr[g]
        c_g = bc_scr[SSM_GROUPS + g]
        cb = _dot_nt(c_g, b_g)
        x2 = x_scr[hp]
        ys = []
        for e in range(2):
            h = 2 * hp + e
            ex = exp_ref[h]
            col = jnp.dot(a_cs, ex, precision=HIGHEST, preferred_element_type=F32)
            dtcol = jnp.dot(dt, ex, precision=HIGHEST, preferred_element_type=F32)
            seg = col - acst_scr[pl.ds(h, 1), :]
            decay = jnp.exp(jnp.where(tri, seg, NEG_INF))
            w = cb * decay * dtt_scr[pl.ds(h, 1), :]
            xh = x2[:, e * SSM_HEADDIM:(e + 1) * SSM_HEADDIM]
            y_diag = _dot(w.astype(BF16), xh.astype(BF16))
            h_prev = st_ref[h]
            colp = col[:, 0:SSM_HEADDIM]
            y_off = _dot_nt(c_g, h_prev.astype(BF16)) * jnp.exp(colp)
            ys.append(y_diag + y_off)
            last = col[L - 1:L, :]
            decay_end = jnp.exp(last[:, 0:SSM_HEADDIM] - colp)
            xw = (decay_end * dtcol[:, 0:SSM_HEADDIM]) * xh
            st = _dot_tn(xw.astype(BF16), b_g)
            st_ref[h] = h_prev * jnp.exp(last) + st
        y_scr[hp] = jnp.concatenate(ys, axis=1)
        return carry

    lax.fori_loop(0, n_pairs, pair_body, 0)

    group_w = D_INNER // SSM_GROUPS
    pairs_per_group = group_w // LANES
    for g in range(SSM_GROUPS):
        parts = []
        ssq = jnp.zeros((L, 1), F32)
        for j in range(pairs_per_group):
            hp = g * pairs_per_group + j
            lo = hp * LANES
            y = y_scr[hp] + dskip_ref[:, lo:lo + LANES] * x_scr[hp]
            y = y * _silu(z_ref[:, lo:lo + LANES])
            parts.append(y)
            ssq = ssq + jnp.sum(y * y, axis=-1, keepdims=True)
        inv = lax.rsqrt(ssq * (1.0 / group_w) + EPS)
        for j in range(pairs_per_group):
            lo = (g * pairs_per_group + j) * LANES
            y_ref[:, lo:lo + LANES] = (parts[j] * inv * ng_ref[:, lo:lo + LANES]).astype(y_ref.dtype)


def _ssd_prompt(xbc, dtg, z, conv_w, conv_b, dt_bias, a_log, d_skip, norm_g, *, batch, name):
    M = xbc.shape[0]
    L = SSM_CHUNK
    nc = M // batch // L
    expand = (jnp.arange(SSM_HEADS)[:, None, None] == jnp.arange(SSM_HEADS)[None, :, None])
    expand = jnp.broadcast_to(expand, (SSM_HEADS, SSM_HEADS, LANES)).astype(F32)
    row = lambda v: v.reshape(1, -1)
    const2 = lambda shape: pl.BlockSpec(shape, lambda b, c: (0, 0))
    return pl.pallas_call(
        _ssd_body,
        grid=(batch, nc),
        in_specs=[pl.BlockSpec((L, CONV_DIM), lambda b, c: (b * nc + c, 0)),
                  pl.BlockSpec((L, LANES), lambda b, c: (b * nc + c, 1)),
                  pl.BlockSpec((L, D_INNER), lambda b, c: (b * nc + c, 0)),
                  const2((SSM_CONV, CONV_DIM)), const2((1, CONV_DIM)),
                  const2((1, SSM_HEADS)), const2((1, SSM_HEADS)),
                  const2((1, D_INNER)), const2((1, D_INNER)),
                  pl.BlockSpec((SSM_HEADS, SSM_HEADS, LANES), lambda b, c: (0, 0, 0))],
        out_specs=[pl.BlockSpec((L, D_INNER), lambda b, c: (b * nc + c, 0)),
                   pl.BlockSpec((None, SSM_HEADS, SSM_HEADDIM, D_STATE), lambda b, c: (b, 0, 0, 0))],
        out_shape=[jax.ShapeDtypeStruct((M, D_INNER), BF16),
                   jax.ShapeDtypeStruct((batch, SSM_HEADS, SSM_HEADDIM, D_STATE), F32)],
        scratch_shapes=[pltpu.VMEM((L + SUBLANES, CONV_DIM), F32),
                        pltpu.VMEM((SSM_HEADS // 2, L, LANES), F32),
                        pltpu.VMEM((2 * SSM_GROUPS, L, D_STATE), BF16),
                        pltpu.VMEM((SSM_HEADS, L), F32),
                        pltpu.VMEM((SSM_HEADS, L), F32),
                        pltpu.VMEM((SSM_HEADS // 2, L, LANES), F32)],
        compiler_params=_cparams(("arbitrary", "arbitrary")),
        name=name,
    )(xbc, dtg, z, conv_w, row(conv_b), row(dt_bias), row(a_log),
      row(jnp.repeat(d_skip, SSM_HEADDIM)), row(norm_g), expand)


def _t5_bucket(dist):
    n = jnp.maximum(dist, 0)
    max_exact = N_BUCKETS // 2
    nf = jnp.maximum(n, max_exact).astype(F32)
    large = max_exact + (jnp.log(nf / max_exact) / math.log(MAX_DISTANCE / max_exact)
                         * (N_BUCKETS - max_exact)).astype(jnp.int32)
    return jnp.where(n < max_exact, n, jnp.minimum(large, N_BUCKETS - 1))


def _bias_tables(rel_bias):
    n_off = WINDOW // TQ + 1
    by_dist = rel_bias[_t5_bucket(jnp.arange(n_off * TQ + TQ))]
    key = jnp.arange(TQ)[:, None]
    qry = jnp.arange(TQ)[None, :]
    delta = (jnp.arange(n_off) * TQ)[:, None, None]
    dist = delta + qry[None] - key[None]
    ok = (dist >= 0) & (dist <= WINDOW)
    tz = jnp.where(ok[..., None], by_dist[jnp.clip(dist, 0)], NEG_INF)
    tz = tz.reshape(n_off, TQ, TQ, N_KV, Q_PER_KV).transpose(3, 0, 1, 4, 2)
    tz = tz.reshape(N_KV, n_off, TQ, Q_PER_KV * TQ)
    far = rel_bias[N_BUCKETS - 1]
    rel = jnp.arange(8)[:, None] - 4
    cdist = qry - L_CMP * rel - (L_CMP - 1)
    band = jnp.where((cdist >= 0)[..., None], by_dist[jnp.clip(cdist, 0)], NEG_INF)
    far_rows = jnp.broadcast_to(far[None, None, :], (1, TQ, N_HEADS))
    band = jnp.concatenate([band, far_rows, jnp.full((1, TQ, N_HEADS), NEG_INF, F32),
                            jnp.zeros((6, TQ, N_HEADS), F32)], axis=0)
    band = band.reshape(16, TQ, N_KV, Q_PER_KV).transpose(2, 0, 3, 1).reshape(N_KV, 16, Q_PER_KV * TQ)
    cfar = jnp.broadcast_to(far.reshape(N_KV, 1, Q_PER_KV, 1), (N_KV, 1, Q_PER_KV, TQ))
    return tz, band, cfar.reshape(N_KV, 1, Q_PER_KV * TQ)


def _layer_weights(l, w_ada, w_in, cmp_w1, cmp_w2, w_a, w_b, w_o, w_gu, w_down):
    wi = w_in[l]
    small = jnp.zeros((D_MODEL, 2 * LANES), F32)
    small = small.at[:, 0:NSA_GATE_COLS].set(wi[:, OFF_G:OFF_Z])
    small = small.at[:, LANES:LANES + SSM_HEADS].set(wi[:, OFF_DT:OFF_MG])
    c = lambda w: w.astype(BF16)
    return dict(ada=c(w_ada[l]), q=c(wi[:, 0:OFF_KV]), kv=c(wi[:, OFF_KV:OFF_G]), small=c(small),
                z=c(wi[:, OFF_Z:OFF_XBC]), xbc=c(wi[:, OFF_XBC:OFF_DT]), mg=c(wi[:, OFF_MG:P_IN]),
                cmp_w1=c(cmp_w1[l]), cmp_w2=c(cmp_w2[l]), a=c(w_a[l]), b=c(w_b[l]), o=c(w_o[l]),
                gu=c(w_gu[l]), down=c(w_down[l]))


def _prompt_layer(x, mod, l, Wl, P, tables, batch, T):
    M = x.shape[0]
    sh1, sc1, g1, sh2, sc2, g2 = [m.reshape(batch, 1, D_MODEL) for m in jnp.split(mod, 6, axis=-1)]
    tm = 1024 if T % 1024 == 0 else 512
    h = _norm_mod(x, P['norm1_g'][l], sc1, sh1, tm=512, rows_per_batch=T, name=f"norm1_p{l}")
    q = _mm(h, Wl['q'], tm=tm, tn=512, name=f"proj_q_p{l}", out_dtype=BF16,
            epilogue=lambda acc: acc * SCALE)
    kv = _mm(h, Wl['kv'], tm=tm, tn=512, name=f"proj_kv_p{l}")
    small = _mm(h, Wl['small'], tm=tm, tn=2 * LANES, name=f"proj_small_p{l}")
    z = _mm(h, Wl['z'], tm=tm, tn=512, name=f"proj_z_p{l}")
    xbc = _mm(h, Wl['xbc'], tm=tm, tn=512, name=f"proj_xbc_p{l}")
    gates = _mm(h, Wl['mg'], tm=tm, tn=512, name=f"proj_mg_p{l}", epilogue=jax.nn.sigmoid)

    o_nsa = _prompt_attention(q, kv, small, l, Wl, P, tables, batch, T)
    kv6 = kv.reshape(batch, T, 6, N_KV, HEAD_DIM)

    y, ssm_state = _ssd_prompt(xbc, small, z, P['ssm_conv_w'][l], P['ssm_conv_b'][l], P['dt_bias'][l],
                               P['a_log'][l], P['d_skip'][l], P['ssm_norm_g'][l], batch=batch, name=f"ssd_p{l}")

    u = _merge(o_nsa, y, Wl['a'], Wl['b'], gates, tm=tm, tn=512, name=f"merge_p{l}")
    x = _mm(u, Wl['o'], tm=tm, tn=512, name=f"proj_o_p{l}", rows_per_batch=T,
            epilogue=lambda acc, xr, gr: xr + gr * acc, extras=[('full', x), ('batch', g1)])

    h2 = _norm_mod(x, P['norm2_g'][l], sc2, sh2, tm=512, rows_per_batch=T, name=f"norm2_p{l}")
    act = _ffn_up(h2, Wl['gu'], P['ffn_conv_w'][l], P['ffn_conv_b'][l], tm=512, tn=D_FF // 2,
                  rows_per_batch=T, name=f"ffn_up_p{l}")
    h2_last = h2.reshape(batch, T, D_MODEL)[:, T - SUBLANES:].reshape(batch * SUBLANES, D_MODEL)
    gate_last = _mm(h2_last, Wl['gu'][:, 0:D_FF], tm=batch * SUBLANES, tn=D_FF // 2, name=f"ffn_state_p{l}")
    ffn_state = gate_last.reshape(batch, SUBLANES, D_FF)[:, SUBLANES - (FFN_CONV - 1):]
    x = _mm(act, Wl['down'], tm=tm, tn=512, name=f"ffn_down_p{l}", rows_per_batch=T,
            epilogue=lambda acc, xr, gr: xr + gr * acc, extras=[('full', x), ('batch', g2)])

    kv_rows = kv6[:, :, :4]
    win_state = kv6[:, T - min(WINDOW, T):, 4:]
    conv_state = xbc.reshape(batch, T, CONV_DIM)[:, T - (SSM_CONV - 1):]
    return x, (kv_rows, win_state, conv_state, ssm_state, ffn_state)


def _prompt_attention(q, kv, small, l, Wl, P, tables, batch, T):
    nq = T // TQ
    kv6 = kv.reshape(batch, T, 6, N_KV, HEAD_DIM)
    n_cmp = T // L_CMP

    def cmp_in(slot):
        xb = kv6[:, :, slot].reshape(batch, n_cmp, L_CMP, N_KV, HEAD_DIM).transpose(0, 3, 1, 2, 4)
        return xb.reshape(batch * N_KV * n_cmp, L_CMP * HEAD_DIM)

    ctm = min(512, batch * N_KV * n_cmp)
    kc = _compress(cmp_in(0), P['cmp_pe'][l, 0], Wl['cmp_w1'][0], Wl['cmp_w2'][0], tm=ctm, name=f"cmp_k_p{l}")
    vc = _compress(cmp_in(1), P['cmp_pe'][l, 1], Wl['cmp_w1'][1], Wl['cmp_w2'][1], tm=ctm, name=f"cmp_v_p{l}")

    def even_odd(c):
        c = c.reshape(batch, N_KV, n_cmp // 2, 2, HEAD_DIM).transpose(0, 1, 3, 2, 4)
        return c.reshape(batch, N_KV, n_cmp, HEAD_DIM)

    kc = even_odd(kc.reshape(batch, N_KV, n_cmp, HEAD_DIM)).astype(BF16)
    vct = even_odd(vc.reshape(batch, N_KV, n_cmp, HEAD_DIM)).astype(BF16).transpose(0, 1, 3, 2)

    def k_tiles(slot):
        return kv6[:, :, slot].astype(BF16).reshape(batch, nq, TQ, N_KV, HEAD_DIM).transpose(0, 3, 1, 2, 4)

    def vt_tiles(slot):
        return kv6[:, :, slot].astype(BF16).reshape(batch, nq, TQ, N_KV, HEAD_DIM).transpose(0, 3, 1, 4, 2)

    q2 = q.reshape(batch, nq, TQ, N_KV, Q_PER_KV, HEAD_DIM).transpose(0, 3, 1, 5, 4, 2)
    q2 = q2.reshape(batch, N_KV, nq, HEAD_DIM, Q_PER_KV * TQ)
    ng = small[:, 0:NSA_GATE_COLS].reshape(batch, nq, TQ, 3, N_KV, Q_PER_KV).transpose(0, 4, 1, 3, 5, 2)
    ng = ng.reshape(batch, N_KV, nq, 3, Q_PER_KV * TQ)
    tz, band, cfar = tables
    o_t = _nsa_prompt(q2, k_tiles(2), vt_tiles(3), k_tiles(4), vt_tiles(5), kc, vct, ng, tz, band, cfar,
                      name=f"nsa_p{l}")
    return o_t.transpose(0, 2, 1).reshape(batch * T, Q_COLS)


def _causal_dwconv(x, prev, w, b):
    K = w.shape[0]
    T = x.shape[1]
    xp = jnp.concatenate([prev.astype(x.dtype), x], axis=1)
    y = b
    for k in range(K):
        y = y + w[k] * xp[:, k:k + T]
    return y, xp[:, T:]


def _rms_norm(x, g):
    return x * lax.rsqrt(jnp.mean(x * x, axis=-1, keepdims=True) + EPS) * g


def _sample_layer(x, mod, l, Wl, P, pos0, past_kv, win_buf, conv_prev, ssm_h0, ffn_prev):
    B = x.shape[0]
    sh1, sc1, g1, sh2, sc2, g2 = jnp.split(mod, 6, axis=-1)
    h = _norm_mod(x, P['norm1_g'][l], sc1, sh1, tm=B, rows_per_batch=1, name=f"norm1_s{l}")
    q = _mm(h, Wl['q'], tm=B, tn=512, name=f"proj_q_s{l}")
    kv = _mm(h, Wl['kv'], tm=B, tn=512, name=f"proj_kv_s{l}")
    small = _mm(h, Wl['small'], tm=B, tn=2 * LANES, name=f"proj_small_s{l}")
    z = _mm(h, Wl['z'], tm=B, tn=512, name=f"proj_z_s{l}")
    xbc = _mm(h, Wl['xbc'], tm=B, tn=512, name=f"proj_xbc_s{l}")
    gates = _mm(h, Wl['mg'], tm=B, tn=512, name=f"proj_mg_s{l}", epilogue=jax.nn.sigmoid)
    nsa_g = small[:, 0:NSA_GATE_COLS]
    dt_raw = small[:, LANES:LANES + SSM_HEADS]

    T = 1
    q = q.reshape(B, T, N_KV, Q_PER_KV, HEAD_DIM)
    kv = kv.reshape(B, T, 6, N_KV, HEAD_DIM)
    q_pos = pos0 + jnp.arange(T)
    rel_bias = P['rel_bias']
    new_rows = kv[:, :, :4]
    pad = (-(pos0 + T)) % L_SEL
    parts = [past_kv, new_rows]
    if pad:
        parts.append(jnp.zeros((B, pad) + new_rows.shape[2:], kv.dtype))
    full = jnp.concatenate(parts, axis=1)
    Tk = full.shape[1]
    n_cmp = Tk // L_CMP

    def cmp(slot):
        xb = full[:, :, slot].reshape(B, n_cmp, L_CMP, N_KV, HEAD_DIM).transpose(0, 3, 1, 2, 4)
        rows = B * N_KV * n_cmp
        out = _compress(xb.reshape(rows, L_CMP * HEAD_DIM), P['cmp_pe'][l, slot], Wl['cmp_w1'][slot],
                        Wl['cmp_w2'][slot], tm=512 if rows % 512 == 0 else rows, name=f"cmp{slot}_s{l}")
        return out.reshape(B, N_KV, n_cmp, HEAD_DIM).transpose(0, 2, 1, 3)

    kc, vc = cmp(0), cmp(1)
    k_end = jnp.arange(n_cmp) * L_CMP + (L_CMP - 1)
    dist = q_pos[:, None] - k_end[None, :]
    bias = rel_bias[_t5_bucket(dist)].reshape(dist.shape + (N_KV, Q_PER_KV)).transpose(2, 3, 0, 1)
    logits = jnp.einsum('bqgrd,bcgd->bgrqc', q, kc).astype(F32) * SCALE + bias
    valid = dist >= 0
    p_cmp = jnp.where(valid, jax.nn.softmax(jnp.where(valid, logits, NEG_INF), axis=-1), 0.0)
    o_cmp = jnp.einsum('bgrqc,bcgd->bqgrd', p_cmp, vc)
    n_blocks = Tk // L_SEL
    imp = p_cmp.sum(axis=2)
    imp = imp.reshape(B, N_KV, T, n_blocks, L_SEL // L_CMP).sum(-1)
    blk = jnp.arange(n_blocks)
    forced = (blk[None] == (q_pos // L_SEL)[:, None]) | (blk[None] == 0)
    bvalid = blk[None] * L_SEL <= q_pos[:, None]
    score = jnp.where(forced, jnp.inf, jnp.where(bvalid, imp, -jnp.inf))
    _, idx = lax.top_k(score, min(TOP_N, n_blocks))
    kb = full[:, :, 2].reshape(B, n_blocks, L_SEL, N_KV, HEAD_DIM).transpose(0, 3, 1, 2, 4)
    vb = full[:, :, 3].reshape(B, n_blocks, L_SEL, N_KV, HEAD_DIM).transpose(0, 3, 1, 2, 4)
    bi = jnp.arange(B)[:, None, None, None]
    gi = jnp.arange(N_KV)[None, :, None, None]
    kg = kb[bi, gi, idx]
    vg = vb[bi, gi, idx]
    k_pos = idx[..., None] * L_SEL + jnp.arange(L_SEL)
    sdist = q_pos[None, None, :, None, None] - k_pos
    rb = rel_bias.reshape(N_BUCKETS, N_KV, Q_PER_KV).transpose(1, 0, 2)
    sbias = rb[gi[..., None], _t5_bucket(sdist)].transpose(0, 1, 5, 2, 3, 4)
    slog = jnp.einsum('bqgrd,bgqnld->bgrqnl', q, kg).astype(F32) * SCALE + sbias
    slog = jnp.where((sdist >= 0)[:, :, None], slog, NEG_INF)
    slog = slog.reshape(slog.shape[:4] + (-1,))
    sp = jax.nn.softmax(slog, axis=-1)
    vg = vg.reshape(vg.shape[:3] + (-1, vg.shape[-1]))
    o_sel = jnp.einsum('bgrqk,bgqkd->bqgrd', sp, vg)

    win_new = kv[:, :, 4:]
    Lb = win_buf.shape[1]
    wk = jnp.concatenate([win_buf, win_new], axis=1)
    wk_pos = pos0 - Lb + jnp.arange(Lb + T)
    wdist = q_pos[:, None] - wk_pos[None, :]
    wmask = (wdist >= 0) & (wdist <= WINDOW) & (wk_pos[None, :] >= 0)
    wbias = rel_bias[_t5_bucket(wdist)].reshape(wdist.shape + (N_KV, Q_PER_KV)).transpose(2, 3, 0, 1)
    wlog = jnp.einsum('bqgrd,bkgd->bgrqk', q, wk[:, :, 0]).astype(F32) * SCALE + wbias
    wp = jax.nn.softmax(jnp.where(wmask, wlog, NEG_INF), axis=-1)
    o_win = jnp.einsum('bgrqk,bkgd->bqgrd', wp, wk[:, :, 1])
    win_state = wk[:, T:]
    g = jax.nn.sigmoid(nsa_g).reshape(B, T, 3, N_KV, Q_PER_KV)[..., None]
    o_nsa = (g[:, :, 0] * o_cmp + g[:, :, 1] * o_sel + g[:, :, 2] * o_win).reshape(B, Q_COLS)

    xbc3, conv_state = _causal_dwconv(xbc.reshape(B, T, CONV_DIM), conv_prev, P['ssm_conv_w'][l], P['ssm_conv_b'][l])
    xbc3 = _silu(xbc3)[:, 0]
    xs, bm, cm = jnp.split(xbc3, (D_INNER, D_INNER + SSM_GROUPS * D_STATE), axis=-1)
    dt = jax.nn.softplus(dt_raw + P['dt_bias'][l])
    a = -jnp.exp(P['a_log'][l])
    R = SSM_HEADS // SSM_GROUPS
    xh = xs.reshape(B, SSM_GROUPS, R, SSM_HEADDIM)
    dth = dt.reshape(B, SSM_GROUPS, R)
    bmg = bm.reshape(B, SSM_GROUPS, D_STATE)
    cmg = cm.reshape(B, SSM_GROUPS, D_STATE)
    a1 = dth * a.reshape(SSM_GROUPS, R)
    h0 = ssm_h0.reshape(B, SSM_GROUPS, R, SSM_HEADDIM, D_STATE)
    st = jnp.einsum('bgn,bgrp->bgrpn', bmg, (dth[..., None] * xh))
    h_new = h0 * jnp.exp(a1)[..., None, None] + st
    cbv = jnp.einsum('bgn,bgn->bg', cmg, bmg)
    y_diag = (cbv[:, :, None] * dth)[..., None] * xh
    y_off = jnp.einsum('bgn,bgrpn->bgrp', cmg, h0) * jnp.exp(a1)[..., None]
    y = y_diag + y_off
    y = y + P['d_skip'][l].reshape(SSM_GROUPS, R)[None, :, :, None] * xh
    y = y.reshape(B, D_INNER) * _silu(z)
    y = _rms_norm(y.reshape(B, SSM_GROUPS, D_INNER // SSM_GROUPS),
                  P['ssm_norm_g'][l].reshape(SSM_GROUPS, -1)).reshape(B, D_INNER)
    ssm_state = h_new.reshape(B, SSM_HEADS, SSM_HEADDIM, D_STATE)

    u = _merge(o_nsa.astype(BF16), y.astype(BF16), Wl['a'], Wl['b'], gates, tm=B, tn=512, name=f"merge_s{l}")
    x = _mm(u, Wl['o'], tm=B, tn=512, name=f"proj_o_s{l}",
            epilogue=lambda acc, xr, gr: xr + gr * acc, extras=[('full', x), ('full', g1)])

    h2 = _norm_mod(x, P['norm2_g'][l], sc2, sh2, tm=B, rows_per_batch=1, name=f"norm2_s{l}")
    gu = _mm(h2, Wl['gu'], tm=B, tn=D_FF // 2, name=f"ffn_gu_s{l}")
    gate, up = gu[:, :D_FF], gu[:, D_FF:]
    gate3, ffn_state = _causal_dwconv(gate.reshape(B, T, D_FF), ffn_prev, P['ffn_conv_w'][l], P['ffn_conv_b'][l])
    act = (_silu(gate3[:, 0]) * up).astype(BF16)
    x = _mm(act, Wl['down'], tm=B, tn=512, name=f"ffn_down_s{l}",
            epilogue=lambda acc, xr, gr: xr + gr * acc, extras=[('full', x), ('full', g2)])
    return x, (new_rows, win_state, conv_state, ssm_state, ffn_state)


def kernel(x_prompt, x_sample, cache_kv, cache_win_kv, state_ssm_conv, state_ssm, state_ffn_conv, page_table, c_prompt, c_sample, w_ada, b_ada, norm1_g, norm2_g, final_g, w_in, rel_bias, cmp_pe, cmp_w1, cmp_w2, ssm_conv_w, ssm_conv_b, dt_bias, a_log, d_skip, ssm_norm_g, w_a, w_b, w_o, w_gu, ffn_conv_w, ffn_conv_b, w_down):
    P = dict(norm1_g=norm1_g, norm2_g=norm2_g, rel_bias=rel_bias, cmp_pe=cmp_pe.reshape(DEPTH, 2, -1),
             ssm_conv_w=ssm_conv_w, ssm_conv_b=ssm_conv_b, dt_bias=dt_bias, a_log=a_log, d_skip=d_skip,
             ssm_norm_g=ssm_norm_g, ffn_conv_w=ffn_conv_w, ffn_conv_b=ffn_conv_b)
    Bp, T, _ = x_prompt.shape
    Bs = x_sample.shape[0]
    past_len = page_table.shape[1] * cache_kv.shape[2]
    tables = _bias_tables(rel_bias)
    n_c = Bp + Bs
    c_rows = -(-n_c // SUBLANES) * SUBLANES
    c_all = jnp.zeros((c_rows, D_MODEL), F32).at[:Bp].set(c_prompt).at[Bp:n_c].set(c_sample).astype(BF16)
    xp = x_prompt.reshape(Bp * T, D_MODEL)
    xs = x_sample.reshape(Bs, D_MODEL)
    st_p, st_s = [], []
    for l in range(DEPTH):
        Wl = _layer_weights(l, w_ada, w_in, cmp_w1, cmp_w2, w_a, w_b, w_o, w_gu, w_down)
        mod = _mm(c_all, Wl['ada'], tm=c_rows, tn=512, name=f"ada{l}",
                  epilogue=lambda acc, b: acc + b, extras=[('col', b_ada[l].reshape(1, -1))])
        xp, sp = _prompt_layer(xp, mod[:Bp], l, Wl, P, tables, Bp, T)
        st_p.append(sp)
        past = cache_kv[l, page_table].reshape(Bs, past_len, 4, N_KV, HEAD_DIM)
        xs, ss = _sample_layer(xs, mod[Bp:n_c], l, Wl, P, past_len, past, cache_win_kv[l],
                               state_ssm_conv[l], state_ssm[l], state_ffn_conv[l])
        st_s.append(ss)
    y_prompt = _final_norm(xp, final_g, tm=512, name="final_norm_p").reshape(Bp, T, D_MODEL)
    y_sample = _final_norm(xs, final_g, tm=Bs, name="final_norm_s").reshape(Bs, 1, D_MODEL)
    kv_p, win_p, conv_p, ssm_p, ffn_p = [jnp.stack([s[i] for s in st_p]) for i in range(5)]
    kv_s, win_s, conv_s, ssm_s, ffn_s = [jnp.stack([s[i] for s in st_s]) for i in range(5)]
    return (y_prompt, y_sample, kv_p, win_p, conv_p, ssm_p, ffn_p, kv_s, win_s, conv_s, ssm_s, ffn_s)
```

```python
import functools
import math

import jax
import jax.numpy as jnp
from jax import lax
from jax.experimental import pallas as pl
from jax.experimental.pallas import tpu as pltpu

D_MODEL = 1024
N_HEADS = 16
N_KV = 4
HEAD_DIM = 64
Q_PER_KV = N_HEADS // N_KV
L_CMP = 32
L_SEL = 64
TOP_N = 16
WINDOW = 512
CMP_HID = 128
Q_BLOCK = 128
SCALE = HEAD_DIM ** -0.5
N_BUCKETS = 32
MAX_DISTANCE = 128
D_INNER = 2 * D_MODEL
SSM_HEADDIM = 64
SSM_HEADS = D_INNER // SSM_HEADDIM
SSM_GROUPS = 4
D_STATE = 128
SSM_CONV = 4
SSM_CHUNK = 128
CONV_DIM = D_INNER + 2 * SSM_GROUPS * D_STATE
D_FF = 2816
FFN_CONV = 3
EPS = 1e-6
NEG_INF = -1e30
DEPTH = 2

Q_COLS = N_HEADS * HEAD_DIM
KV_COLS = 6 * N_KV * HEAD_DIM
NSA_GATE_COLS = 3 * N_HEADS
OFF_KV = Q_COLS
OFF_G = OFF_KV + KV_COLS
OFF_Z = OFF_G + NSA_GATE_COLS
OFF_XBC = OFF_Z + D_INNER
OFF_DT = OFF_XBC + CONV_DIM
OFF_MG = OFF_DT + SSM_HEADS
P_IN = OFF_MG + 2 * D_MODEL

LANES = 128
SUBLANES = 8
BF16_ROWS = 16
TQ = 128
WIDTH = Q_PER_KV * TQ
FAR_TILES = 4
WIN_TILES = WINDOW // TQ + 1
AUG_ROW0 = HEAD_DIM
VMEM_LIMIT = 48 * 1024 * 1024

BF16 = jnp.bfloat16
F32 = jnp.float32


def _cparams(sem):
    return pltpu.CompilerParams(dimension_semantics=sem, vmem_limit_bytes=VMEM_LIMIT)


def _silu(x):
    return x * jax.nn.sigmoid(x)


def _dot(a, b):
    return jnp.dot(a, b, preferred_element_type=F32)


def _dot_nt(a, b):
    return lax.dot_general(a, b, (((1,), (1,)), ((), ())), preferred_element_type=F32)


def _dot_tn(a, b):
    return lax.dot_general(a, b, (((0,), (0,)), ((), ())), preferred_element_type=F32)


def _split3(x):
    hi = x.astype(BF16)
    r1 = x - hi.astype(F32)
    mid = r1.astype(BF16)
    lo = (r1 - mid.astype(F32)).astype(BF16)
    return hi, mid, lo


def _mm_body(a_ref, w_ref, *refs, epilogue, n_extra):
    acc = _dot(a_ref[...], w_ref[...])
    extras = [r[...] for r in refs[:n_extra]]
    o_ref = refs[n_extra]
    o_ref[...] = epilogue(acc, *extras).astype(o_ref.dtype)


def _mm(a, w, *, tm, tn, name, out_dtype=F32, epilogue=None, extras=(), rows_per_batch=None):
    M, K = a.shape
    N = w.shape[1]
    assert M % tm == 0 and N % tn == 0, (M, N, tm, tn)
    in_specs = [pl.BlockSpec((tm, K), lambda i, j: (i, 0)),
                pl.BlockSpec((K, tn), lambda i, j: (0, j))]
    ops = [a, w]
    for kind, arr in extras:
        if kind == 'col':
            in_specs.append(pl.BlockSpec((1, tn), lambda i, j: (0, j)))
        elif kind == 'full':
            in_specs.append(pl.BlockSpec((tm, tn), lambda i, j: (i, j)))
        else:
            assert kind == 'batch' and rows_per_batch % tm == 0
            tpb = rows_per_batch // tm
            in_specs.append(pl.BlockSpec((None, 1, tn), lambda i, j, tpb=tpb: (i // tpb, 0, j)))
        ops.append(arr)
    if epilogue is None:
        epilogue = lambda acc: acc
    return pl.pallas_call(
        functools.partial(_mm_body, epilogue=epilogue, n_extra=len(extras)),
        grid=(M // tm, N // tn),
        in_specs=in_specs,
        out_specs=pl.BlockSpec((tm, tn), lambda i, j: (i, j)),
        out_shape=jax.ShapeDtypeStruct((M, N), out_dtype),
        compiler_params=_cparams(("parallel", "parallel")),
        name=name,
    )(*ops)


def _norm_mod_body(x_ref, g_ref, sc_ref, sh_ref, o_ref):
    x = x_ref[...]
    y = x * lax.rsqrt(jnp.mean(x * x, axis=-1, keepdims=True) + EPS) * g_ref[...]
    o_ref[...] = (y * (1.0 + sc_ref[...]) + sh_ref[...]).astype(o_ref.dtype)


def _norm_mod(x, g, sc, sh, *, tm, rows_per_batch, name):
    M, D = x.shape
    if sc.ndim == 3:
        tpb = rows_per_batch // tm
        mod_spec = pl.BlockSpec((None, 1, D), lambda i: (i // tpb, 0, 0))
    else:
        mod_spec = pl.BlockSpec((tm, D), lambda i: (i, 0))
    return pl.pallas_call(
        _norm_mod_body,
        grid=(M // tm,),
        in_specs=[pl.BlockSpec((tm, D), lambda i: (i, 0)), pl.BlockSpec((1, D), lambda i: (0, 0)),
                  mod_spec, mod_spec],
        out_specs=pl.BlockSpec((tm, D), lambda i: (i, 0)),
        out_shape=jax.ShapeDtypeStruct((M, D), BF16),
        compiler_params=_cparams(("parallel",)),
        name=name,
    )(x, g.reshape(1, D), sc, sh)


def _final_norm_body(x_ref, g_ref, o_ref):
    x = x_ref[...]
    o_ref[...] = x * lax.rsqrt(jnp.mean(x * x, axis=-1, keepdims=True) + EPS) * g_ref[...]


def _final_norm(x, g, *, tm, name):
    M, D = x.shape
    return pl.pallas_call(
        _final_norm_body,
        grid=(M // tm,),
        in_specs=[pl.BlockSpec((tm, D), lambda i: (i, 0)), pl.BlockSpec((1, D), lambda i: (0, 0))],
        out_specs=pl.BlockSpec((tm, D), lambda i: (i, 0)),
        out_shape=jax.ShapeDtypeStruct((M, D), F32),
        compiler_params=_cparams(("parallel",)),
        name=name,
    )(x, g.reshape(1, D))


def _merge_body(o_ref, y_ref, wa_ref, wb_ref, ga_ref, gb_ref, u_ref):
    pa = _dot(o_ref[...], wa_ref[...])
    pb = _dot(y_ref[...], wb_ref[...])
    u_ref[...] = (ga_ref[...] * pa + gb_ref[...] * pb).astype(u_ref.dtype)


def _merge(o_nsa, y, w_a, w_b, gates, *, tm, tn, name):
    M = o_nsa.shape[0]
    N = w_a.shape[1]
    nj = N // tn
    return pl.pallas_call(
        _merge_body,
        grid=(M // tm, nj),
        in_specs=[pl.BlockSpec((tm, o_nsa.shape[1]), lambda i, j: (i, 0)),
                  pl.BlockSpec((tm, y.shape[1]), lambda i, j: (i, 0)),
                  pl.BlockSpec((w_a.shape[0], tn), lambda i, j: (0, j)),
                  pl.BlockSpec((w_b.shape[0], tn), lambda i, j: (0, j)),
                  pl.BlockSpec((tm, tn), lambda i, j: (i, j)),
                  pl.BlockSpec((tm, tn), lambda i, j, nj=nj: (i, j + nj))],
        out_specs=pl.BlockSpec((tm, tn), lambda i, j: (i, j)),
        out_shape=jax.ShapeDtypeStruct((M, N), BF16),
        compiler_params=_cparams(("parallel", "parallel")),
        name=name,
    )(o_nsa, y, w_a, w_b, gates, gates)


def _ffn_up_body(h_ref, wg_ref, wu_ref, cw_ref, cb_ref, o_ref, pad_ref, *, tm, tiles_per_batch):
    i = pl.program_id(1)
    h = h_ref[...]
    gate = _dot(h, wg_ref[...])
    up = _dot(h, wu_ref[...])

    @pl.when(i % tiles_per_batch == 0)
    def _():
        pad_ref[0:SUBLANES, :] = jnp.zeros((SUBLANES, pad_ref.shape[1]), F32)

    pad_ref[SUBLANES:SUBLANES + tm, :] = gate
    y = cb_ref[...]
    for k in range(FFN_CONV):
        off = SUBLANES - (FFN_CONV - 1) + k
        y = y + cw_ref[k:k + 1, :] * pad_ref[off:off + tm, :]
    pad_ref[0:SUBLANES, :] = pad_ref[tm:tm + SUBLANES, :]
    o_ref[...] = (_silu(y) * up).astype(o_ref.dtype)


def _ffn_up(h, w_gu, conv_w, conv_b, *, tm, tn, rows_per_batch, name):
    M, K = h.shape
    nj = D_FF // tn
    return pl.pallas_call(
        functools.partial(_ffn_up_body, tm=tm, tiles_per_batch=rows_per_batch // tm),
        grid=(nj, M // tm),
        in_specs=[pl.BlockSpec((tm, K), lambda j, i: (i, 0)),
                  pl.BlockSpec((K, tn), lambda j, i: (0, j)),
                  pl.BlockSpec((K, tn), lambda j, i, nj=nj: (0, j + nj)),
                  pl.BlockSpec((FFN_CONV, tn), lambda j, i: (0, j)),
                  pl.BlockSpec((1, tn), lambda j, i: (0, j))],
        out_specs=pl.BlockSpec((tm, tn), lambda j, i: (i, j)),
        out_shape=jax.ShapeDtypeStruct((M, D_FF), BF16),
        scratch_shapes=[pltpu.VMEM((tm + SUBLANES, tn), F32)],
        compiler_params=_cparams(("arbitrary", "arbitrary")),
        name=name,
    )(h, w_gu, w_gu, conv_w, conv_b.reshape(1, D_FF))


def _compress_body(x_ref, pe_ref, w1_ref, w2_ref, o_ref):
    xb = (x_ref[...] + pe_ref[...]).astype(BF16)
    hid = _silu(_dot(xb, w1_ref[...]))
    o_ref[...] = _dot(hid.astype(BF16), w2_ref[...])


def _compress(x, pe, w1, w2, *, tm, name):
    rows, kdim = x.shape
    return pl.pallas_call(
        _compress_body,
        grid=(rows // tm,),
        in_specs=[pl.BlockSpec((tm, kdim), lambda i: (i, 0)), pl.BlockSpec((1, kdim), lambda i: (0, 0)),
                  pl.BlockSpec((kdim, CMP_HID), lambda i: (0, 0)),
                  pl.BlockSpec((CMP_HID, HEAD_DIM), lambda i: (0, 0))],
        out_specs=pl.BlockSpec((tm, HEAD_DIM), lambda i: (i, 0)),
        out_shape=jax.ShapeDtypeStruct((rows, HEAD_DIM), F32),
        compiler_params=_cparams(("parallel",)),
        name=name,
    )(x, pe.reshape(1, kdim), w1, w2)


def _attend(s, vt):
    m = jnp.max(s, axis=0, keepdims=True)
    p = jnp.exp(s - m)
    return m, jnp.sum(p, axis=0, keepdims=True), _dot(vt, p.astype(BF16))


def _lane_concat(tiles3):
    return jnp.concatenate([tiles3[t] for t in range(tiles3.shape[0])], axis=1)


def _nsa_prompt_body(q_ref, ksel_ref, vsel_ref, kwin_ref, vwin_ref, kc_ref, vct_ref, gate_ref,
                     tzw_ref, tzn_ref, band_ref, caug_ref, o_ref,
                     qaug_ref, m_ref, l_ref, acc_ref, sel0_ref, selfar_ref, *, n_blocks, n_cmp, top_n):
    qi = pl.program_id(2)
    q0 = qi * TQ
    n_grp_rows = n_blocks // SUBLANES

    @pl.when(qi == 0)
    def _():
        selfar_ref[:, SUBLANES:2 * SUBLANES, :] = jnp.broadcast_to(
            caug_ref[...][None], (selfar_ref.shape[0], SUBLANES, WIDTH))
        selfar_ref[n_grp_rows:, 0:SUBLANES, :] = jnp.full(
            (selfar_ref.shape[0] - n_grp_rows, SUBLANES, WIDTH), NEG_INF, F32)
        sel0_ref[0:SUBLANES, :] = jnp.zeros((SUBLANES, WIDTH), F32)

    qaug_ref[0:HEAD_DIM, :] = q_ref[...]
    qaug_ref[HEAD_DIM:, :] = jnp.zeros((qaug_ref.shape[0] - HEAD_DIM, WIDTH), BF16)
    q = q_ref[...]

    half = n_cmp // 2
    row = lax.broadcasted_iota(jnp.int32, (n_cmp, 16), 0)
    cidx = jnp.where(row < half, 2 * row, 2 * (row - half) + 1)
    crel = cidx - (TQ // L_CMP) * qi
    cat = jnp.where(crel <= -5, 8, jnp.where(crel >= 4, 9, crel + 4))
    place = (cat == lax.broadcasted_iota(jnp.int32, (n_cmp, 16), 1)).astype(BF16)
    b_hi, b_mid, b_lo = _split3(band_ref[...])
    bias_c = _dot(place, b_hi) + _dot(place, b_mid) + _dot(place, b_lo)
    s = _dot(kc_ref[...], q) + bias_c
    m = jnp.max(s, axis=0, keepdims=True)
    e = jnp.exp(s - m)
    denom = jnp.sum(e, axis=0, keepdims=True)
    anyvalid = (m > 0.5 * NEG_INF).astype(F32)
    p = e * (anyvalid / denom)
    o_cmp = _dot(vct_ref[...], p.astype(BF16))
    psum = p[:, 0:TQ]
    for r in range(1, Q_PER_KV):
        psum = psum + p[:, r * TQ:(r + 1) * TQ]
    imp = psum[0:half, :] + psum[half:n_cmp, :]

    blk = lax.broadcasted_iota(jnp.int32, (n_blocks, TQ), 0)
    blk_f = blk.astype(F32)
    qpos = q0 + lax.broadcasted_iota(jnp.int32, (n_blocks, TQ), 1)
    forced = (blk == 0) | (blk == lax.shift_right_logical(qpos, 6))
    valid = blk * L_SEL <= qpos
    score = jnp.where(forced, 1e30, jnp.where(valid, imp, -1.0))
    picked = jnp.zeros((n_blocks, TQ), jnp.bool_)
    for _ in range(top_n):
        mx = jnp.max(score, axis=0, keepdims=True)
        first = jnp.min(jnp.where(score == mx, blk_f, float(n_blocks)), axis=0, keepdims=True)
        hit = blk_f == first
        picked = picked | hit
        score = jnp.where(hit, -2.0, score)
    seladd = jnp.where(picked, 0.0, NEG_INF)
    selfar = jnp.where(blk < 2 * (qi - 1), seladd, NEG_INF)
    sel0_ref[SUBLANES:, :] = jnp.concatenate([seladd] * Q_PER_KV, axis=1)
    selfar_ref[0:n_grp_rows, 0:SUBLANES, :] = jnp.concatenate([selfar] * Q_PER_KV, axis=1).reshape(
        n_grp_rows, SUBLANES, WIDTH)

    kw = kwin_ref[pl.ds(qi, WIN_TILES)].reshape(WIN_TILES * TQ, LANES)
    m_w, l_w, a_w = _attend(_dot(kw, qaug_ref[...]) + tzw_ref[...], _lane_concat(vwin_ref[pl.ds(qi, WIN_TILES)]))
    o_win = a_w * (1.0 / l_w)

    m_ref[...] = jnp.full((1, WIDTH), NEG_INF, F32)
    l_ref[...] = jnp.zeros((1, WIDTH), F32)
    acc_ref[...] = jnp.zeros((HEAD_DIM, WIDTH), F32)

    def far_body(gi, carry):
        qaug_ref[AUG_ROW0:AUG_ROW0 + BF16_ROWS, :] = selfar_ref[gi].astype(BF16)
        first = 1 + FAR_TILES * gi
        kg = ksel_ref[pl.ds(first, FAR_TILES)].reshape(FAR_TILES * TQ, LANES)
        m_g, l_g, a_g = _attend(_dot(kg, qaug_ref[...]), _lane_concat(vsel_ref[pl.ds(first, FAR_TILES)]))
        m_old = m_ref[...]
        m_new = jnp.maximum(m_old, m_g)
        a_old = jnp.exp(m_old - m_new)
        a_grp = jnp.exp(m_g - m_new)
        l_ref[...] = l_ref[...] * a_old + l_g * a_grp
        acc_ref[...] = acc_ref[...] * a_old + a_g * a_grp
        m_ref[...] = m_new
        return carry

    n_far = jnp.maximum(qi - 1, 0)
    lax.fori_loop(0, lax.shift_right_logical(n_far + (FAR_TILES - 1), 2), far_body, 0)

    qaug_ref[AUG_ROW0:AUG_ROW0 + BF16_ROWS, :] = jnp.zeros((BF16_ROWS, WIDTH), BF16)
    near_mask = jnp.concatenate(
        [jnp.broadcast_to(sel0_ref[pl.ds(SUBLANES + 2 * (qi - 1) + i, 1), :], (L_SEL, WIDTH)) for i in range(4)],
        axis=0)
    kn = ksel_ref[pl.ds(qi, 2)].reshape(2 * TQ, LANES)
    m_n, l_n, a_n = _attend(_dot(kn, qaug_ref[...]) + tzn_ref[...] + near_mask, _lane_concat(vsel_ref[pl.ds(qi, 2)]))
    m_old = m_ref[...]
    m_new = jnp.maximum(m_old, m_n)
    a_old = jnp.exp(m_old - m_new)
    a_grp = jnp.exp(m_n - m_new)
    o_sel = (acc_ref[...] * a_old + a_n * a_grp) * (1.0 / (l_ref[...] * a_old + l_n * a_grp))

    g = jax.nn.sigmoid(gate_ref[...])
    o = g[0:1, :] * o_cmp + g[1:2, :] * o_sel + g[2:3, :] * o_win
    for r in range(Q_PER_KV):
        o_ref[r * HEAD_DIM:(r + 1) * HEAD_DIM, :] = o[:, r * TQ:(r + 1) * TQ].astype(o_ref.dtype)


def _nsa_prompt(q2, ksel, vsel, kwin, vwin, kc, vct, gates, tables, *, name):
    tzw, tzn, band, caug = tables
    B, G, nq, D, _ = q2.shape
    n_cmp = kc.shape[2]
    n_blocks = n_cmp // (L_SEL // L_CMP)
    assert n_blocks % SUBLANES == 0 and nq % FAR_TILES == 0
    T = nq * TQ
    whole = lambda arr: pl.BlockSpec((None, None) + arr.shape[2:], lambda b, g, i: (b, g, 0, 0, 0))
    return pl.pallas_call(
        functools.partial(_nsa_prompt_body, n_blocks=n_blocks, n_cmp=n_cmp, top_n=min(TOP_N, n_blocks)),
        grid=(B, G, nq),
        in_specs=[pl.BlockSpec((None, None, None, D, WIDTH), lambda b, g, i: (b, g, i, 0, 0)),
                  whole(ksel), whole(vsel), whole(kwin), whole(vwin),
                  pl.BlockSpec((None, None, n_cmp, D), lambda b, g, i: (b, g, 0, 0)),
                  pl.BlockSpec((None, None, D, n_cmp), lambda b, g, i: (b, g, 0, 0)),
                  pl.BlockSpec((None, None, None, 3, WIDTH), lambda b, g, i: (b, g, i, 0, 0)),
                  pl.BlockSpec((None, None, WIN_TILES * TQ, WIDTH),
                               lambda b, g, i: (g, jnp.minimum(i, WIN_TILES - 1), 0, 0)),
                  pl.BlockSpec((None, None, 2 * TQ, WIDTH), lambda b, g, i: (g, jnp.minimum(i, 1), 0, 0)),
                  pl.BlockSpec((None, 16, WIDTH), lambda b, g, i: (g, 0, 0)),
                  pl.BlockSpec((None, SUBLANES, WIDTH), lambda b, g, i: (g, 0, 0))],
        out_specs=pl.BlockSpec((None, Q_PER_KV * D, TQ), lambda b, g, i: (b, g, i)),
        out_shape=jax.ShapeDtypeStruct((B, G * Q_PER_KV * D, T), BF16),
        scratch_shapes=[pltpu.VMEM((LANES, WIDTH), BF16),
                        pltpu.VMEM((1, WIDTH), F32), pltpu.VMEM((1, WIDTH), F32),
                        pltpu.VMEM((D, WIDTH), F32),
                        pltpu.VMEM((n_blocks + SUBLANES, WIDTH), F32),
                        pltpu.VMEM((nq // FAR_TILES + 1, 2 * SUBLANES, WIDTH), F32)],
        compiler_params=_cparams(("arbitrary", "arbitrary", "arbitrary")),
        name=name,
    )(q2, ksel, vsel, kwin, vwin, kc, vct, gates, tzw, tzn, band, caug)


SSD_QUAD = 4

def _ssd_body(xbc_ref, dt_ref, z_ref, cw_ref, cb_ref, dtb_ref, alog_ref, dskip_ref, ng_ref,
              exp_h_ref, exp_p_ref, y_ref, st_ref, pad_ref, x_scr, colh_scr, colp_scr, dtp_scr, y_scr):
    c = pl.program_id(1)
    L = SSM_CHUNK
    P = SSM_HEADDIM
    heads_per_group = SSM_HEADS // SSM_GROUPS

    @pl.when(c == 0)
    def _():
        pad_ref[0:SUBLANES, :] = jnp.zeros((SUBLANES, CONV_DIM), F32)
        st_ref[...] = jnp.zeros(st_ref.shape, F32)

    pad_ref[SUBLANES:SUBLANES + L, :] = xbc_ref[...]
    conv = cb_ref[...]
    for k in range(SSM_CONV):
        off = SUBLANES - (SSM_CONV - 1) + k
        conv = conv + cw_ref[k:k + 1, :] * pad_ref[off:off + L, :]
    pad_ref[0:SUBLANES, :] = pad_ref[L:L + SUBLANES, :]
    xbc = _silu(conv)
    x_scr[...] = xbc[:, 0:D_INNER]
    bmat = [xbc[:, D_INNER + g * D_STATE:D_INNER + (g + 1) * D_STATE].astype(BF16) for g in range(SSM_GROUPS)]
    cmat = [xbc[:, D_INNER + (SSM_GROUPS + g) * D_STATE:D_INNER + (SSM_GROUPS + g + 1) * D_STATE].astype(BF16)
            for g in range(SSM_GROUPS)]

    x = dt_ref[:, 0:SSM_HEADS] + dtb_ref[...]
    dt = jnp.maximum(x, 0.0) + jnp.log1p(jnp.exp(-jnp.abs(x)))
    a = dt * (-jnp.exp(alog_ref[...]))
    tri = lax.broadcasted_iota(jnp.int32, (L, L), 0) >= lax.broadcasted_iota(jnp.int32, (L, L), 1)
    tri_b = tri.astype(BF16)
    a_hi, a_mid, a_lo = _split3(a)
    a_cs = _dot(tri_b, a_hi) + _dot(tri_b, a_mid) + _dot(tri_b, a_lo)
    acs_parts = jnp.concatenate(_split3(a_cs), axis=1)
    dt_parts = jnp.concatenate(_split3(dt), axis=1)
    colh_scr[...] = _dot(acs_parts, exp_h_ref[...])
    colp_scr[...] = _dot(acs_parts, exp_p_ref[...])
    dtp_scr[...] = _dot(dt_parts, exp_p_ref[...])
    acs_t = a_cs.T
    dt_t = dt.T
    tri_q = jnp.concatenate([tri] * SSD_QUAD, axis=1)

    for qd in range(SSM_HEADS // SSD_QUAD):
        g = (qd * SSD_QUAD) // heads_per_group
        h0 = qd * SSD_QUAD
        b_g, c_g = bmat[g], cmat[g]
        cb = _dot_nt(c_g, b_g)
        col = colh_scr[:, h0 * LANES:(h0 + SSD_QUAD) * LANES]
        row = jnp.concatenate([acs_t[h0 + i:h0 + i + 1, :] for i in range(SSD_QUAD)], axis=1)
        dtrow = jnp.concatenate([dt_t[h0 + i:h0 + i + 1, :] for i in range(SSD_QUAD)], axis=1)
        decay = jnp.exp(jnp.where(tri_q, col - row, NEG_INF))
        w = (jnp.concatenate([cb] * SSD_QUAD, axis=1) * decay * dtrow).astype(BF16)
        xq = x_scr[:, h0 * P:(h0 + SSD_QUAD) * P]
        xq_b = xq.astype(BF16)
        y_diag = jnp.concatenate([_dot(w[:, i * LANES:(i + 1) * LANES], xq_b[:, i * P:(i + 1) * P])
                                  for i in range(SSD_QUAD)], axis=1)
        h_prev = st_ref[h0:h0 + SSD_QUAD].reshape(SSD_QUAD * P, D_STATE)
        colp = colp_scr[:, h0 * P:(h0 + SSD_QUAD) * P]
        y_off = _dot_nt(c_g, h_prev.astype(BF16)) * jnp.exp(colp)
        y_scr[:, h0 * P:(h0 + SSD_QUAD) * P] = y_diag + y_off
        decay_end = jnp.exp(colp[L - 1:L, :] - colp)
        xw = (decay_end * dtp_scr[:, h0 * P:(h0 + SSD_QUAD) * P]) * xq
        st = _dot_tn(xw.astype(BF16), b_g)
        chunk_decay = jnp.exp(col[L - 1:L, :])
        cd = jnp.concatenate([jnp.broadcast_to(chunk_decay[:, i * LANES:(i + 1) * LANES], (P, D_STATE))
                              for i in range(SSD_QUAD)], axis=0)
        st_ref[h0:h0 + SSD_QUAD] = (h_prev * cd + st).reshape(SSD_QUAD, P, D_STATE)

    group_w = D_INNER // SSM_GROUPS
    y = y_scr[...] + dskip_ref[...] * x_scr[...]
    y = y * _silu(z_ref[...])
    for g in range(SSM_GROUPS):
        yg = y[:, g * group_w:(g + 1) * group_w]
        inv = lax.rsqrt(jnp.mean(yg * yg, axis=-1, keepdims=True) + EPS)
        y_ref[:, g * group_w:(g + 1) * group_w] = (
            yg * inv * ng_ref[:, g * group_w:(g + 1) * group_w]).astype(y_ref.dtype)


def _ssd_prompt(xbc, dtg, z, conv_w, conv_b, dt_bias, a_log, d_skip, norm_g, *, batch, name):
    M = xbc.shape[0]
    L = SSM_CHUNK
    nc = M // batch // L
    heads = jnp.arange(SSM_HEADS)

    def expand(lanes):
        e = (heads[:, None, None] == heads[None, :, None])
        e = jnp.broadcast_to(e, (SSM_HEADS, SSM_HEADS, lanes)).reshape(SSM_HEADS, SSM_HEADS * lanes)
        return jnp.concatenate([e] * 3, axis=0).astype(BF16)

    row = lambda v: v.reshape(1, -1)
    const2 = lambda shape: pl.BlockSpec(shape, lambda b, c: (0, 0))
    return pl.pallas_call(
        _ssd_body,
        grid=(batch, nc),
        in_specs=[pl.BlockSpec((L, CONV_DIM), lambda b, c: (b * nc + c, 0)),
                  pl.BlockSpec((L, LANES), lambda b, c: (b * nc + c, 1)),
                  pl.BlockSpec((L, D_INNER), lambda b, c: (b * nc + c, 0)),
                  const2((SSM_CONV, CONV_DIM)), const2((1, CONV_DIM)),
                  const2((1, SSM_HEADS)), const2((1, SSM_HEADS)),
                  const2((1, D_INNER)), const2((1, D_INNER)),
                  const2((3 * SSM_HEADS, SSM_HEADS * LANES)), const2((3 * SSM_HEADS, D_INNER))],
        out_specs=[pl.BlockSpec((L, D_INNER), lambda b, c: (b * nc + c, 0)),
                   pl.BlockSpec((None, SSM_HEADS, SSM_HEADDIM, D_STATE), lambda b, c: (b, 0, 0, 0))],
        out_shape=[jax.ShapeDtypeStruct((M, D_INNER), BF16),
                   jax.ShapeDtypeStruct((batch, SSM_HEADS, SSM_HEADDIM, D_STATE), F32)],
        scratch_shapes=[pltpu.VMEM((L + SUBLANES, CONV_DIM), F32),
                        pltpu.VMEM((L, D_INNER), F32),
                        pltpu.VMEM((L, SSM_HEADS * LANES), F32),
                        pltpu.VMEM((L, D_INNER), F32),
                        pltpu.VMEM((L, D_INNER), F32),
                        pltpu.VMEM((L, D_INNER), F32)],
        compiler_params=_cparams(("arbitrary", "arbitrary")),
        name=name,
    )(xbc, dtg, z, conv_w, row(conv_b), row(dt_bias), row(a_log),
      row(jnp.repeat(d_skip, SSM_HEADDIM)), row(norm_g), expand(LANES), expand(SSM_HEADDIM))


def _t5_bucket(dist):
    n = jnp.maximum(dist, 0)
    max_exact = N_BUCKETS // 2
    nf = jnp.maximum(n, max_exact).astype(F32)
    large = max_exact + (jnp.log(nf / max_exact) / math.log(MAX_DISTANCE / max_exact)
                         * (N_BUCKETS - max_exact)).astype(jnp.int32)
    return jnp.where(n < max_exact, n, jnp.minimum(large, N_BUCKETS - 1))


def _bias_tables(rel_bias):
    by_dist = rel_bias[_t5_bucket(jnp.arange((WIN_TILES + 1) * TQ))]
    key = jnp.arange(TQ)[:, None]
    qry = jnp.arange(TQ)[None, :]
    delta = (jnp.arange(WIN_TILES - 1, -1, -1) * TQ)[:, None, None]
    dist = delta + qry[None] - key[None]
    ok = (dist >= 0) & (dist <= WINDOW)
    tz = jnp.where(ok[..., None], by_dist[jnp.clip(dist, 0)], NEG_INF)
    tz = tz.reshape(WIN_TILES, TQ, TQ, N_KV, Q_PER_KV).transpose(3, 0, 1, 4, 2)
    tz = tz.reshape(N_KV, WIN_TILES, TQ, WIDTH)
    tile = jnp.arange(WIN_TILES)[None, :, None, None]

    def variants(tiles, n_var):
        n_tiles = tiles.shape[1]
        v = jnp.arange(n_var)[:, None, None, None]
        masked = tile[:, :n_tiles] < (n_tiles - 1 - v)
        out = jnp.where(masked[None], NEG_INF, tiles[:, None])
        return out.reshape(N_KV, n_var, n_tiles * TQ, WIDTH)

    tzw = variants(tz, WIN_TILES)
    tzn = variants(tz[:, WIN_TILES - 2:], 2)
    far = rel_bias[N_BUCKETS - 1]
    rel = jnp.arange(8)[:, None] - 4
    cdist = qry - L_CMP * rel - (L_CMP - 1)
    band = jnp.where((cdist >= 0)[..., None], by_dist[jnp.clip(cdist, 0)], NEG_INF)
    far_rows = jnp.broadcast_to(far[None, None, :], (1, TQ, N_HEADS))
    band = jnp.concatenate([band, far_rows, jnp.full((1, TQ, N_HEADS), NEG_INF, F32),
                            jnp.zeros((6, TQ, N_HEADS), F32)], axis=0)
    band = band.reshape(16, TQ, N_KV, Q_PER_KV).transpose(2, 0, 3, 1).reshape(N_KV, 16, WIDTH)
    parts = jnp.stack([p.astype(F32) for p in _split3(far)] + [jnp.zeros_like(far)] * (SUBLANES - 3))
    caug = jnp.broadcast_to(parts.reshape(SUBLANES, N_KV, Q_PER_KV, 1), (SUBLANES, N_KV, Q_PER_KV, TQ))
    caug = caug.transpose(1, 0, 2, 3).reshape(N_KV, SUBLANES, WIDTH)
    return tzw, tzn, band, caug


def _layer_weights(l, w_ada, w_in, cmp_pe, cmp_w1, cmp_w2, w_a, w_b, w_o, w_gu, w_down):
    wi = w_in[l]
    small = jnp.zeros((D_MODEL, 2 * LANES), F32)
    small = small.at[:, 0:NSA_GATE_COLS].set(wi[:, OFF_G:OFF_Z])
    small = small.at[:, LANES:LANES + SSM_HEADS].set(wi[:, OFF_DT:OFF_MG])
    c = lambda w: w.astype(BF16)
    w1r = cmp_w1[l].reshape(2, L_CMP, HEAD_DIM, CMP_HID)
    w1_pair = jnp.zeros((2, L_CMP, 2 * HEAD_DIM, 2 * CMP_HID), F32)
    w1_pair = w1_pair.at[:, :, 0:HEAD_DIM, 0:CMP_HID].set(w1r).at[:, :, HEAD_DIM:, CMP_HID:].set(w1r)
    pe_pair = jnp.tile(cmp_pe[l], (1, 1, 2))
    return dict(ada=c(w_ada[l]), q=c(wi[:, 0:OFF_KV]), kv=c(wi[:, OFF_KV:OFF_G]), small=c(small),
                w1_pair=c(w1_pair), pe_pair=pe_pair,
                z=c(wi[:, OFF_Z:OFF_XBC]), xbc=c(wi[:, OFF_XBC:OFF_DT]), mg=c(wi[:, OFF_MG:P_IN]),
                cmp_w1=c(cmp_w1[l]), cmp_w2=c(cmp_w2[l]), a=c(w_a[l]), b=c(w_b[l]), o=c(w_o[l]),
                gu=c(w_gu[l]), down=c(w_down[l]))


def _prompt_layer(x, mod, l, Wl, P, tables, batch, T):
    sh1, sc1, g1, sh2, sc2, g2 = [m.reshape(batch, 1, D_MODEL) for m in jnp.split(mod, 6, axis=-1)]
    tm = 1024 if T % 1024 == 0 else 512
    h = _norm_mod(x, P['norm1_g'][l], sc1, sh1, tm=512, rows_per_batch=T, name=f"norm1_p{l}")
    q = _mm(h, Wl['q'], tm=tm, tn=512, name=f"proj_q_p{l}", out_dtype=BF16,
            epilogue=lambda acc: acc * SCALE)
    kv = _mm(h, Wl['kv'], tm=tm, tn=512, name=f"proj_kv_p{l}")
    small = _mm(h, Wl['small'], tm=tm, tn=2 * LANES, name=f"proj_small_p{l}")
    z = _mm(h, Wl['z'], tm=tm, tn=512, name=f"proj_z_p{l}")
    xbc = _mm(h, Wl['xbc'], tm=tm, tn=512, name=f"proj_xbc_p{l}")
    gates = _mm(h, Wl['mg'], tm=tm, tn=512, name=f"proj_mg_p{l}", epilogue=jax.nn.sigmoid)

    o_nsa = _prompt_attention(q, kv, small, l, Wl, P, tables, batch, T)
    kv6 = kv.reshape(batch, T, 6, N_KV, HEAD_DIM)

    y, ssm_state = _ssd_prompt(xbc, small, z, P['ssm_conv_w'][l], P['ssm_conv_b'][l], P['dt_bias'][l],
                               P['a_log'][l], P['d_skip'][l], P['ssm_norm_g'][l], batch=batch, name=f"ssd_p{l}")

    u = _merge(o_nsa, y, Wl['a'], Wl['b'], gates, tm=tm, tn=512, name=f"merge_p{l}")
    x = _mm(u, Wl['o'], tm=tm, tn=512, name=f"proj_o_p{l}", rows_per_batch=T,
            epilogue=lambda acc, xr, gr: xr + gr * acc, extras=[('full', x), ('batch', g1)])

    h2 = _norm_mod(x, P['norm2_g'][l], sc2, sh2, tm=512, rows_per_batch=T, name=f"norm2_p{l}")
    act = _ffn_up(h2, Wl['gu'], P['ffn_conv_w'][l], P['ffn_conv_b'][l], tm=512, tn=D_FF // 2,
                  rows_per_batch=T, name=f"ffn_up_p{l}")
    h2_last = h2.reshape(batch, T, D_MODEL)[:, T - SUBLANES:].reshape(batch * SUBLANES, D_MODEL)
    gate_last = _mm(h2_last, Wl['gu'][:, 0:D_FF], tm=batch * SUBLANES, tn=D_FF // 2, name=f"ffn_state_p{l}")
    ffn_state = gate_last.reshape(batch, SUBLANES, D_FF)[:, SUBLANES - (FFN_CONV - 1):]
    x = _mm(act, Wl['down'], tm=tm, tn=512, name=f"ffn_down_p{l}", rows_per_batch=T,
            epilogue=lambda acc, xr, gr: xr + gr * acc, extras=[('full', x), ('batch', g2)])

    kv_rows = kv6[:, :, :4]
    win_state = kv6[:, T - min(WINDOW, T):, 4:]
    conv_state = xbc.reshape(batch, T, CONV_DIM)[:, T - (SSM_CONV - 1):]
    return x, (kv_rows, win_state, conv_state, ssm_state, ffn_state)


def _prompt_attention(q, kv, small, l, Wl, P, tables, batch, T):
    nq = T // TQ
    kv6 = kv.reshape(batch, T, 6, N_KV, HEAD_DIM)
    n_cmp = T // L_CMP

    def cmp_in(slot):
        xb = kv6[:, :, slot].reshape(batch, n_cmp, L_CMP, N_KV, HEAD_DIM).transpose(0, 3, 1, 2, 4)
        return xb.reshape(batch * N_KV * n_cmp, L_CMP * HEAD_DIM)

    ctm = min(512, batch * N_KV * n_cmp)
    kc = _compress(cmp_in(0), P['cmp_pe'][l, 0], Wl['cmp_w1'][0], Wl['cmp_w2'][0], tm=ctm, name=f"cmp_k_p{l}")
    vc = _compress(cmp_in(1), P['cmp_pe'][l, 1], Wl['cmp_w1'][1], Wl['cmp_w2'][1], tm=ctm, name=f"cmp_v_p{l}")

    def even_odd(c):
        c = c.reshape(batch, N_KV, n_cmp // 2, 2, HEAD_DIM).transpose(0, 1, 3, 2, 4)
        return c.reshape(batch, N_KV, n_cmp, HEAD_DIM)

    kc = even_odd(kc.reshape(batch, N_KV, n_cmp, HEAD_DIM)).astype(BF16)
    vct = even_odd(vc.reshape(batch, N_KV, n_cmp, HEAD_DIM)).astype(BF16).transpose(0, 1, 3, 2)

    def k_tiles(slot, front, back, aug):
        k = kv6[:, :, slot].astype(BF16).reshape(batch, nq, TQ, N_KV, HEAD_DIM).transpose(0, 3, 1, 2, 4)
        k = jnp.pad(k, ((0, 0), (0, 0), (front, back), (0, 0), (0, LANES - HEAD_DIM)))
        if aug:
            kt = jnp.arange(front + nq + back) - front
            blk_col = AUG_ROW0 + 2 * (kt % FAR_TILES)[:, None] + (jnp.arange(TQ) // L_SEL)[None, :]
            lane = jnp.arange(LANES)[None, None, :]
            ones = (lane == blk_col[..., None]) | ((lane >= AUG_ROW0 + SUBLANES) & (lane < AUG_ROW0 + SUBLANES + 3))
            k = jnp.where(ones[None, None], jnp.ones((), BF16), k)
        return k

    def vt_tiles(slot, front, back):
        v = kv6[:, :, slot].astype(BF16).reshape(batch, nq, TQ, N_KV, HEAD_DIM).transpose(0, 3, 1, 4, 2)
        return jnp.pad(v, ((0, 0), (0, 0), (front, back), (0, 0), (0, 0)))

    q2 = q.reshape(batch, nq, TQ, N_KV, Q_PER_KV, HEAD_DIM).transpose(0, 3, 1, 5, 4, 2)
    q2 = q2.reshape(batch, N_KV, nq, HEAD_DIM, WIDTH)
    ng = small[:, 0:NSA_GATE_COLS].reshape(batch, nq, TQ, 3, N_KV, Q_PER_KV).transpose(0, 4, 1, 3, 5, 2)
    ng = ng.reshape(batch, N_KV, nq, 3, WIDTH)
    o_t = _nsa_prompt(q2, k_tiles(2, 1, 1, True), vt_tiles(3, 1, 1),
                      k_tiles(4, WIN_TILES - 1, 0, False), vt_tiles(5, WIN_TILES - 1, 0),
                      kc, vct, ng, tables, name=f"nsa_p{l}")
    return o_t.transpose(0, 2, 1).reshape(batch * T, Q_COLS)


def _pool_compress_body(x00_ref, x01_ref, x10_ref, x11_ref, pe_ref, w1_ref, w2_ref, o_ref, *, rows):
    x_refs = ((x00_ref, x01_ref), (x10_ref, x11_ref))
    for s in range(2):
        outs = []
        for pair in range(N_KV // 2):
            xs = [(x_refs[s][pair][pl.ds(l, rows, stride=L_CMP), :] + pe_ref[s, l:l + 1, :]).astype(BF16)
                  for l in range(L_CMP)]
            w1 = w1_ref[s].reshape(L_CMP * 2 * HEAD_DIM, 2 * CMP_HID)
            hid = _silu(_dot(jnp.concatenate(xs, axis=1), w1)).astype(BF16)
            for gg in range(2):
                outs.append(_dot(hid[:, gg * CMP_HID:(gg + 1) * CMP_HID], w2_ref[s]))
        o_ref[:, s * N_KV * HEAD_DIM:(s + 1) * N_KV * HEAD_DIM] = jnp.concatenate(outs, axis=1)


def _pool_compress(cache2, l, pe2, w1pair, w2, *, page, pages_per_step, name):
    n_rows = cache2.shape[1]
    rows_in = pages_per_step * page
    rows_out = rows_in // L_CMP
    half = 2 * N_KV * HEAD_DIM
    return pl.pallas_call(
        functools.partial(_pool_compress_body, rows=rows_out),
        grid=(n_rows // rows_in,),
        in_specs=[pl.BlockSpec((None, rows_in, 2 * HEAD_DIM), lambda i, j=j: (l, i, j)) for j in range(N_KV)]
                 + [pl.BlockSpec(pe2.shape, lambda i: (0, 0, 0)),
                    pl.BlockSpec(w1pair.shape, lambda i: (0, 0, 0, 0)),
                    pl.BlockSpec(w2.shape, lambda i: (0, 0, 0))],
        out_specs=pl.BlockSpec((rows_out, half), lambda i: (i, 0)),
        out_shape=jax.ShapeDtypeStruct((n_rows // L_CMP, half), F32),
        compiler_params=_cparams(("parallel",)),
        name=name,
    )(cache2, cache2, cache2, cache2, pe2, w1pair, w2)


def _softmax_lanes(s):
    m = jnp.max(s, axis=1, keepdims=True)
    e = jnp.exp(s - m)
    return m, e, jnp.sum(e, axis=1, keepdims=True)


def _sum3(lhs_bf16, x):
    hi, mid, lo = _split3(x)
    return _dot(lhs_bf16, hi) + _dot(lhs_bf16, mid) + _dot(lhs_bf16, lo)


def _nsa_decode_body(pt_ref, q_ref, gate_ref, kvnew_ref, kcn_ref, *refs, n_pages, n_blocks, top_n, win_len):
    sel_pages = refs[0:n_pages]
    kc_pages = refs[n_pages:2 * n_pages]
    (win_ref, bsel_ref, bwin_ref, bcmp_ref, e_ref, pair_ref, rsum_ref, rexp_ref,
     o_ref, kc_scr, s_scr) = refs[2 * n_pages:]
    gd = N_KV * HEAD_DIM
    blocks_per_page = kc_pages[0].shape[0]
    n_past_blocks = blocks_per_page * n_pages

    @pl.when(pl.program_id(0) == 0)
    def _():
        kc_scr[...] = jnp.zeros(kc_scr.shape, F32)

    q = q_ref[...]
    row0 = lax.broadcasted_iota(jnp.int32, (LANES, gd), 0) == 0

    def new_page(lane0):
        return jnp.where(row0, kvnew_ref[:, lane0:lane0 + gd], 0.0).astype(BF16)

    for p in range(n_pages):
        kc_scr[p * blocks_per_page:(p + 1) * blocks_per_page, :] = kc_pages[p][...]
    kc_scr[n_past_blocks:n_past_blocks + SUBLANES, :] = kcn_ref[...]
    kcv = kc_scr[...]
    m_c, e_c, den_c = _softmax_lanes(_dot_nt(q, kcv[:, 0:gd].astype(BF16)) + bcmp_ref[...])
    p_c = e_c * ((m_c > 0.5 * NEG_INF).astype(F32) / den_c)
    o_cmp = _dot(p_c.astype(BF16), kcv[:, gd:2 * gd].astype(BF16))
    imp = _sum3(rsum_ref[...], _sum3_rhs(p_c, pair_ref[...]))

    lane_i = lax.broadcasted_iota(jnp.int32, (SUBLANES, LANES), 1)
    forced = (lane_i == 0) | (lane_i == n_blocks - 1)
    score = jnp.where(lane_i < n_blocks, jnp.where(forced, 1e30, imp), -1.0)
    score_t = score.T
    ii = lax.broadcasted_iota(jnp.int32, (LANES, LANES), 0)
    jj = lax.broadcasted_iota(jnp.int32, (LANES, LANES), 1)
    sel_rows = []
    for g in range(N_KV):
        col = jnp.broadcast_to(score_t[:, g:g + 1], (LANES, LANES))
        rowv = jnp.broadcast_to(score[g:g + 1, :], (LANES, LANES))
        ahead = (col > rowv) | ((col == rowv) & (ii < jj))
        rank = jnp.sum(ahead.astype(F32), axis=0, keepdims=True)
        sel_rows.append((rank < float(top_n)).astype(F32))
    sel = jnp.concatenate(sel_rows + [jnp.zeros((SUBLANES - N_KV, LANES), F32)], axis=0)
    sel = jnp.where(lane_i < n_blocks, sel, 0.0)
    sel_h = _dot(rexp_ref[...], sel.astype(BF16))
    sel_keys = _dot(sel_h.astype(BF16), e_ref[...])

    for p in range(n_pages):
        s_scr[:, p * LANES:(p + 1) * LANES] = _dot_nt(q, sel_pages[p][:, 0:gd].astype(BF16))
    s_scr[:, n_pages * LANES:(n_pages + 1) * LANES] = _dot_nt(q, new_page(2 * gd))
    s = jnp.where(sel_keys > 0.5, s_scr[...] + bsel_ref[...], NEG_INF)
    _, e_s, den_s = _softmax_lanes(s)
    e_b = e_s.astype(BF16)
    o_sel = _dot(e_b[:, n_pages * LANES:(n_pages + 1) * LANES], new_page(3 * gd))
    for p in range(n_pages):
        o_sel = o_sel + _dot(e_b[:, p * LANES:(p + 1) * LANES], sel_pages[p][:, gd:2 * gd].astype(BF16))
    o_sel = o_sel * (1.0 / den_s)

    s_w = jnp.concatenate([_dot_nt(q, win_ref[:, 0:gd].astype(BF16)), _dot_nt(q, new_page(4 * gd))], axis=1)
    _, e_w, den_w = _softmax_lanes(s_w + bwin_ref[...])
    e_wb = e_w.astype(BF16)
    o_win = (_dot(e_wb[:, 0:win_len], win_ref[:, gd:2 * gd].astype(BF16))
             + _dot(e_wb[:, win_len:win_len + LANES], new_page(5 * gd))) * (1.0 / den_w)

    g = jax.nn.sigmoid(gate_ref[...])
    o = g[:, 0:1] * o_cmp + g[:, 1:2] * o_sel + g[:, 2:3] * o_win
    own = (lax.broadcasted_iota(jnp.int32, (N_HEADS, gd), 1) // HEAD_DIM
           == lax.broadcasted_iota(jnp.int32, (N_HEADS, gd), 0) // Q_PER_KV)
    o = jnp.where(own, o, 0.0)
    acc = o[:, 0:HEAD_DIM]
    for gi in range(1, N_KV):
        acc = acc + o[:, gi * HEAD_DIM:(gi + 1) * HEAD_DIM]
    o_ref[...] = acc


def _sum3_rhs(x, rhs_bf16):
    hi, mid, lo = _split3(x)
    return _dot(hi, rhs_bf16) + _dot(mid, rhs_bf16) + _dot(lo, rhs_bf16)


def _nsa_decode(page_table, q_bd, gates_t, kvnew, kcn, cache4, kcpool, cache_win, l, tabs, *, name):
    B, n_pages = page_table.shape
    page = cache4.shape[2]
    gd = N_KV * HEAD_DIM
    win_len = cache_win.shape[2]
    bsel, bwin, bcmp, e_mat, pair, rsum, rexp, n_blocks = tabs
    blocks_per_page = kcpool.shape[1]
    const = lambda a: pl.BlockSpec(a.shape, lambda b, pt: (0,) * a.ndim)
    sel_specs = [pl.BlockSpec((None, None, page, 2 * gd), lambda b, pt, p=p: (l, pt[b, p], 0, 1))
                 for p in range(n_pages)]
    kc_specs = [pl.BlockSpec((None, blocks_per_page, 2 * gd), lambda b, pt, p=p: (pt[b, p], 0, 0))
                for p in range(n_pages)]
    grid_spec = pltpu.PrefetchScalarGridSpec(
        num_scalar_prefetch=1,
        grid=(B,),
        in_specs=[pl.BlockSpec((None, N_HEADS, gd), lambda b, pt: (b, 0, 0)),
                  pl.BlockSpec((None, N_HEADS, 3), lambda b, pt: (b, 0, 0)),
                  pl.BlockSpec((None, 1, kvnew.shape[2]), lambda b, pt: (b, 0, 0)),
                  pl.BlockSpec((None, SUBLANES, 2 * gd), lambda b, pt: (b, 0, 0))]
                 + sel_specs + kc_specs
                 + [pl.BlockSpec((None, None, win_len, 2 * gd), lambda b, pt: (l, b, 0, 0)),
                    const(bsel), const(bwin), const(bcmp), const(e_mat), const(pair), const(rsum), const(rexp)],
        out_specs=pl.BlockSpec((None, N_HEADS, HEAD_DIM), lambda b, pt: (b, 0, 0)),
        scratch_shapes=[pltpu.VMEM((LANES, 2 * gd), F32),
                        pltpu.VMEM((N_HEADS, (n_pages + 1) * LANES), F32)])
    return pl.pallas_call(
        functools.partial(_nsa_decode_body, n_pages=n_pages, n_blocks=n_blocks,
                          top_n=min(TOP_N, n_blocks), win_len=win_len),
        grid_spec=grid_spec,
        out_shape=jax.ShapeDtypeStruct((B, N_HEADS, HEAD_DIM), F32),
        compiler_params=_cparams(("arbitrary",)),
        name=name,
    )(page_table, q_bd, gates_t, kvnew, kcn, *([cache4] * n_pages), *([kcpool] * n_pages), cache_win,
      bsel, bwin, bcmp, e_mat, pair, rsum, rexp)


def _decode_tables(rel_bias, past_len, n_pages, win_len):
    q_pos = past_len
    n_blocks = (past_len + 1 + L_SEL - 1) // L_SEL
    n_cmp = n_blocks * (L_SEL // L_CMP)
    by_dist = rel_bias[_t5_bucket(jnp.arange(q_pos + 1))].T
    keys = (n_pages + 1) * LANES
    kpos = jnp.arange(keys)
    bsel = jnp.where((kpos <= q_pos)[None], by_dist[:, jnp.clip(q_pos - kpos, 0)], NEG_INF)
    wl = jnp.arange(win_len + LANES)
    wdist = win_len - wl
    bwin = jnp.where((wdist >= 0)[None], by_dist[:, jnp.clip(wdist, 0, q_pos)], NEG_INF)
    c = jnp.arange(LANES)
    cdist = q_pos - (c * L_CMP + L_CMP - 1)
    bcmp = jnp.where(((cdist >= 0) & (c < n_cmp))[None], by_dist[:, jnp.clip(cdist, 0)], NEG_INF)
    e_mat = (jnp.arange(LANES)[:, None] == (kpos // L_SEL)[None, :]).astype(BF16)
    pair = ((c[:, None] // (L_SEL // L_CMP) == c[None, :]) & (c[:, None] < n_cmp)).astype(BF16)
    heads = jnp.arange(N_HEADS)
    rsum = (jnp.arange(SUBLANES)[:, None] == (heads // Q_PER_KV)[None, :]).astype(BF16)
    return bsel, bwin, bcmp, e_mat, pair, rsum, rsum.T, n_blocks


def _conv_step_body(x_ref, p0_ref, p1_ref, p2_ref, cw_ref, cb_ref, dt_ref, dtb_ref, alog_ref,
                    xact_ref, dto_ref, dec_ref):
    y = cb_ref[...]
    for k, ref in enumerate((p0_ref, p1_ref, p2_ref, x_ref)):
        y = y + cw_ref[k:k + 1, :] * ref[...]
    xact_ref[...] = _silu(y)
    x = dt_ref[:, 0:SSM_HEADS] + dtb_ref[...]
    dt = jnp.maximum(x, 0.0) + jnp.log1p(jnp.exp(-jnp.abs(x)))
    dto_ref[...] = dt
    dec_ref[...] = jnp.exp(dt * (-jnp.exp(alog_ref[...])))


def _conv_step(xbc, prev, conv_w, conv_b, dtg, dt_bias, a_log, *, name):
    B = xbc.shape[0]
    full = lambda a: pl.BlockSpec(a.shape, lambda i: (0,) * a.ndim)
    ops = [xbc, prev[:, 0], prev[:, 1], prev[:, 2], conv_w, conv_b.reshape(1, -1), dtg,
           dt_bias.reshape(1, -1), a_log.reshape(1, -1)]
    in_specs = [full(a) for a in ops]
    in_specs[6] = pl.BlockSpec((B, LANES), lambda i: (0, 1))
    return pl.pallas_call(
        _conv_step_body,
        grid=(1,),
        in_specs=in_specs,
        out_specs=[pl.BlockSpec((B, CONV_DIM), lambda i: (0, 0)), pl.BlockSpec((B, SSM_HEADS), lambda i: (0, 0)),
                   pl.BlockSpec((B, SSM_HEADS), lambda i: (0, 0))],
        out_shape=[jax.ShapeDtypeStruct((B, CONV_DIM), F32), jax.ShapeDtypeStruct((B, SSM_HEADS), F32),
                   jax.ShapeDtypeStruct((B, SSM_HEADS), F32)],
        compiler_params=_cparams(("arbitrary",)),
        name=name,
    )(*ops)


def _ssd_step_body(dts_ref, decs_ref, h0_ref, xs_ref, xst_ref, bm_ref, cm_ref, z_ref, dtc_ref, dskip_ref, ng_ref,
                   y_ref, st_ref, yoff_scr):
    b = pl.program_id(0)
    heads_per_group = SSM_HEADS // SSM_GROUPS
    bm = bm_ref[...]
    cm = cm_ref[...]
    rb = lambda v: v.astype(BF16).astype(F32)
    cb = jnp.sum(rb(cm) * rb(bm), axis=-1, keepdims=True)
    cm8 = jnp.concatenate([cm, jnp.zeros((SUBLANES - SSM_GROUPS, D_STATE), F32)], axis=0).astype(BF16)
    for h in range(SSM_HEADS):
        g = h // heads_per_group
        h_prev = h0_ref[h]
        dec = decs_ref[b, h]
        y_off = _dot_nt(cm8, h_prev.astype(BF16))
        yoff_scr[h:h + 1, :] = y_off[g:g + 1, :] * dec
        xdt = xst_ref[:, h:h + 1] * dts_ref[b, h]
        st_ref[h] = h_prev * dec + xdt * bm[g:g + 1, :]
    xs = xs_ref[...]
    cbh = jnp.concatenate([jnp.broadcast_to(cb[g:g + 1, :], (heads_per_group, 1)) for g in range(SSM_GROUPS)], axis=0)
    y = (cbh * dtc_ref[...]) * xs + yoff_scr[...] + dskip_ref[...] * xs
    y = y * _silu(z_ref[...])
    for g in range(SSM_GROUPS):
        yg = y[g * heads_per_group:(g + 1) * heads_per_group, :]
        ms = jnp.sum(jnp.sum(yg * yg, axis=1, keepdims=True), axis=0, keepdims=True) * (1.0 / (heads_per_group * SSM_HEADDIM))
        y_ref[g * heads_per_group:(g + 1) * heads_per_group, :] = (
            yg * lax.rsqrt(ms + EPS) * ng_ref[g * heads_per_group:(g + 1) * heads_per_group, :]).astype(y_ref.dtype)


def _ssd_step(state, l, xact, dt, dec, z, d_skip, norm_g, *, name):
    B = xact.shape[0]
    H, Pd, N = SSM_HEADS, SSM_HEADDIM, D_STATE
    xs = xact[:, 0:D_INNER].reshape(B, H, Pd)
    bm = xact[:, D_INNER:D_INNER + SSM_GROUPS * N].reshape(B, SSM_GROUPS, N)
    cm = xact[:, D_INNER + SSM_GROUPS * N:].reshape(B, SSM_GROUPS, N)
    per_b = lambda shape: pl.BlockSpec((None,) + shape, lambda b, *_: (b,) + (0,) * len(shape))
    const = lambda shape: pl.BlockSpec(shape, lambda b, *_: (0,) * len(shape))
    grid_spec = pltpu.PrefetchScalarGridSpec(
        num_scalar_prefetch=2,
        grid=(B,),
        in_specs=[pl.BlockSpec((None, None, H, Pd, N), lambda b, *_: (l, b, 0, 0, 0)),
                  per_b((H, Pd)), per_b((Pd, H)), per_b((SSM_GROUPS, N)), per_b((SSM_GROUPS, N)),
                  per_b((H, Pd)), per_b((H, 1)), const((H, 1)), const((H, Pd))],
        out_specs=[per_b((H, Pd)), per_b((H, Pd, N))],
        scratch_shapes=[pltpu.VMEM((H, Pd), F32)])
    return pl.pallas_call(
        _ssd_step_body,
        grid_spec=grid_spec,
        out_shape=[jax.ShapeDtypeStruct((B, H, Pd), BF16), jax.ShapeDtypeStruct((B, H, Pd, N), F32)],
        compiler_params=_cparams(("arbitrary",)),
        name=name,
    )(dt, dec, state, xs, xs.transpose(0, 2, 1), bm, cm, z.reshape(B, H, Pd), dt.reshape(B, H, 1),
      d_skip.reshape(H, 1), norm_g.reshape(H, Pd))


def _ffn_up_step_body(h_ref, wg_ref, wu_ref, p0_ref, p1_ref, cw_ref, cb_ref, act_ref, gate_ref):
    h = h_ref[...]
    gate = _dot(h, wg_ref[...])
    up = _dot(h, wu_ref[...])
    y = cb_ref[...] + cw_ref[0:1, :] * p0_ref[...] + cw_ref[1:2, :] * p1_ref[...] + cw_ref[2:3, :] * gate
    gate_ref[...] = gate
    act_ref[...] = (_silu(y) * up).astype(act_ref.dtype)


def _ffn_up_step(h, w_gu, prev, conv_w, conv_b, *, tn, name):
    B, K = h.shape
    nj = D_FF // tn
    col = lambda r: pl.BlockSpec((r, tn), lambda j: (0, j))
    return pl.pallas_call(
        _ffn_up_step_body,
        grid=(nj,),
        in_specs=[pl.BlockSpec((B, K), lambda j: (0, 0)), col(K),
                  pl.BlockSpec((K, tn), lambda j, nj=nj: (0, j + nj)),
                  col(B), col(B), col(FFN_CONV), col(1)],
        out_specs=[col(B), col(B)],
        out_shape=[jax.ShapeDtypeStruct((B, D_FF), BF16), jax.ShapeDtypeStruct((B, D_FF), F32)],
        compiler_params=_cparams(("parallel",)),
        name=name,
    )(h, w_gu, w_gu, prev[:, 0], prev[:, 1], conv_w, conv_b.reshape(1, D_FF))


def _pages_per_step(n_pool, limit=32):
    return max(p for p in range(2, limit + 1, 2) if n_pool % p == 0)


def _sample_layer(x, mod, l, Wl, P, C):
    B = x.shape[0]
    gd = N_KV * HEAD_DIM
    sh1, sc1, g1, sh2, sc2, g2 = jnp.split(mod, 6, axis=-1)
    h = _norm_mod(x, P['norm1_g'][l], sc1, sh1, tm=B, rows_per_batch=1, name=f"norm1_s{l}")
    q = _mm(h, Wl['q'], tm=B, tn=512, name=f"proj_q_s{l}", out_dtype=BF16, epilogue=lambda acc: acc * SCALE)
    kv = _mm(h, Wl['kv'], tm=B, tn=512, name=f"proj_kv_s{l}")
    small = _mm(h, Wl['small'], tm=B, tn=2 * LANES, name=f"proj_small_s{l}")
    z = _mm(h, Wl['z'], tm=B, tn=512, name=f"proj_z_s{l}")
    xbc = _mm(h, Wl['xbc'], tm=B, tn=512, name=f"proj_xbc_s{l}")
    gates = _mm(h, Wl['mg'], tm=B, tn=512, name=f"proj_mg_s{l}", epilogue=jax.nn.sigmoid)

    cache_kv, page_table = C['cache_kv'], C['page_table']
    n_pool, page = cache_kv.shape[1], cache_kv.shape[2]
    kcpool = _pool_compress(cache_kv.reshape(DEPTH, n_pool * page, 4 * gd), l, Wl['pe_pair'], Wl['w1_pair'],
                            Wl['cmp_w2'], page=page, pages_per_step=_pages_per_step(n_pool), name=f"cmp_pool_s{l}")
    kcpool = kcpool.reshape(n_pool, page // L_CMP, 2 * gd)

    def new_blocks(slot):
        first = jnp.pad(kv[:, slot * gd:(slot + 1) * gd].reshape(B, N_KV, 1, HEAD_DIM),
                        ((0, 0), (0, 0), (0, 1), (0, (L_CMP - 1) * HEAD_DIM)))
        rows = B * N_KV * 2
        out = _compress(first.reshape(rows, L_CMP * HEAD_DIM), P['cmp_pe'][l, slot], Wl['cmp_w1'][slot],
                        Wl['cmp_w2'][slot], tm=512 if rows % 512 == 0 else rows, name=f"cmp{slot}_new_s{l}")
        return out.reshape(B, N_KV, 2, HEAD_DIM).transpose(0, 2, 1, 3).reshape(B, 2, gd)

    kcn = jnp.pad(jnp.concatenate([new_blocks(0), new_blocks(1)], axis=-1), ((0, 0), (0, SUBLANES - 2), (0, 0)))
    qh = q.reshape(B, N_KV, Q_PER_KV, 1, HEAD_DIM)
    q_bd = (qh * jnp.eye(N_KV, dtype=BF16)[None, :, None, :, None]).reshape(B, N_HEADS, gd)
    gates_t = small[:, 0:NSA_GATE_COLS].reshape(B, 3, N_HEADS).transpose(0, 2, 1)
    win_buf = C['cache_win_kv']
    win_len = win_buf.shape[2]
    o_nsa = _nsa_decode(page_table, q_bd, gates_t, kv.reshape(B, 1, KV_COLS), kcn,
                        cache_kv.reshape(DEPTH, n_pool, page, 4 * gd), kcpool,
                        win_buf.reshape(DEPTH, B, win_len, 2 * gd), l, C['tables'], name=f"nsa_s{l}")
    o_nsa = o_nsa.reshape(B, Q_COLS)
    new_rows = kv[:, 0:4 * gd].reshape(B, 1, 4, N_KV, HEAD_DIM)
    win_state = jnp.concatenate([win_buf[l][:, 1:], kv[:, 4 * gd:].reshape(B, 1, 2, N_KV, HEAD_DIM)], axis=1)

    conv_prev = C['state_ssm_conv'][l]
    xact, dt, dec = _conv_step(xbc, conv_prev, P['ssm_conv_w'][l], P['ssm_conv_b'][l], small,
                               P['dt_bias'][l], P['a_log'][l], name=f"conv_s{l}")
    conv_state = jnp.concatenate([conv_prev[:, 1:], xbc[:, None]], axis=1)
    y, ssm_state = _ssd_step(C['state_ssm'], l, xact, dt, dec, z, P['d_skip'][l], P['ssm_norm_g'][l],
                             name=f"ssd_s{l}")

    u = _merge(o_nsa.astype(BF16), y.reshape(B, D_INNER), Wl['a'], Wl['b'], gates, tm=B, tn=512, name=f"merge_s{l}")
    x = _mm(u, Wl['o'], tm=B, tn=512, name=f"proj_o_s{l}",
            epilogue=lambda acc, xr, gr: xr + gr * acc, extras=[('full', x), ('full', g1)])

    h2 = _norm_mod(x, P['norm2_g'][l], sc2, sh2, tm=B, rows_per_batch=1, name=f"norm2_s{l}")
    ffn_prev = C['state_ffn_conv'][l]
    act, gate_raw = _ffn_up_step(h2, Wl['gu'], ffn_prev, P['ffn_conv_w'][l], P['ffn_conv_b'][l],
                                 tn=D_FF // 2, name=f"ffn_up_s{l}")
    ffn_state = jnp.concatenate([ffn_prev[:, 1:], gate_raw[:, None]], axis=1)
    x = _mm(act, Wl['down'], tm=B, tn=512, name=f"ffn_down_s{l}",
            epilogue=lambda acc, xr, gr: xr + gr * acc, extras=[('full', x), ('full', g2)])
    return x, (new_rows, win_state, conv_state, ssm_state, ffn_state)


def kernel(x_prompt, x_sample, cache_kv, cache_win_kv, state_ssm_conv, state_ssm, state_ffn_conv, page_table, c_prompt, c_sample, w_ada, b_ada, norm1_g, norm2_g, final_g, w_in, rel_bias, cmp_pe, cmp_w1, cmp_w2, ssm_conv_w, ssm_conv_b, dt_bias, a_log, d_skip, ssm_norm_g, w_a, w_b, w_o, w_gu, ffn_conv_w, ffn_conv_b, w_down):
    P = dict(norm1_g=norm1_g, norm2_g=norm2_g, rel_bias=rel_bias, cmp_pe=cmp_pe.reshape(DEPTH, 2, -1),
             ssm_conv_w=ssm_conv_w, ssm_conv_b=ssm_conv_b, dt_bias=dt_bias, a_log=a_log, d_skip=d_skip,
             ssm_norm_g=ssm_norm_g, ffn_conv_w=ffn_conv_w, ffn_conv_b=ffn_conv_b)
    Bp, T, _ = x_prompt.shape
    Bs = x_sample.shape[0]
    past_len = page_table.shape[1] * cache_kv.shape[2]
    tables = _bias_tables(rel_bias)
    n_c = Bp + Bs
    c_rows = -(-n_c // SUBLANES) * SUBLANES
    c_all = jnp.zeros((c_rows, D_MODEL), F32).at[:Bp].set(c_prompt).at[Bp:n_c].set(c_sample).astype(BF16)
    xp = x_prompt.reshape(Bp * T, D_MODEL)
    xs = x_sample.reshape(Bs, D_MODEL)
    C = dict(cache_kv=cache_kv, page_table=page_table, cache_win_kv=cache_win_kv, state_ssm_conv=state_ssm_conv,
             state_ssm=state_ssm, state_ffn_conv=state_ffn_conv,
             tables=_decode_tables(rel_bias, past_len, page_table.shape[1], cache_win_kv.shape[2]))
    st_p, st_s = [], []
    for l in range(DEPTH):
        Wl = _layer_weights(l, w_ada, w_in, cmp_pe, cmp_w1, cmp_w2, w_a, w_b, w_o, w_gu, w_down)
        mod = _mm(c_all, Wl['ada'], tm=c_rows, tn=512, name=f"ada{l}",
                  epilogue=lambda acc, b: acc + b, extras=[('col', b_ada[l].reshape(1, -1))])
        xp, sp = _prompt_layer(xp, mod[:Bp], l, Wl, P, tables, Bp, T)
        st_p.append(sp)
        xs, ss = _sample_layer(xs, mod[Bp:n_c], l, Wl, P, C)
        st_s.append(ss)
    y_prompt = _final_norm(xp, final_g, tm=512, name="final_norm_p").reshape(Bp, T, D_MODEL)
    y_sample = _final_norm(xs, final_g, tm=Bs, name="final_norm_s").reshape(Bs, 1, D_MODEL)
    kv_p, win_p, conv_p, ssm_p, ffn_p = [jnp.stack([s[i] for s in st_p]) for i in range(5)]
    kv_s, win_s, conv_s, ssm_s, ffn_s = [jnp.stack([s[i] for s in st_s]) for i in range(5)]
    return (y_prompt, y_sample, kv_p, win_p, conv_p, ssm_p, ffn_p, kv_s, win_s, conv_s, ssm_s, ffn_s)
```

```python
import functools
import math

import jax
import jax.numpy as jnp
from jax import lax
from jax.experimental import pallas as pl
from jax.experimental.pallas import tpu as pltpu

D_MODEL = 1024
N_HEADS = 16
N_KV = 4
HEAD_DIM = 64
Q_PER_KV = N_HEADS // N_KV
L_CMP = 32
L_SEL = 64
TOP_N = 16
WINDOW = 512
CMP_HID = 128
Q_BLOCK = 128
SCALE = HEAD_DIM ** -0.5
N_BUCKETS = 32
MAX_DISTANCE = 128
D_INNER = 2 * D_MODEL
SSM_HEADDIM = 64
SSM_HEADS = D_INNER // SSM_HEADDIM
SSM_GROUPS = 4
D_STATE = 128
SSM_CONV = 4
SSM_CHUNK = 128
CONV_DIM = D_INNER + 2 * SSM_GROUPS * D_STATE
D_FF = 2816
FFN_CONV = 3
EPS = 1e-6
NEG_INF = -1e30
DEPTH = 2

Q_COLS = N_HEADS * HEAD_DIM
KV_COLS = 6 * N_KV * HEAD_DIM
NSA_GATE_COLS = 3 * N_HEADS
OFF_KV = Q_COLS
OFF_G = OFF_KV + KV_COLS
OFF_Z = OFF_G + NSA_GATE_COLS
OFF_XBC = OFF_Z + D_INNER
OFF_DT = OFF_XBC + CONV_DIM
OFF_MG = OFF_DT + SSM_HEADS
P_IN = OFF_MG + 2 * D_MODEL

LOG2E = math.log2(math.e)

LANES = 128
SUBLANES = 8
BF16_ROWS = 16
TQ = 128
WIDTH = Q_PER_KV * TQ
FAR_TILES = 4
WIN_TILES = WINDOW // TQ + 1
AUG_ROW0 = HEAD_DIM
VMEM_LIMIT = 48 * 1024 * 1024

BF16 = jnp.bfloat16
F32 = jnp.float32


def _cparams(sem):
    return pltpu.CompilerParams(dimension_semantics=sem, vmem_limit_bytes=VMEM_LIMIT)


def _silu(x):
    return x * jax.nn.sigmoid(x)


def _dot(a, b):
    return jnp.dot(a, b, preferred_element_type=F32)


def _dot_nt(a, b):
    return lax.dot_general(a, b, (((1,), (1,)), ((), ())), preferred_element_type=F32)


def _dot_tn(a, b):
    return lax.dot_general(a, b, (((0,), (0,)), ((), ())), preferred_element_type=F32)


def _split3(x):
    hi = x.astype(BF16)
    r1 = x - hi.astype(F32)
    mid = r1.astype(BF16)
    lo = (r1 - mid.astype(F32)).astype(BF16)
    return hi, mid, lo


def _mm_body(a_ref, w_ref, *refs, epilogue, n_extra):
    acc = _dot(a_ref[...], w_ref[...])
    extras = [r[...] for r in refs[:n_extra]]
    o_ref = refs[n_extra]
    o_ref[...] = epilogue(acc, *extras).astype(o_ref.dtype)


def _mm(a, w, *, tm, tn, name, out_dtype=F32, epilogue=None, extras=(), rows_per_batch=None):
    M, K = a.shape
    N = w.shape[1]
    assert M % tm == 0 and N % tn == 0, (M, N, tm, tn)
    in_specs = [pl.BlockSpec((tm, K), lambda i, j: (i, 0)),
                pl.BlockSpec((K, tn), lambda i, j: (0, j))]
    ops = [a, w]
    for kind, arr in extras:
        if kind == 'col':
            in_specs.append(pl.BlockSpec((1, tn), lambda i, j: (0, j)))
        elif kind == 'full':
            in_specs.append(pl.BlockSpec((tm, tn), lambda i, j: (i, j)))
        else:
            assert kind == 'batch' and rows_per_batch % tm == 0
            tpb = rows_per_batch // tm
            in_specs.append(pl.BlockSpec((None, 1, tn), lambda i, j, tpb=tpb: (i // tpb, 0, j)))
        ops.append(arr)
    if epilogue is None:
        epilogue = lambda acc: acc
    return pl.pallas_call(
        functools.partial(_mm_body, epilogue=epilogue, n_extra=len(extras)),
        grid=(M // tm, N // tn),
        in_specs=in_specs,
        out_specs=pl.BlockSpec((tm, tn), lambda i, j: (i, j)),
        out_shape=jax.ShapeDtypeStruct((M, N), out_dtype),
        compiler_params=_cparams(("parallel", "parallel")),
        name=name,
    )(*ops)


def _norm_mod_body(x_ref, g_ref, sc_ref, sh_ref, o_ref):
    x = x_ref[...]
    y = x * lax.rsqrt(jnp.mean(x * x, axis=-1, keepdims=True) + EPS) * g_ref[...]
    o_ref[...] = (y * (1.0 + sc_ref[...]) + sh_ref[...]).astype(o_ref.dtype)


def _norm_mod(x, g, sc, sh, *, tm, rows_per_batch, name):
    M, D = x.shape
    if sc.ndim == 3:
        tpb = rows_per_batch // tm
        mod_spec = pl.BlockSpec((None, 1, D), lambda i: (i // tpb, 0, 0))
    else:
        mod_spec = pl.BlockSpec((tm, D), lambda i: (i, 0))
    return pl.pallas_call(
        _norm_mod_body,
        grid=(M // tm,),
        in_specs=[pl.BlockSpec((tm, D), lambda i: (i, 0)), pl.BlockSpec((1, D), lambda i: (0, 0)),
                  mod_spec, mod_spec],
        out_specs=pl.BlockSpec((tm, D), lambda i: (i, 0)),
        out_shape=jax.ShapeDtypeStruct((M, D), BF16),
        compiler_params=_cparams(("parallel",)),
        name=name,
    )(x, g.reshape(1, D), sc, sh)


def _final_norm_body(x_ref, g_ref, o_ref):
    x = x_ref[...]
    o_ref[...] = x * lax.rsqrt(jnp.mean(x * x, axis=-1, keepdims=True) + EPS) * g_ref[...]


def _final_norm(x, g, *, tm, name):
    M, D = x.shape
    return pl.pallas_call(
        _final_norm_body,
        grid=(M // tm,),
        in_specs=[pl.BlockSpec((tm, D), lambda i: (i, 0)), pl.BlockSpec((1, D), lambda i: (0, 0))],
        out_specs=pl.BlockSpec((tm, D), lambda i: (i, 0)),
        out_shape=jax.ShapeDtypeStruct((M, D), F32),
        compiler_params=_cparams(("parallel",)),
        name=name,
    )(x, g.reshape(1, D))


def _merge_body(o_ref, y_ref, wa_ref, wb_ref, ga_ref, gb_ref, u_ref):
    pa = _dot(o_ref[...], wa_ref[...])
    pb = _dot(y_ref[...], wb_ref[...])
    u_ref[...] = (ga_ref[...] * pa + gb_ref[...] * pb).astype(u_ref.dtype)


def _merge(o_nsa, y, w_a, w_b, gates, *, tm, tn, name):
    M = o_nsa.shape[0]
    N = w_a.shape[1]
    nj = N // tn
    return pl.pallas_call(
        _merge_body,
        grid=(M // tm, nj),
        in_specs=[pl.BlockSpec((tm, o_nsa.shape[1]), lambda i, j: (i, 0)),
                  pl.BlockSpec((tm, y.shape[1]), lambda i, j: (i, 0)),
                  pl.BlockSpec((w_a.shape[0], tn), lambda i, j: (0, j)),
                  pl.BlockSpec((w_b.shape[0], tn), lambda i, j: (0, j)),
                  pl.BlockSpec((tm, tn), lambda i, j: (i, j)),
                  pl.BlockSpec((tm, tn), lambda i, j, nj=nj: (i, j + nj))],
        out_specs=pl.BlockSpec((tm, tn), lambda i, j: (i, j)),
        out_shape=jax.ShapeDtypeStruct((M, N), BF16),
        compiler_params=_cparams(("parallel", "parallel")),
        name=name,
    )(o_nsa, y, w_a, w_b, gates, gates)


def _ffn_up_body(h_ref, wg_ref, wu_ref, cw_ref, cb_ref, o_ref, pad_ref, *, tm, tiles_per_batch):
    i = pl.program_id(1)
    h = h_ref[...]
    gate = _dot(h, wg_ref[...])
    up = _dot(h, wu_ref[...])

    @pl.when(i % tiles_per_batch == 0)
    def _():
        pad_ref[0:SUBLANES, :] = jnp.zeros((SUBLANES, pad_ref.shape[1]), F32)

    pad_ref[SUBLANES:SUBLANES + tm, :] = gate
    y = cb_ref[...]
    for k in range(FFN_CONV):
        off = SUBLANES - (FFN_CONV - 1) + k
        y = y + cw_ref[k:k + 1, :] * pad_ref[off:off + tm, :]
    pad_ref[0:SUBLANES, :] = pad_ref[tm:tm + SUBLANES, :]
    o_ref[...] = (_silu(y) * up).astype(o_ref.dtype)


def _ffn_up(h, w_gu, conv_w, conv_b, *, tm, tn, rows_per_batch, name):
    M, K = h.shape
    nj = D_FF // tn
    return pl.pallas_call(
        functools.partial(_ffn_up_body, tm=tm, tiles_per_batch=rows_per_batch // tm),
        grid=(nj, M // tm),
        in_specs=[pl.BlockSpec((tm, K), lambda j, i: (i, 0)),
                  pl.BlockSpec((K, tn), lambda j, i: (0, j)),
                  pl.BlockSpec((K, tn), lambda j, i, nj=nj: (0, j + nj)),
                  pl.BlockSpec((FFN_CONV, tn), lambda j, i: (0, j)),
                  pl.BlockSpec((1, tn), lambda j, i: (0, j))],
        out_specs=pl.BlockSpec((tm, tn), lambda j, i: (i, j)),
        out_shape=jax.ShapeDtypeStruct((M, D_FF), BF16),
        scratch_shapes=[pltpu.VMEM((tm + SUBLANES, tn), F32)],
        compiler_params=_cparams(("arbitrary", "arbitrary")),
        name=name,
    )(h, w_gu, w_gu, conv_w, conv_b.reshape(1, D_FF))


def _compress_body(x_ref, pe_ref, w1_ref, w2_ref, o_ref):
    xb = (x_ref[...] + pe_ref[...]).astype(BF16)
    hid = _silu(_dot(xb, w1_ref[...]))
    o_ref[...] = _dot(hid.astype(BF16), w2_ref[...])


def _compress(x, pe, w1, w2, *, tm, name):
    rows, kdim = x.shape
    return pl.pallas_call(
        _compress_body,
        grid=(rows // tm,),
        in_specs=[pl.BlockSpec((tm, kdim), lambda i: (i, 0)), pl.BlockSpec((1, kdim), lambda i: (0, 0)),
                  pl.BlockSpec((kdim, CMP_HID), lambda i: (0, 0)),
                  pl.BlockSpec((CMP_HID, HEAD_DIM), lambda i: (0, 0))],
        out_specs=pl.BlockSpec((tm, HEAD_DIM), lambda i: (i, 0)),
        out_shape=jax.ShapeDtypeStruct((rows, HEAD_DIM), F32),
        compiler_params=_cparams(("parallel",)),
        name=name,
    )(x, pe.reshape(1, kdim), w1, w2)


def _attend(s, vt):
    m = jnp.max(s, axis=0, keepdims=True)
    p = jnp.exp2(s - m)
    return m, jnp.sum(p, axis=0, keepdims=True), _dot(vt, p.astype(BF16))


def _lane_concat(tiles3):
    return jnp.concatenate([tiles3[t] for t in range(tiles3.shape[0])], axis=1)


def _nsa_prompt_body(q_ref, ksel_ref, vsel_ref, kwin_ref, vwin_ref, kc_ref, vct_ref, gate_ref,
                     tzw_ref, tzn_ref, band_ref, caug_ref, o_ref,
                     qaug_ref, m_ref, l_ref, acc_ref, sel0_ref, selfar_ref, sa_scr, sb_scr, *, n_blocks, n_cmp, top_n):
    qi = pl.program_id(2)
    q0 = qi * TQ
    n_grp_rows = n_blocks // SUBLANES

    @pl.when(qi == 0)
    def _():
        selfar_ref[:, SUBLANES:2 * SUBLANES, :] = jnp.broadcast_to(
            caug_ref[...][None], (selfar_ref.shape[0], SUBLANES, WIDTH))
        selfar_ref[n_grp_rows:, 0:SUBLANES, :] = jnp.full(
            (selfar_ref.shape[0] - n_grp_rows, SUBLANES, WIDTH), NEG_INF, F32)
        sel0_ref[0:SUBLANES, :] = jnp.zeros((SUBLANES, WIDTH), F32)

    qaug_ref[0:HEAD_DIM, :] = q_ref[...]
    qaug_ref[HEAD_DIM:, :] = jnp.zeros((qaug_ref.shape[0] - HEAD_DIM, WIDTH), BF16)
    q = q_ref[...]

    half = n_cmp // 2
    row = lax.broadcasted_iota(jnp.int32, (n_cmp, 16), 0)
    cidx = jnp.where(row < half, 2 * row, 2 * (row - half) + 1)
    crel = cidx - (TQ // L_CMP) * qi
    cat = jnp.where(crel <= -5, 8, jnp.where(crel >= 4, 9, crel + 4))
    place = (cat == lax.broadcasted_iota(jnp.int32, (n_cmp, 16), 1)).astype(BF16)
    b_hi, b_mid, b_lo = _split3(band_ref[...])
    bias_c = _dot(place, b_hi) + _dot(place, b_mid) + _dot(place, b_lo)
    s = _dot(kc_ref[...], q) + bias_c
    m = jnp.max(s, axis=0, keepdims=True)
    e = jnp.exp2(s - m)
    denom = jnp.sum(e, axis=0, keepdims=True)
    anyvalid = (m > 0.5 * NEG_INF).astype(F32)
    p = e * (anyvalid / denom)
    o_cmp = _dot(vct_ref[...], p.astype(BF16))
    psum = p[:, 0:TQ]
    for r in range(1, Q_PER_KV):
        psum = psum + p[:, r * TQ:(r + 1) * TQ]
    imp = psum[0:half, :] + psum[half:n_cmp, :]

    blk = lax.broadcasted_iota(jnp.int32, (n_blocks, TQ), 0)
    blk_f = blk.astype(F32)
    qpos = q0 + lax.broadcasted_iota(jnp.int32, (n_blocks, TQ), 1)
    forced = (blk == 0) | (blk == lax.shift_right_logical(qpos, 6))
    valid = blk * L_SEL <= qpos
    score = jnp.where(forced, 1e30, jnp.where(valid, imp, -1.0))
    picked = jnp.zeros((n_blocks, TQ), jnp.bool_)
    for _ in range(top_n):
        mx = jnp.max(score, axis=0, keepdims=True)
        first = jnp.min(jnp.where(score == mx, blk_f, float(n_blocks)), axis=0, keepdims=True)
        hit = blk_f == first
        picked = picked | hit
        score = jnp.where(hit, -2.0, score)
    seladd = jnp.where(picked, 0.0, NEG_INF)
    selfar = jnp.where(blk < 2 * (qi - 1), seladd, NEG_INF)
    sel0_ref[SUBLANES:, :] = jnp.concatenate([seladd] * Q_PER_KV, axis=1)
    selfar_ref[0:n_grp_rows, 0:SUBLANES, :] = jnp.concatenate([selfar] * Q_PER_KV, axis=1).reshape(
        n_grp_rows, SUBLANES, WIDTH)

    kw = kwin_ref[pl.ds(qi, WIN_TILES)].reshape(WIN_TILES * TQ, LANES)
    m_w, l_w, a_w = _attend(_dot(kw, qaug_ref[...]) + tzw_ref[...], _lane_concat(vwin_ref[pl.ds(qi, WIN_TILES)]))
    o_win = a_w * (1.0 / l_w)

    m_ref[...] = jnp.full((1, WIDTH), NEG_INF, F32)
    l_ref[...] = jnp.zeros((1, WIDTH), F32)
    acc_ref[...] = jnp.zeros((HEAD_DIM, WIDTH), F32)

    def far_scores(gi):
        qaug_ref[AUG_ROW0:AUG_ROW0 + BF16_ROWS, :] = selfar_ref[gi].astype(BF16)
        kg = ksel_ref[pl.ds(1 + FAR_TILES * gi, FAR_TILES)].reshape(FAR_TILES * TQ, LANES)
        return _dot(kg, qaug_ref[...])

    def far_update(s, gi):
        m_g, l_g, a_g = _attend(s, _lane_concat(vsel_ref[pl.ds(1 + FAR_TILES * gi, FAR_TILES)]))
        m_old = m_ref[...]
        m_new = jnp.maximum(m_old, m_g)
        a_old = jnp.exp2(m_old - m_new)
        a_grp = jnp.exp2(m_g - m_new)
        l_ref[...] = l_ref[...] * a_old + l_g * a_grp
        acc_ref[...] = acc_ref[...] * a_old + a_g * a_grp
        m_ref[...] = m_new

    sa_scr[...] = far_scores(0)

    def far_body(j, carry):
        sb_scr[...] = far_scores(2 * j + 1)
        far_update(sa_scr[...], 2 * j)
        sa_scr[...] = far_scores(2 * j + 2)
        far_update(sb_scr[...], 2 * j + 1)
        return carry

    n_far = jnp.maximum(qi - 1, 0)
    n_groups = lax.shift_right_logical(n_far + (FAR_TILES - 1), 2)
    lax.fori_loop(0, lax.shift_right_logical(n_groups + 1, 1), far_body, 0)

    qaug_ref[AUG_ROW0:AUG_ROW0 + BF16_ROWS, :] = jnp.zeros((BF16_ROWS, WIDTH), BF16)
    near_mask = jnp.concatenate(
        [jnp.broadcast_to(sel0_ref[pl.ds(SUBLANES + 2 * (qi - 1) + i, 1), :], (L_SEL, WIDTH)) for i in range(4)],
        axis=0)
    kn = ksel_ref[pl.ds(qi, 2)].reshape(2 * TQ, LANES)
    m_n, l_n, a_n = _attend(_dot(kn, qaug_ref[...]) + tzn_ref[...] + near_mask, _lane_concat(vsel_ref[pl.ds(qi, 2)]))
    m_old = m_ref[...]
    m_new = jnp.maximum(m_old, m_n)
    a_old = jnp.exp2(m_old - m_new)
    a_grp = jnp.exp2(m_n - m_new)
    o_sel = (acc_ref[...] * a_old + a_n * a_grp) * (1.0 / (l_ref[...] * a_old + l_n * a_grp))

    g = jax.nn.sigmoid(gate_ref[...])
    o = g[0:1, :] * o_cmp + g[1:2, :] * o_sel + g[2:3, :] * o_win
    for r in range(Q_PER_KV):
        o_ref[r * HEAD_DIM:(r + 1) * HEAD_DIM, :] = o[:, r * TQ:(r + 1) * TQ].astype(o_ref.dtype)


def _nsa_prompt(q2, ksel, vsel, kwin, vwin, kc, vct, gates, tables, *, name):
    tzw, tzn, band, caug = tables
    B, G, nq, D, _ = q2.shape
    n_cmp = kc.shape[2]
    n_blocks = n_cmp // (L_SEL // L_CMP)
    assert n_blocks % SUBLANES == 0 and nq % FAR_TILES == 0
    T = nq * TQ
    whole = lambda arr: pl.BlockSpec((None, None) + arr.shape[2:], lambda b, g, i: (b, g, 0, 0, 0))
    return pl.pallas_call(
        functools.partial(_nsa_prompt_body, n_blocks=n_blocks, n_cmp=n_cmp, top_n=min(TOP_N, n_blocks)),
        grid=(B, G, nq),
        in_specs=[pl.BlockSpec((None, None, None, D, WIDTH), lambda b, g, i: (b, g, i, 0, 0)),
                  whole(ksel), whole(vsel), whole(kwin), whole(vwin),
                  pl.BlockSpec((None, None, n_cmp, D), lambda b, g, i: (b, g, 0, 0)),
                  pl.BlockSpec((None, None, D, n_cmp), lambda b, g, i: (b, g, 0, 0)),
                  pl.BlockSpec((None, None, None, 3, WIDTH), lambda b, g, i: (b, g, i, 0, 0)),
                  pl.BlockSpec((None, None, WIN_TILES * TQ, WIDTH),
                               lambda b, g, i: (g, jnp.minimum(i, WIN_TILES - 1), 0, 0)),
                  pl.BlockSpec((None, None, 2 * TQ, WIDTH), lambda b, g, i: (g, jnp.minimum(i, 1), 0, 0)),
                  pl.BlockSpec((None, 16, WIDTH), lambda b, g, i: (g, 0, 0)),
                  pl.BlockSpec((None, SUBLANES, WIDTH), lambda b, g, i: (g, 0, 0))],
        out_specs=pl.BlockSpec((None, Q_PER_KV * D, TQ), lambda b, g, i: (b, g, i)),
        out_shape=jax.ShapeDtypeStruct((B, G * Q_PER_KV * D, T), BF16),
        scratch_shapes=[pltpu.VMEM((LANES, WIDTH), BF16),
                        pltpu.VMEM((1, WIDTH), F32), pltpu.VMEM((1, WIDTH), F32),
                        pltpu.VMEM((D, WIDTH), F32),
                        pltpu.VMEM((n_blocks + SUBLANES, WIDTH), F32),
                        pltpu.VMEM((nq // FAR_TILES + 2, 2 * SUBLANES, WIDTH), F32),
                        pltpu.VMEM((FAR_TILES * TQ, WIDTH), F32), pltpu.VMEM((FAR_TILES * TQ, WIDTH), F32)],
        compiler_params=_cparams(("arbitrary", "arbitrary", "arbitrary")),
        name=name,
    )(q2, ksel, vsel, kwin, vwin, kc, vct, gates, tzw, tzn, band, caug)


SSD_QUAD = 4

def _ssd_body(xbc_ref, dt_ref, z_ref, cw_ref, cb_ref, dtb_ref, alog_ref, dskip_ref, ng_ref,
              exp_h_ref, exp_p_ref, y_ref, st_ref, pad_ref, x_scr, colh_scr, colp_scr, dtp_scr, y_scr):
    c = pl.program_id(1)
    L = SSM_CHUNK
    P = SSM_HEADDIM
    heads_per_group = SSM_HEADS // SSM_GROUPS

    @pl.when(c == 0)
    def _():
        pad_ref[0:SUBLANES, :] = jnp.zeros((SUBLANES, CONV_DIM), F32)
        st_ref[...] = jnp.zeros(st_ref.shape, F32)

    pad_ref[SUBLANES:SUBLANES + L, :] = xbc_ref[...]
    conv = cb_ref[...]
    for k in range(SSM_CONV):
        off = SUBLANES - (SSM_CONV - 1) + k
        conv = conv + cw_ref[k:k + 1, :] * pad_ref[off:off + L, :]
    pad_ref[0:SUBLANES, :] = pad_ref[L:L + SUBLANES, :]
    xbc = _silu(conv)
    x_scr[...] = xbc[:, 0:D_INNER]
    bmat = [xbc[:, D_INNER + g * D_STATE:D_INNER + (g + 1) * D_STATE].astype(BF16) for g in range(SSM_GROUPS)]
    cmat = [xbc[:, D_INNER + (SSM_GROUPS + g) * D_STATE:D_INNER + (SSM_GROUPS + g + 1) * D_STATE].astype(BF16)
            for g in range(SSM_GROUPS)]

    x = dt_ref[:, 0:SSM_HEADS] + dtb_ref[...]
    dt = jnp.maximum(x, 0.0) + jnp.log1p(jnp.exp(-jnp.abs(x)))
    a = dt * (-jnp.exp(alog_ref[...]))
    tri = lax.broadcasted_iota(jnp.int32, (L, L), 0) >= lax.broadcasted_iota(jnp.int32, (L, L), 1)
    tri_b = tri.astype(BF16)
    a_hi, a_mid, a_lo = _split3(a)
    a_cs = _dot(tri_b, a_hi) + _dot(tri_b, a_mid) + _dot(tri_b, a_lo)
    acs_parts = jnp.concatenate(_split3(a_cs), axis=1)
    dt_parts = jnp.concatenate(_split3(dt), axis=1)
    colh_scr[...] = _dot(acs_parts, exp_h_ref[...])
    colp_scr[...] = _dot(acs_parts, exp_p_ref[...])
    dtp_scr[...] = _dot(dt_parts, exp_p_ref[...])
    acs_t = a_cs.T
    dt_t = dt.T
    tri_q = jnp.concatenate([tri] * SSD_QUAD, axis=1)

    for qd in range(SSM_HEADS // SSD_QUAD):
        g = (qd * SSD_QUAD) // heads_per_group
        h0 = qd * SSD_QUAD
        b_g, c_g = bmat[g], cmat[g]
        cb = _dot_nt(c_g, b_g)
        col = colh_scr[:, h0 * LANES:(h0 + SSD_QUAD) * LANES]
        row = jnp.concatenate([acs_t[h0 + i:h0 + i + 1, :] for i in range(SSD_QUAD)], axis=1)
        dtrow = jnp.concatenate([dt_t[h0 + i:h0 + i + 1, :] for i in range(SSD_QUAD)], axis=1)
        decay = jnp.exp(jnp.where(tri_q, col - row, NEG_INF))
        w = (jnp.concatenate([cb] * SSD_QUAD, axis=1) * decay * dtrow).astype(BF16)
        xq = x_scr[:, h0 * P:(h0 + SSD_QUAD) * P]
        xq_b = xq.astype(BF16)
        y_diag = jnp.concatenate([_dot(w[:, i * LANES:(i + 1) * LANES], xq_b[:, i * P:(i + 1) * P])
                                  for i in range(SSD_QUAD)], axis=1)
        h_prev = st_ref[h0:h0 + SSD_QUAD].reshape(SSD_QUAD * P, D_STATE)
        colp = colp_scr[:, h0 * P:(h0 + SSD_QUAD) * P]
        y_off = _dot_nt(c_g, h_prev.astype(BF16)) * jnp.exp(colp)
        y_scr[:, h0 * P:(h0 + SSD_QUAD) * P] = y_diag + y_off
        decay_end = jnp.exp(colp[L - 1:L, :] - colp)
        xw = (decay_end * dtp_scr[:, h0 * P:(h0 + SSD_QUAD) * P]) * xq
        st = _dot_tn(xw.astype(BF16), b_g)
        chunk_decay = jnp.exp(col[L - 1:L, :])
        cd = jnp.concatenate([jnp.broadcast_to(chunk_decay[:, i * LANES:(i + 1) * LANES], (P, D_STATE))
                              for i in range(SSD_QUAD)], axis=0)
        st_ref[h0:h0 + SSD_QUAD] = (h_prev * cd + st).reshape(SSD_QUAD, P, D_STATE)

    group_w = D_INNER // SSM_GROUPS
    y = y_scr[...] + dskip_ref[...] * x_scr[...]
    y = y * _silu(z_ref[...])
    for g in range(SSM_GROUPS):
        yg = y[:, g * group_w:(g + 1) * group_w]
        inv = lax.rsqrt(jnp.mean(yg * yg, axis=-1, keepdims=True) + EPS)
        y_ref[:, g * group_w:(g + 1) * group_w] = (
            yg * inv * ng_ref[:, g * group_w:(g + 1) * group_w]).astype(y_ref.dtype)


def _ssd_prompt(xbc, dtg, z, conv_w, conv_b, dt_bias, a_log, d_skip, norm_g, *, batch, name):
    M = xbc.shape[0]
    L = SSM_CHUNK
    nc = M // batch // L
    heads = jnp.arange(SSM_HEADS)

    def expand(lanes):
        e = (heads[:, None, None] == heads[None, :, None])
        e = jnp.broadcast_to(e, (SSM_HEADS, SSM_HEADS, lanes)).reshape(SSM_HEADS, SSM_HEADS * lanes)
        return jnp.concatenate([e] * 3, axis=0).astype(BF16)

    row = lambda v: v.reshape(1, -1)
    const2 = lambda shape: pl.BlockSpec(shape, lambda b, c: (0, 0))
    return pl.pallas_call(
        _ssd_body,
        grid=(batch, nc),
        in_specs=[pl.BlockSpec((L, CONV_DIM), lambda b, c: (b * nc + c, 0)),
                  pl.BlockSpec((L, LANES), lambda b, c: (b * nc + c, 1)),
                  pl.BlockSpec((L, D_INNER), lambda b, c: (b * nc + c, 0)),
                  const2((SSM_CONV, CONV_DIM)), const2((1, CONV_DIM)),
                  const2((1, SSM_HEADS)), const2((1, SSM_HEADS)),
                  const2((1, D_INNER)), const2((1, D_INNER)),
                  const2((3 * SSM_HEADS, SSM_HEADS * LANES)), const2((3 * SSM_HEADS, D_INNER))],
        out_specs=[pl.BlockSpec((L, D_INNER), lambda b, c: (b * nc + c, 0)),
                   pl.BlockSpec((None, SSM_HEADS, SSM_HEADDIM, D_STATE), lambda b, c: (b, 0, 0, 0))],
        out_shape=[jax.ShapeDtypeStruct((M, D_INNER), BF16),
                   jax.ShapeDtypeStruct((batch, SSM_HEADS, SSM_HEADDIM, D_STATE), F32)],
        scratch_shapes=[pltpu.VMEM((L + SUBLANES, CONV_DIM), F32),
                        pltpu.VMEM((L, D_INNER), F32),
                        pltpu.VMEM((L, SSM_HEADS * LANES), F32),
                        pltpu.VMEM((L, D_INNER), F32),
                        pltpu.VMEM((L, D_INNER), F32),
                        pltpu.VMEM((L, D_INNER), F32)],
        compiler_params=_cparams(("arbitrary", "arbitrary")),
        name=name,
    )(xbc, dtg, z, conv_w, row(conv_b), row(dt_bias), row(a_log),
      row(jnp.repeat(d_skip, SSM_HEADDIM)), row(norm_g), expand(LANES), expand(SSM_HEADDIM))


def _t5_bucket(dist):
    n = jnp.maximum(dist, 0)
    max_exact = N_BUCKETS // 2
    nf = jnp.maximum(n, max_exact).astype(F32)
    large = max_exact + (jnp.log(nf / max_exact) / math.log(MAX_DISTANCE / max_exact)
                         * (N_BUCKETS - max_exact)).astype(jnp.int32)
    return jnp.where(n < max_exact, n, jnp.minimum(large, N_BUCKETS - 1))


def _bias_tables(rel_bias):
    by_dist = rel_bias[_t5_bucket(jnp.arange((WIN_TILES + 1) * TQ))]
    key = jnp.arange(TQ)[:, None]
    qry = jnp.arange(TQ)[None, :]
    delta = (jnp.arange(WIN_TILES - 1, -1, -1) * TQ)[:, None, None]
    dist = delta + qry[None] - key[None]
    ok = (dist >= 0) & (dist <= WINDOW)
    tz = jnp.where(ok[..., None], by_dist[jnp.clip(dist, 0)], NEG_INF)
    tz = tz.reshape(WIN_TILES, TQ, TQ, N_KV, Q_PER_KV).transpose(3, 0, 1, 4, 2)
    tz = tz.reshape(N_KV, WIN_TILES, TQ, WIDTH)
    tile = jnp.arange(WIN_TILES)[None, :, None, None]

    def variants(tiles, n_var):
        n_tiles = tiles.shape[1]
        v = jnp.arange(n_var)[:, None, None, None]
        masked = tile[:, :n_tiles] < (n_tiles - 1 - v)
        out = jnp.where(masked[None], NEG_INF, tiles[:, None])
        return out.reshape(N_KV, n_var, n_tiles * TQ, WIDTH)

    tzw = variants(tz, WIN_TILES)
    tzn = variants(tz[:, WIN_TILES - 2:], 2)
    far = rel_bias[N_BUCKETS - 1]
    rel = jnp.arange(8)[:, None] - 4
    cdist = qry - L_CMP * rel - (L_CMP - 1)
    band = jnp.where((cdist >= 0)[..., None], by_dist[jnp.clip(cdist, 0)], NEG_INF)
    far_rows = jnp.broadcast_to(far[None, None, :], (1, TQ, N_HEADS))
    band = jnp.concatenate([band, far_rows, jnp.full((1, TQ, N_HEADS), NEG_INF, F32),
                            jnp.zeros((6, TQ, N_HEADS), F32)], axis=0)
    band = band.reshape(16, TQ, N_KV, Q_PER_KV).transpose(2, 0, 3, 1).reshape(N_KV, 16, WIDTH)
    parts = jnp.stack([p.astype(F32) for p in _split3(far)] + [jnp.zeros_like(far)] * (SUBLANES - 3))
    caug = jnp.broadcast_to(parts.reshape(SUBLANES, N_KV, Q_PER_KV, 1), (SUBLANES, N_KV, Q_PER_KV, TQ))
    caug = caug.transpose(1, 0, 2, 3).reshape(N_KV, SUBLANES, WIDTH)
    return tzw, tzn, band, caug


def _layer_weights(l, w_ada, w_in, cmp_pe, cmp_w1, cmp_w2, w_a, w_b, w_o, w_gu, w_down):
    wi = w_in[l]
    small = jnp.zeros((D_MODEL, 2 * LANES), F32)
    small = small.at[:, 0:NSA_GATE_COLS].set(wi[:, OFF_G:OFF_Z])
    small = small.at[:, LANES:LANES + SSM_HEADS].set(wi[:, OFF_DT:OFF_MG])
    c = lambda w: w.astype(BF16)
    w1r = cmp_w1[l].reshape(2, L_CMP, HEAD_DIM, CMP_HID)
    w1_pair = jnp.zeros((2, L_CMP, 2 * HEAD_DIM, 2 * CMP_HID), F32)
    w1_pair = w1_pair.at[:, :, 0:HEAD_DIM, 0:CMP_HID].set(w1r).at[:, :, HEAD_DIM:, CMP_HID:].set(w1r)
    pe_pair = jnp.tile(cmp_pe[l], (1, 1, 2))
    return dict(ada=c(w_ada[l]), q=c(wi[:, 0:OFF_KV]), kv=c(wi[:, OFF_KV:OFF_G]), small=c(small),
                w1_pair=c(w1_pair), pe_pair=pe_pair,
                z=c(wi[:, OFF_Z:OFF_XBC]), xbc=c(wi[:, OFF_XBC:OFF_DT]), mg=c(wi[:, OFF_MG:P_IN]),
                cmp_w1=c(cmp_w1[l]), cmp_w2=c(cmp_w2[l]), a=c(w_a[l]), b=c(w_b[l]), o=c(w_o[l]),
                gu=c(w_gu[l]), down=c(w_down[l]))


def _prompt_layer(x, mod, l, Wl, P, tables, batch, T):
    sh1, sc1, g1, sh2, sc2, g2 = [m.reshape(batch, 1, D_MODEL) for m in jnp.split(mod, 6, axis=-1)]
    tm = 1024 if T % 1024 == 0 else 512
    h = _norm_mod(x, P['norm1_g'][l], sc1, sh1, tm=512, rows_per_batch=T, name=f"norm1_p{l}")
    q = _mm(h, Wl['q'], tm=tm, tn=512, name=f"proj_q_p{l}", out_dtype=BF16,
            epilogue=lambda acc: acc * (SCALE * LOG2E))
    kv = _mm(h, Wl['kv'], tm=tm, tn=512, name=f"proj_kv_p{l}")
    small = _mm(h, Wl['small'], tm=tm, tn=2 * LANES, name=f"proj_small_p{l}")
    z = _mm(h, Wl['z'], tm=tm, tn=512, name=f"proj_z_p{l}")
    xbc = _mm(h, Wl['xbc'], tm=tm, tn=512, name=f"proj_xbc_p{l}")
    gates = _mm(h, Wl['mg'], tm=tm, tn=512, name=f"proj_mg_p{l}", epilogue=jax.nn.sigmoid)

    o_nsa = _prompt_attention(q, kv, small, l, Wl, P, tables, batch, T)
    kv6 = kv.reshape(batch, T, 6, N_KV, HEAD_DIM)

    y, ssm_state = _ssd_prompt(xbc, small, z, P['ssm_conv_w'][l], P['ssm_conv_b'][l], P['dt_bias'][l],
                               P['a_log'][l], P['d_skip'][l], P['ssm_norm_g'][l], batch=batch, name=f"ssd_p{l}")

    u = _merge(o_nsa, y, Wl['a'], Wl['b'], gates, tm=tm, tn=512, name=f"merge_p{l}")
    x = _mm(u, Wl['o'], tm=tm, tn=512, name=f"proj_o_p{l}", rows_per_batch=T,
            epilogue=lambda acc, xr, gr: xr + gr * acc, extras=[('full', x), ('batch', g1)])

    h2 = _norm_mod(x, P['norm2_g'][l], sc2, sh2, tm=512, rows_per_batch=T, name=f"norm2_p{l}")
    act = _ffn_up(h2, Wl['gu'], P['ffn_conv_w'][l], P['ffn_conv_b'][l], tm=512, tn=D_FF // 2,
                  rows_per_batch=T, name=f"ffn_up_p{l}")
    h2_last = h2.reshape(batch, T, D_MODEL)[:, T - SUBLANES:].reshape(batch * SUBLANES, D_MODEL)
    gate_last = _mm(h2_last, Wl['gu'][:, 0:D_FF], tm=batch * SUBLANES, tn=D_FF // 2, name=f"ffn_state_p{l}")
    ffn_state = gate_last.reshape(batch, SUBLANES, D_FF)[:, SUBLANES - (FFN_CONV - 1):]
    x = _mm(act, Wl['down'], tm=tm, tn=512, name=f"ffn_down_p{l}", rows_per_batch=T,
            epilogue=lambda acc, xr, gr: xr + gr * acc, extras=[('full', x), ('batch', g2)])

    kv_rows = kv6[:, :, :4]
    win_state = kv6[:, T - min(WINDOW, T):, 4:]
    conv_state = xbc.reshape(batch, T, CONV_DIM)[:, T - (SSM_CONV - 1):]
    return x, (kv_rows, win_state, conv_state, ssm_state, ffn_state)


def _prompt_attention(q, kv, small, l, Wl, P, tables, batch, T):
    nq = T // TQ
    kv6 = kv.reshape(batch, T, 6, N_KV, HEAD_DIM)
    n_cmp = T // L_CMP

    def cmp_in(slot):
        xb = kv6[:, :, slot].reshape(batch, n_cmp, L_CMP, N_KV, HEAD_DIM).transpose(0, 3, 1, 2, 4)
        return xb.reshape(batch * N_KV * n_cmp, L_CMP * HEAD_DIM)

    ctm = min(512, batch * N_KV * n_cmp)
    kc = _compress(cmp_in(0), P['cmp_pe'][l, 0], Wl['cmp_w1'][0], Wl['cmp_w2'][0], tm=ctm, name=f"cmp_k_p{l}")
    vc = _compress(cmp_in(1), P['cmp_pe'][l, 1], Wl['cmp_w1'][1], Wl['cmp_w2'][1], tm=ctm, name=f"cmp_v_p{l}")

    def even_odd(c):
        c = c.reshape(batch, N_KV, n_cmp // 2, 2, HEAD_DIM).transpose(0, 1, 3, 2, 4)
        return c.reshape(batch, N_KV, n_cmp, HEAD_DIM)

    kc = even_odd(kc.reshape(batch, N_KV, n_cmp, HEAD_DIM)).astype(BF16)
    vct = even_odd(vc.reshape(batch, N_KV, n_cmp, HEAD_DIM)).astype(BF16).transpose(0, 1, 3, 2)

    def k_tiles(slot, front, back, aug):
        k = kv6[:, :, slot].astype(BF16).reshape(batch, nq, TQ, N_KV, HEAD_DIM).transpose(0, 3, 1, 2, 4)
        k = jnp.pad(k, ((0, 0), (0, 0), (front, back), (0, 0), (0, LANES - HEAD_DIM)))
        if aug:
            kt = jnp.arange(front + nq + back) - front
            blk_col = AUG_ROW0 + 2 * (kt % FAR_TILES)[:, None] + (jnp.arange(TQ) // L_SEL)[None, :]
            lane = jnp.arange(LANES)[None, None, :]
            ones = (lane == blk_col[..., None]) | ((lane >= AUG_ROW0 + SUBLANES) & (lane < AUG_ROW0 + SUBLANES + 3))
            k = jnp.where(ones[None, None], jnp.ones((), BF16), k)
        return k

    def vt_tiles(slot, front, back):
        v = kv6[:, :, slot].astype(BF16).reshape(batch, nq, TQ, N_KV, HEAD_DIM).transpose(0, 3, 1, 4, 2)
        return jnp.pad(v, ((0, 0), (0, 0), (front, back), (0, 0), (0, 0)))

    q2 = q.reshape(batch, nq, TQ, N_KV, Q_PER_KV, HEAD_DIM).transpose(0, 3, 1, 5, 4, 2)
    q2 = q2.reshape(batch, N_KV, nq, HEAD_DIM, WIDTH)
    ng = small[:, 0:NSA_GATE_COLS].reshape(batch, nq, TQ, 3, N_KV, Q_PER_KV).transpose(0, 4, 1, 3, 5, 2)
    ng = ng.reshape(batch, N_KV, nq, 3, WIDTH)
    o_t = _nsa_prompt(q2, k_tiles(2, 1, 2 * FAR_TILES, True), vt_tiles(3, 1, 2 * FAR_TILES),
                      k_tiles(4, WIN_TILES - 1, 0, False), vt_tiles(5, WIN_TILES - 1, 0),
                      kc, vct, ng, tables, name=f"nsa_p{l}")
    return o_t.transpose(0, 2, 1).reshape(batch * T, Q_COLS)


def _pool_compress_body(x_ref, pe_ref, w1_ref, w2_ref, o_ref, rows_scr, *, pages, page):
    rows = pages * page // L_CMP
    for s in range(2):
        outs = []
        for pair in range(N_KV // 2):
            for p in range(pages):
                rows_scr[p * page:(p + 1) * page, :] = x_ref[p, s, 2 * pair:2 * pair + 2].reshape(
                    2 * HEAD_DIM, page).T
            xs = [(rows_scr[pl.ds(l, rows, stride=L_CMP), :] + pe_ref[s, l:l + 1, :]).astype(BF16)
                  for l in range(L_CMP)]
            w1 = w1_ref[s].reshape(L_CMP * 2 * HEAD_DIM, 2 * CMP_HID)
            hid = _silu(_dot(jnp.concatenate(xs, axis=1), w1)).astype(BF16)
            for gg in range(2):
                outs.append(_dot(hid[:, gg * CMP_HID:(gg + 1) * CMP_HID], w2_ref[s]))
        o_ref[:, s * N_KV * HEAD_DIM:(s + 1) * N_KV * HEAD_DIM] = jnp.concatenate(outs, axis=1)


def _pool_compress(cache_t, l, pe2, w1pair, w2, *, pages_per_step, name):
    n_pool, page = cache_t.shape[1], cache_t.shape[5]
    rows_out = pages_per_step * page // L_CMP
    half = 2 * N_KV * HEAD_DIM
    return pl.pallas_call(
        functools.partial(_pool_compress_body, pages=pages_per_step, page=page),
        grid=(n_pool // pages_per_step,),
        in_specs=[pl.BlockSpec((None, pages_per_step, 2, N_KV, HEAD_DIM, page), lambda i: (l, i, 0, 0, 0, 0)),
                  pl.BlockSpec(pe2.shape, lambda i: (0, 0, 0)),
                  pl.BlockSpec(w1pair.shape, lambda i: (0, 0, 0, 0)),
                  pl.BlockSpec(w2.shape, lambda i: (0, 0, 0))],
        out_specs=pl.BlockSpec((rows_out, half), lambda i: (i, 0)),
        out_shape=jax.ShapeDtypeStruct((n_pool * page // L_CMP, half), F32),
        scratch_shapes=[pltpu.VMEM((pages_per_step * page, 2 * HEAD_DIM), F32)],
        compiler_params=_cparams(("parallel",)),
        name=name,
    )(cache_t, pe2, w1pair, w2)


def _softmax_lanes(s):
    m = jnp.max(s, axis=1, keepdims=True)
    e = jnp.exp(s - m)
    return m, e, jnp.sum(e, axis=1, keepdims=True)


def _sum3(lhs_bf16, x):
    hi, mid, lo = _split3(x)
    return _dot(lhs_bf16, hi) + _dot(lhs_bf16, mid) + _dot(lhs_bf16, lo)


def _nsa_decode_body(pt_ref, q_ref, gate_ref, kvnew_ref, kcn_ref, *refs, n_pages, n_blocks, top_n, win_len):
    sel_pages = refs[0:n_pages]
    kc_pages = refs[n_pages:2 * n_pages]
    (win_ref, bsel_ref, bwin_ref, bcmp_ref, e_ref, pair_ref, rsum_ref, rexp_ref,
     o_ref, kc_scr, s_scr) = refs[2 * n_pages:]
    gd = N_KV * HEAD_DIM
    page = sel_pages[0].shape[3]
    blocks_per_page = kc_pages[0].shape[0]
    n_past_blocks = blocks_per_page * n_pages

    @pl.when(pl.program_id(0) == 0)
    def _():
        kc_scr[...] = jnp.zeros(kc_scr.shape, F32)

    q = q_ref[...]
    q_f = q.astype(F32)
    lane0 = lax.broadcasted_iota(jnp.int32, (N_HEADS, LANES), 1) == 0
    rb = lambda v: v.astype(BF16).astype(F32)

    def new_scores(slot):
        k_new = rb(kvnew_ref[:, slot * gd:(slot + 1) * gd])
        return jnp.where(lane0, jnp.sum(q_f * k_new, axis=1, keepdims=True), 0.0)

    def new_value(e_tile, slot):
        return rb(e_tile[:, 0:1]) * rb(kvnew_ref[:, slot * gd:(slot + 1) * gd])

    for p in range(n_pages):
        kc_scr[p * blocks_per_page:(p + 1) * blocks_per_page, :] = kc_pages[p][...]
    kc_scr[n_past_blocks:n_past_blocks + SUBLANES, :] = kcn_ref[...]
    kcv = kc_scr[...]
    m_c, e_c, den_c = _softmax_lanes(_dot_nt(q, kcv[:, 0:gd].astype(BF16)) + bcmp_ref[...])
    p_c = e_c * ((m_c > 0.5 * NEG_INF).astype(F32) / den_c)
    o_cmp = _dot(p_c.astype(BF16), kcv[:, gd:2 * gd].astype(BF16))
    imp = _sum3(rsum_ref[...], _sum3_rhs(p_c, pair_ref[...]))

    lane_i = lax.broadcasted_iota(jnp.int32, (SUBLANES, LANES), 1)
    forced = (lane_i == 0) | (lane_i == n_blocks - 1)
    score = jnp.where(lane_i < n_blocks, jnp.where(forced, 1e30, imp), -1.0)
    score_t = score.T
    ii = lax.broadcasted_iota(jnp.int32, (LANES, LANES), 0)
    jj = lax.broadcasted_iota(jnp.int32, (LANES, LANES), 1)
    sel_rows = []
    for g in range(N_KV):
        col = jnp.broadcast_to(score_t[:, g:g + 1], (LANES, LANES))
        rowv = jnp.broadcast_to(score[g:g + 1, :], (LANES, LANES))
        ahead = (col > rowv) | ((col == rowv) & (ii < jj))
        rank = jnp.sum(ahead.astype(F32), axis=0, keepdims=True)
        sel_rows.append((rank < float(top_n)).astype(F32))
    sel = jnp.concatenate(sel_rows + [jnp.zeros((SUBLANES - N_KV, LANES), F32)], axis=0)
    sel = jnp.where(lane_i < n_blocks, sel, 0.0)
    sel_h = _dot(rexp_ref[...], sel.astype(BF16))
    sel_keys = _dot(sel_h.astype(BF16), e_ref[...])

    for p in range(n_pages):
        s_scr[:, p * page:(p + 1) * page] = _dot(q, sel_pages[p][0].reshape(gd, page).astype(BF16))
    s_scr[:, n_pages * page:n_pages * page + LANES] = new_scores(2)
    s = jnp.where(sel_keys > 0.5, s_scr[...] + bsel_ref[...], NEG_INF)
    _, e_s, den_s = _softmax_lanes(s)
    e_b = e_s.astype(BF16)
    o_sel = new_value(e_s[:, n_pages * page:n_pages * page + LANES], 3)
    for p in range(n_pages):
        o_sel = o_sel + _dot_nt(e_b[:, p * page:(p + 1) * page], sel_pages[p][1].reshape(gd, page).astype(BF16))
    o_sel = o_sel * (1.0 / den_s)

    s_w = jnp.concatenate([_dot(q, win_ref[0].reshape(gd, win_len).astype(BF16)), new_scores(4)], axis=1)
    _, e_w, den_w = _softmax_lanes(s_w + bwin_ref[...])
    o_win = (_dot_nt(e_w[:, 0:win_len].astype(BF16), win_ref[1].reshape(gd, win_len).astype(BF16))
             + new_value(e_w[:, win_len:win_len + LANES], 5)) * (1.0 / den_w)

    g = jax.nn.sigmoid(gate_ref[...])
    o = g[:, 0:1] * o_cmp + g[:, 1:2] * o_sel + g[:, 2:3] * o_win
    own = (lax.broadcasted_iota(jnp.int32, (N_HEADS, gd), 1) // HEAD_DIM
           == lax.broadcasted_iota(jnp.int32, (N_HEADS, gd), 0) // Q_PER_KV)
    o = jnp.where(own, o, 0.0)
    acc = o[:, 0:HEAD_DIM]
    for gi in range(1, N_KV):
        acc = acc + o[:, gi * HEAD_DIM:(gi + 1) * HEAD_DIM]
    o_ref[...] = acc


def _sum3_rhs(x, rhs_bf16):
    hi, mid, lo = _split3(x)
    return _dot(hi, rhs_bf16) + _dot(mid, rhs_bf16) + _dot(lo, rhs_bf16)


def _nsa_decode(page_table, q_bd, gates_t, kvnew, kcn, cache_t, kcpool, win_t, l, tabs, *, name):
    B, n_pages = page_table.shape
    page = cache_t.shape[5]
    gd = N_KV * HEAD_DIM
    win_len = win_t.shape[5]
    bsel, bwin, bcmp, e_mat, pair, rsum, rexp, n_blocks = tabs
    blocks_per_page = kcpool.shape[1]
    const = lambda a: pl.BlockSpec(a.shape, lambda b, pt: (0,) * a.ndim)
    sel_specs = [pl.BlockSpec((None, None, 2, N_KV, HEAD_DIM, page), lambda b, pt, p=p: (l, pt[b, p], 1, 0, 0, 0))
                 for p in range(n_pages)]
    kc_specs = [pl.BlockSpec((None, blocks_per_page, 2 * gd), lambda b, pt, p=p: (pt[b, p], 0, 0))
                for p in range(n_pages)]
    grid_spec = pltpu.PrefetchScalarGridSpec(
        num_scalar_prefetch=1,
        grid=(B,),
        in_specs=[pl.BlockSpec((None, N_HEADS, gd), lambda b, pt: (b, 0, 0)),
                  pl.BlockSpec((None, N_HEADS, 3), lambda b, pt: (b, 0, 0)),
                  pl.BlockSpec((None, 1, kvnew.shape[2]), lambda b, pt: (b, 0, 0)),
                  pl.BlockSpec((None, SUBLANES, 2 * gd), lambda b, pt: (b, 0, 0))]
                 + sel_specs + kc_specs
                 + [pl.BlockSpec((None, None, 2, N_KV, HEAD_DIM, win_len), lambda b, pt: (l, b, 0, 0, 0, 0)),
                    const(bsel), const(bwin), const(bcmp), const(e_mat), const(pair), const(rsum), const(rexp)],
        out_specs=pl.BlockSpec((None, N_HEADS, HEAD_DIM), lambda b, pt: (b, 0, 0)),
        scratch_shapes=[pltpu.VMEM((LANES, 2 * gd), F32),
                        pltpu.VMEM((N_HEADS, n_pages * page + LANES), F32)])
    return pl.pallas_call(
        functools.partial(_nsa_decode_body, n_pages=n_pages, n_blocks=n_blocks,
                          top_n=min(TOP_N, n_blocks), win_len=win_len),
        grid_spec=grid_spec,
        out_shape=jax.ShapeDtypeStruct((B, N_HEADS, HEAD_DIM), F32),
        compiler_params=_cparams(("arbitrary",)),
        name=name,
    )(page_table, q_bd, gates_t, kvnew, kcn, *([cache_t] * n_pages), *([kcpool] * n_pages), win_t,
      bsel, bwin, bcmp, e_mat, pair, rsum, rexp)


def _decode_tables(rel_bias, past_len, n_pages, win_len):
    q_pos = past_len
    n_blocks = (past_len + 1 + L_SEL - 1) // L_SEL
    n_cmp = n_blocks * (L_SEL // L_CMP)
    by_dist = rel_bias[_t5_bucket(jnp.arange(q_pos + 1))].T
    keys = (n_pages + 1) * LANES
    kpos = jnp.arange(keys)
    bsel = jnp.where((kpos <= q_pos)[None], by_dist[:, jnp.clip(q_pos - kpos, 0)], NEG_INF)
    wl = jnp.arange(win_len + LANES)
    wdist = win_len - wl
    bwin = jnp.where((wdist >= 0)[None], by_dist[:, jnp.clip(wdist, 0, q_pos)], NEG_INF)
    c = jnp.arange(LANES)
    cdist = q_pos - (c * L_CMP + L_CMP - 1)
    bcmp = jnp.where(((cdist >= 0) & (c < n_cmp))[None], by_dist[:, jnp.clip(cdist, 0)], NEG_INF)
    e_mat = (jnp.arange(LANES)[:, None] == (kpos // L_SEL)[None, :]).astype(BF16)
    pair = ((c[:, None] // (L_SEL // L_CMP) == c[None, :]) & (c[:, None] < n_cmp)).astype(BF16)
    heads = jnp.arange(N_HEADS)
    rsum = (jnp.arange(SUBLANES)[:, None] == (heads // Q_PER_KV)[None, :]).astype(BF16)
    return bsel, bwin, bcmp, e_mat, pair, rsum, rsum.T, n_blocks


def _conv_step_body(x_ref, p0_ref, p1_ref, p2_ref, cw_ref, cb_ref, dt_ref, dtb_ref, alog_ref,
                    xact_ref, dto_ref, dec_ref):
    y = cb_ref[...]
    for k, ref in enumerate((p0_ref, p1_ref, p2_ref, x_ref)):
        y = y + cw_ref[k:k + 1, :] * ref[...]
    xact_ref[...] = _silu(y)
    x = dt_ref[:, 0:SSM_HEADS] + dtb_ref[...]
    dt = jnp.maximum(x, 0.0) + jnp.log1p(jnp.exp(-jnp.abs(x)))
    dto_ref[...] = dt
    dec_ref[...] = jnp.exp(dt * (-jnp.exp(alog_ref[...])))


def _conv_step(xbc, prev, conv_w, conv_b, dtg, dt_bias, a_log, *, name):
    B = xbc.shape[0]
    full = lambda a: pl.BlockSpec(a.shape, lambda i: (0,) * a.ndim)
    ops = [xbc, prev[:, 0], prev[:, 1], prev[:, 2], conv_w, conv_b.reshape(1, -1), dtg,
           dt_bias.reshape(1, -1), a_log.reshape(1, -1)]
    in_specs = [full(a) for a in ops]
    in_specs[6] = pl.BlockSpec((B, LANES), lambda i: (0, 1))
    return pl.pallas_call(
        _conv_step_body,
        grid=(1,),
        in_specs=in_specs,
        out_specs=[pl.BlockSpec((B, CONV_DIM), lambda i: (0, 0)), pl.BlockSpec((B, SSM_HEADS), lambda i: (0, 0)),
                   pl.BlockSpec((B, SSM_HEADS), lambda i: (0, 0))],
        out_shape=[jax.ShapeDtypeStruct((B, CONV_DIM), F32), jax.ShapeDtypeStruct((B, SSM_HEADS), F32),
                   jax.ShapeDtypeStruct((B, SSM_HEADS), F32)],
        compiler_params=_cparams(("arbitrary",)),
        name=name,
    )(*ops)


def _ssd_step_body(dts_ref, decs_ref, h0_ref, xs_ref, xst_ref, bm_ref, cm_ref, z_ref, dtc_ref, dskip_ref, ng_ref,
                   y_ref, st_ref, yoff_scr):
    b = pl.program_id(0)
    heads_per_group = SSM_HEADS // SSM_GROUPS
    bm = bm_ref[...]
    cm = cm_ref[...]
    rb = lambda v: v.astype(BF16).astype(F32)
    cb = jnp.sum(rb(cm) * rb(bm), axis=-1, keepdims=True)
    cm8 = jnp.concatenate([cm, jnp.zeros((SUBLANES - SSM_GROUPS, D_STATE), F32)], axis=0).astype(BF16)
    for h in range(SSM_HEADS):
        g = h // heads_per_group
        h_prev = h0_ref[h]
        dec = decs_ref[b, h]
        y_off = _dot_nt(cm8, h_prev.astype(BF16))
        yoff_scr[h:h + 1, :] = y_off[g:g + 1, :] * dec
        xdt = xst_ref[:, h:h + 1] * dts_ref[b, h]
        st_ref[h] = h_prev * dec + xdt * bm[g:g + 1, :]
    xs = xs_ref[...]
    cbh = jnp.concatenate([jnp.broadcast_to(cb[g:g + 1, :], (heads_per_group, 1)) for g in range(SSM_GROUPS)], axis=0)
    y = (cbh * dtc_ref[...]) * xs + yoff_scr[...] + dskip_ref[...] * xs
    y = y * _silu(z_ref[...])
    for g in range(SSM_GROUPS):
        yg = y[g * heads_per_group:(g + 1) * heads_per_group, :]
        ms = jnp.sum(jnp.sum(yg * yg, axis=1, keepdims=True), axis=0, keepdims=True) * (1.0 / (heads_per_group * SSM_HEADDIM))
        y_ref[g * heads_per_group:(g + 1) * heads_per_group, :] = (
            yg * lax.rsqrt(ms + EPS) * ng_ref[g * heads_per_group:(g + 1) * heads_per_group, :]).astype(y_ref.dtype)


def _ssd_step(state, l, xact, dt, dec, z, d_skip, norm_g, *, name):
    B = xact.shape[0]
    H, Pd, N = SSM_HEADS, SSM_HEADDIM, D_STATE
    xs = xact[:, 0:D_INNER].reshape(B, H, Pd)
    bm = xact[:, D_INNER:D_INNER + SSM_GROUPS * N].reshape(B, SSM_GROUPS, N)
    cm = xact[:, D_INNER + SSM_GROUPS * N:].reshape(B, SSM_GROUPS, N)
    per_b = lambda shape: pl.BlockSpec((None,) + shape, lambda b, *_: (b,) + (0,) * len(shape))
    const = lambda shape: pl.BlockSpec(shape, lambda b, *_: (0,) * len(shape))
    grid_spec = pltpu.PrefetchScalarGridSpec(
        num_scalar_prefetch=2,
        grid=(B,),
        in_specs=[pl.BlockSpec((None, None, H, Pd, N), lambda b, *_: (l, b, 0, 0, 0)),
                  per_b((H, Pd)), per_b((Pd, H)), per_b((SSM_GROUPS, N)), per_b((SSM_GROUPS, N)),
                  per_b((H, Pd)), per_b((H, 1)), const((H, 1)), const((H, Pd))],
        out_specs=[per_b((H, Pd)), per_b((H, Pd, N))],
        scratch_shapes=[pltpu.VMEM((H, Pd), F32)])
    return pl.pallas_call(
        _ssd_step_body,
        grid_spec=grid_spec,
        out_shape=[jax.ShapeDtypeStruct((B, H, Pd), BF16), jax.ShapeDtypeStruct((B, H, Pd, N), F32)],
        compiler_params=_cparams(("arbitrary",)),
        name=name,
    )(dt, dec, state, xs, xs.transpose(0, 2, 1), bm, cm, z.reshape(B, H, Pd), dt.reshape(B, H, 1),
      d_skip.reshape(H, 1), norm_g.reshape(H, Pd))


def _ffn_up_step_body(h_ref, wg_ref, wu_ref, p0_ref, p1_ref, cw_ref, cb_ref, act_ref, gate_ref):
    h = h_ref[...]
    gate = _dot(h, wg_ref[...])
    up = _dot(h, wu_ref[...])
    y = cb_ref[...] + cw_ref[0:1, :] * p0_ref[...] + cw_ref[1:2, :] * p1_ref[...] + cw_ref[2:3, :] * gate
    gate_ref[...] = gate
    act_ref[...] = (_silu(y) * up).astype(act_ref.dtype)


def _ffn_up_step(h, w_gu, prev, conv_w, conv_b, *, tn, name):
    B, K = h.shape
    nj = D_FF // tn
    col = lambda r: pl.BlockSpec((r, tn), lambda j: (0, j))
    return pl.pallas_call(
        _ffn_up_step_body,
        grid=(nj,),
        in_specs=[pl.BlockSpec((B, K), lambda j: (0, 0)), col(K),
                  pl.BlockSpec((K, tn), lambda j, nj=nj: (0, j + nj)),
                  col(B), col(B), col(FFN_CONV), col(1)],
        out_specs=[col(B), col(B)],
        out_shape=[jax.ShapeDtypeStruct((B, D_FF), BF16), jax.ShapeDtypeStruct((B, D_FF), F32)],
        compiler_params=_cparams(("parallel",)),
        name=name,
    )(h, w_gu, w_gu, prev[:, 0], prev[:, 1], conv_w, conv_b.reshape(1, D_FF))


def _pages_per_step(n_pool, limit=32):
    return max(p for p in range(2, limit + 1, 2) if n_pool % p == 0)


def _sample_layer(x, mod, l, Wl, P, C):
    B = x.shape[0]
    gd = N_KV * HEAD_DIM
    sh1, sc1, g1, sh2, sc2, g2 = jnp.split(mod, 6, axis=-1)
    h = _norm_mod(x, P['norm1_g'][l], sc1, sh1, tm=B, rows_per_batch=1, name=f"norm1_s{l}")
    q = _mm(h, Wl['q'], tm=B, tn=512, name=f"proj_q_s{l}", out_dtype=BF16, epilogue=lambda acc: acc * SCALE)
    kv = _mm(h, Wl['kv'], tm=B, tn=512, name=f"proj_kv_s{l}")
    small = _mm(h, Wl['small'], tm=B, tn=2 * LANES, name=f"proj_small_s{l}")
    z = _mm(h, Wl['z'], tm=B, tn=512, name=f"proj_z_s{l}")
    xbc = _mm(h, Wl['xbc'], tm=B, tn=512, name=f"proj_xbc_s{l}")
    gates = _mm(h, Wl['mg'], tm=B, tn=512, name=f"proj_mg_s{l}", epilogue=jax.nn.sigmoid)

    cache_kv, page_table = C['cache_kv'], C['page_table']
    n_pool, page = cache_kv.shape[1], cache_kv.shape[2]
    cache_t = cache_kv.transpose(0, 1, 3, 4, 5, 2)
    win_t = C['cache_win_kv'].transpose(0, 1, 3, 4, 5, 2)
    kcpool = _pool_compress(cache_t, l, Wl['pe_pair'], Wl['w1_pair'], Wl['cmp_w2'],
                            pages_per_step=_pages_per_step(n_pool, 16), name=f"cmp_pool_s{l}")
    kcpool = kcpool.reshape(n_pool, page // L_CMP, 2 * gd)

    def new_blocks(slot):
        first = jnp.pad(kv[:, slot * gd:(slot + 1) * gd].reshape(B, N_KV, 1, HEAD_DIM),
                        ((0, 0), (0, 0), (0, 1), (0, (L_CMP - 1) * HEAD_DIM)))
        rows = B * N_KV * 2
        out = _compress(first.reshape(rows, L_CMP * HEAD_DIM), P['cmp_pe'][l, slot], Wl['cmp_w1'][slot],
                        Wl['cmp_w2'][slot], tm=512 if rows % 512 == 0 else rows, name=f"cmp{slot}_new_s{l}")
        return out.reshape(B, N_KV, 2, HEAD_DIM).transpose(0, 2, 1, 3).reshape(B, 2, gd)

    kcn = jnp.pad(jnp.concatenate([new_blocks(0), new_blocks(1)], axis=-1), ((0, 0), (0, SUBLANES - 2), (0, 0)))
    qh = q.reshape(B, N_KV, Q_PER_KV, 1, HEAD_DIM)
    q_bd = (qh * jnp.eye(N_KV, dtype=BF16)[None, :, None, :, None]).reshape(B, N_HEADS, gd)
    gates_t = small[:, 0:NSA_GATE_COLS].reshape(B, 3, N_HEADS).transpose(0, 2, 1)
    win_buf = C['cache_win_kv']
    o_nsa = _nsa_decode(page_table, q_bd, gates_t, kv.reshape(B, 1, KV_COLS), kcn, cache_t, kcpool, win_t, l,
                        C['tables'], name=f"nsa_s{l}")
    o_nsa = o_nsa.reshape(B, Q_COLS)
    new_rows = kv[:, 0:4 * gd].reshape(B, 1, 4, N_KV, HEAD_DIM)
    win_state = jnp.concatenate([win_buf[l][:, 1:], kv[:, 4 * gd:].reshape(B, 1, 2, N_KV, HEAD_DIM)], axis=1)

    conv_prev = C['state_ssm_conv'][l]
    xact, dt, dec = _conv_step(xbc, conv_prev, P['ssm_conv_w'][l], P['ssm_conv_b'][l], small,
                               P['dt_bias'][l], P['a_log'][l], name=f"conv_s{l}")
    conv_state = jnp.concatenate([conv_prev[:, 1:], xbc[:, None]], axis=1)
    y, ssm_state = _ssd_step(C['state_ssm'], l, xact, dt, dec, z, P['d_skip'][l], P['ssm_norm_g'][l],
                             name=f"ssd_s{l}")

    u = _merge(o_nsa.astype(BF16), y.reshape(B, D_INNER), Wl['a'], Wl['b'], gates, tm=B, tn=512, name=f"merge_s{l}")
    x = _mm(u, Wl['o'], tm=B, tn=512, name=f"proj_o_s{l}",
            epilogue=lambda acc, xr, gr: xr + gr * acc, extras=[('full', x), ('full', g1)])

    h2 = _norm_mod(x, P['norm2_g'][l], sc2, sh2, tm=B, rows_per_batch=1, name=f"norm2_s{l}")
    ffn_prev = C['state_ffn_conv'][l]
    act, gate_raw = _ffn_up_step(h2, Wl['gu'], ffn_prev, P['ffn_conv_w'][l], P['ffn_conv_b'][l],
                                 tn=D_FF // 2, name=f"ffn_up_s{l}")
    ffn_state = jnp.concatenate([ffn_prev[:, 1:], gate_raw[:, None]], axis=1)
    x = _mm(act, Wl['down'], tm=B, tn=512, name=f"ffn_down_s{l}",
            epilogue=lambda acc, xr, gr: xr + gr * acc, extras=[('full', x), ('full', g2)])
    return x, (new_rows, win_state, conv_state, ssm_state, ffn_state)


def kernel(x_prompt, x_sample, cache_kv, cache_win_kv, state_ssm_conv, state_ssm, state_ffn_conv, page_table, c_prompt, c_sample, w_ada, b_ada, norm1_g, norm2_g, final_g, w_in, rel_bias, cmp_pe, cmp_w1, cmp_w2, ssm_conv_w, ssm_conv_b, dt_bias, a_log, d_skip, ssm_norm_g, w_a, w_b, w_o, w_gu, ffn_conv_w, ffn_conv_b, w_down):
    P = dict(norm1_g=norm1_g, norm2_g=norm2_g, rel_bias=rel_bias, cmp_pe=cmp_pe.reshape(DEPTH, 2, -1),
             ssm_conv_w=ssm_conv_w, ssm_conv_b=ssm_conv_b, dt_bias=dt_bias, a_log=a_log, d_skip=d_skip,
             ssm_norm_g=ssm_norm_g, ffn_conv_w=ffn_conv_w, ffn_conv_b=ffn_conv_b)
    Bp, T, _ = x_prompt.shape
    Bs = x_sample.shape[0]
    past_len = page_table.shape[1] * cache_kv.shape[2]
    tables = _bias_tables(rel_bias * LOG2E)
    n_c = Bp + Bs
    c_rows = -(-n_c // SUBLANES) * SUBLANES
    c_all = jnp.zeros((c_rows, D_MODEL), F32).at[:Bp].set(c_prompt).at[Bp:n_c].set(c_sample).astype(BF16)
    xp = x_prompt.reshape(Bp * T, D_MODEL)
    xs = x_sample.reshape(Bs, D_MODEL)
    C = dict(cache_kv=cache_kv, page_table=page_table, cache_win_kv=cache_win_kv, state_ssm_conv=state_ssm_conv,
             state_ssm=state_ssm, state_ffn_conv=state_ffn_conv,
             tables=_decode_tables(rel_bias, past_len, page_table.shape[1], cache_win_kv.shape[2]))
    st_p, st_s = [], []
    for l in range(DEPTH):
        Wl = _layer_weights(l, w_ada, w_in, cmp_pe, cmp_w1, cmp_w2, w_a, w_b, w_o, w_gu, w_down)
        mod = _mm(c_all, Wl['ada'], tm=c_rows, tn=512, name=f"ada{l}",
                  epilogue=lambda acc, b: acc + b, extras=[('col', b_ada[l].reshape(1, -1))])
        xp, sp = _prompt_layer(xp, mod[:Bp], l, Wl, P, tables, Bp, T)
        st_p.append(sp)
        xs, ss = _sample_layer(xs, mod[Bp:n_c], l, Wl, P, C)
        st_s.append(ss)
    y_prompt = _final_norm(xp, final_g, tm=512, name="final_norm_p").reshape(Bp, T, D_MODEL)
    y_sample = _final_norm(xs, final_g, tm=Bs, name="final_norm_s").reshape(Bs, 1, D_MODEL)
    kv_p, win_p, conv_p, ssm_p, ffn_p = [jnp.stack([s[i] for s in st_p]) for i in range(5)]
    kv_s, win_s, conv_s, ssm_s, ffn_s = [jnp.stack([s[i] for s in st_s]) for i in range(5)]
    return (y_prompt, y_sample, kv_p, win_p, conv_p, ssm_p, ffn_p, kv_s, win_s, conv_s, ssm_s, ffn_s)
```

```python
import functools
import math

import jax
import jax.numpy as jnp
from jax import lax
from jax.experimental import pallas as pl
from jax.experimental.pallas import tpu as pltpu

D_MODEL = 1024
N_HEADS = 16
N_KV = 4
HEAD_DIM = 64
Q_PER_KV = N_HEADS // N_KV
L_CMP = 32
L_SEL = 64
TOP_N = 16
WINDOW = 512
CMP_HID = 128
Q_BLOCK = 128
SCALE = HEAD_DIM ** -0.5
N_BUCKETS = 32
MAX_DISTANCE = 128
D_INNER = 2 * D_MODEL
SSM_HEADDIM = 64
SSM_HEADS = D_INNER // SSM_HEADDIM
SSM_GROUPS = 4
D_STATE = 128
SSM_CONV = 4
SSM_CHUNK = 128
CONV_DIM = D_INNER + 2 * SSM_GROUPS * D_STATE
D_FF = 2816
FFN_CONV = 3
EPS = 1e-6
NEG_INF = -1e30
DEPTH = 2

Q_COLS = N_HEADS * HEAD_DIM
KV_COLS = 6 * N_KV * HEAD_DIM
NSA_GATE_COLS = 3 * N_HEADS
OFF_KV = Q_COLS
OFF_G = OFF_KV + KV_COLS
OFF_Z = OFF_G + NSA_GATE_COLS
OFF_XBC = OFF_Z + D_INNER
OFF_DT = OFF_XBC + CONV_DIM
OFF_MG = OFF_DT + SSM_HEADS
P_IN = OFF_MG + 2 * D_MODEL

LOG2E = math.log2(math.e)

LANES = 128
SUBLANES = 8
BF16_ROWS = 16
TQ = 128
WIDTH = Q_PER_KV * TQ
FAR_TILES = 4
WIN_TILES = WINDOW // TQ + 1
AUG_ROW0 = HEAD_DIM
VMEM_LIMIT = 48 * 1024 * 1024

BF16 = jnp.bfloat16
F32 = jnp.float32


def _cparams(sem):
    return pltpu.CompilerParams(dimension_semantics=sem, vmem_limit_bytes=VMEM_LIMIT)


def _silu(x):
    return x * jax.nn.sigmoid(x)


def _dot(a, b):
    return jnp.dot(a, b, preferred_element_type=F32)


def _dot_nt(a, b):
    return lax.dot_general(a, b, (((1,), (1,)), ((), ())), preferred_element_type=F32)


def _dot_tn(a, b):
    return lax.dot_general(a, b, (((0,), (0,)), ((), ())), preferred_element_type=F32)


def _split3(x):
    hi = x.astype(BF16)
    r1 = x - hi.astype(F32)
    mid = r1.astype(BF16)
    lo = (r1 - mid.astype(F32)).astype(BF16)
    return hi, mid, lo


def _mm_body(a_ref, w_ref, *refs, epilogue, n_extra):
    acc = _dot(a_ref[...], w_ref[...])
    extras = [r[...] for r in refs[:n_extra]]
    o_ref = refs[n_extra]
    o_ref[...] = epilogue(acc, *extras).astype(o_ref.dtype)


def _mm(a, w, *, tm, tn, name, out_dtype=F32, epilogue=None, extras=(), rows_per_batch=None):
    M, K = a.shape
    N = w.shape[1]
    assert M % tm == 0 and N % tn == 0, (M, N, tm, tn)
    in_specs = [pl.BlockSpec((tm, K), lambda i, j: (i, 0)),
                pl.BlockSpec((K, tn), lambda i, j: (0, j))]
    ops = [a, w]
    for kind, arr in extras:
        if kind == 'col':
            in_specs.append(pl.BlockSpec((1, tn), lambda i, j: (0, j)))
        elif kind == 'full':
            in_specs.append(pl.BlockSpec((tm, tn), lambda i, j: (i, j)))
        else:
            assert kind == 'batch' and rows_per_batch % tm == 0
            tpb = rows_per_batch // tm
            in_specs.append(pl.BlockSpec((None, 1, tn), lambda i, j, tpb=tpb: (i // tpb, 0, j)))
        ops.append(arr)
    if epilogue is None:
        epilogue = lambda acc: acc
    return pl.pallas_call(
        functools.partial(_mm_body, epilogue=epilogue, n_extra=len(extras)),
        grid=(M // tm, N // tn),
        in_specs=in_specs,
        out_specs=pl.BlockSpec((tm, tn), lambda i, j: (i, j)),
        out_shape=jax.ShapeDtypeStruct((M, N), out_dtype),
        compiler_params=_cparams(("parallel", "parallel")),
        name=name,
    )(*ops)


def _norm_mod_body(x_ref, g_ref, sc_ref, sh_ref, o_ref):
    x = x_ref[...]
    y = x * lax.rsqrt(jnp.mean(x * x, axis=-1, keepdims=True) + EPS) * g_ref[...]
    o_ref[...] = (y * (1.0 + sc_ref[...]) + sh_ref[...]).astype(o_ref.dtype)


def _norm_mod(x, g, sc, sh, *, tm, rows_per_batch, name):
    M, D = x.shape
    if sc.ndim == 3:
        tpb = rows_per_batch // tm
        mod_spec = pl.BlockSpec((None, 1, D), lambda i: (i // tpb, 0, 0))
    else:
        mod_spec = pl.BlockSpec((tm, D), lambda i: (i, 0))
    return pl.pallas_call(
        _norm_mod_body,
        grid=(M // tm,),
        in_specs=[pl.BlockSpec((tm, D), lambda i: (i, 0)), pl.BlockSpec((1, D), lambda i: (0, 0)),
                  mod_spec, mod_spec],
        out_specs=pl.BlockSpec((tm, D), lambda i: (i, 0)),
        out_shape=jax.ShapeDtypeStruct((M, D), BF16),
        compiler_params=_cparams(("parallel",)),
        name=name,
    )(x, g.reshape(1, D), sc, sh)


def _final_norm_body(x_ref, g_ref, o_ref):
    x = x_ref[...]
    o_ref[...] = x * lax.rsqrt(jnp.mean(x * x, axis=-1, keepdims=True) + EPS) * g_ref[...]


def _final_norm(x, g, *, tm, name):
    M, D = x.shape
    return pl.pallas_call(
        _final_norm_body,
        grid=(M // tm,),
        in_specs=[pl.BlockSpec((tm, D), lambda i: (i, 0)), pl.BlockSpec((1, D), lambda i: (0, 0))],
        out_specs=pl.BlockSpec((tm, D), lambda i: (i, 0)),
        out_shape=jax.ShapeDtypeStruct((M, D), F32),
        compiler_params=_cparams(("parallel",)),
        name=name,
    )(x, g.reshape(1, D))


def _merge_body(o_ref, y_ref, wa_ref, wb_ref, ga_ref, gb_ref, u_ref):
    pa = _dot(o_ref[...], wa_ref[...])
    pb = _dot(y_ref[...], wb_ref[...])
    u_ref[...] = (ga_ref[...] * pa + gb_ref[...] * pb).astype(u_ref.dtype)


def _merge(o_nsa, y, w_a, w_b, gates, *, tm, tn, name):
    M = o_nsa.shape[0]
    N = w_a.shape[1]
    nj = N // tn
    return pl.pallas_call(
        _merge_body,
        grid=(M // tm, nj),
        in_specs=[pl.BlockSpec((tm, o_nsa.shape[1]), lambda i, j: (i, 0)),
                  pl.BlockSpec((tm, y.shape[1]), lambda i, j: (i, 0)),
                  pl.BlockSpec((w_a.shape[0], tn), lambda i, j: (0, j)),
                  pl.BlockSpec((w_b.shape[0], tn), lambda i, j: (0, j)),
                  pl.BlockSpec((tm, tn), lambda i, j: (i, j)),
                  pl.BlockSpec((tm, tn), lambda i, j, nj=nj: (i, j + nj))],
        out_specs=pl.BlockSpec((tm, tn), lambda i, j: (i, j)),
        out_shape=jax.ShapeDtypeStruct((M, N), BF16),
        compiler_params=_cparams(("parallel", "parallel")),
        name=name,
    )(o_nsa, y, w_a, w_b, gates, gates)


def _ffn_up_body(h_ref, wg_ref, wu_ref, cw_ref, cb_ref, o_ref, pad_ref, *, tm, tiles_per_batch):
    i = pl.program_id(1)
    h = h_ref[...]
    gate = _dot(h, wg_ref[...])
    up = _dot(h, wu_ref[...])

    @pl.when(i % tiles_per_batch == 0)
    def _():
        pad_ref[0:SUBLANES, :] = jnp.zeros((SUBLANES, pad_ref.shape[1]), F32)

    pad_ref[SUBLANES:SUBLANES + tm, :] = gate
    y = cb_ref[...]
    for k in range(FFN_CONV):
        off = SUBLANES - (FFN_CONV - 1) + k
        y = y + cw_ref[k:k + 1, :] * pad_ref[off:off + tm, :]
    pad_ref[0:SUBLANES, :] = pad_ref[tm:tm + SUBLANES, :]
    o_ref[...] = (_silu(y) * up).astype(o_ref.dtype)


def _ffn_up(h, w_gu, conv_w, conv_b, *, tm, tn, rows_per_batch, name):
    M, K = h.shape
    nj = D_FF // tn
    return pl.pallas_call(
        functools.partial(_ffn_up_body, tm=tm, tiles_per_batch=rows_per_batch // tm),
        grid=(nj, M // tm),
        in_specs=[pl.BlockSpec((tm, K), lambda j, i: (i, 0)),
                  pl.BlockSpec((K, tn), lambda j, i: (0, j)),
                  pl.BlockSpec((K, tn), lambda j, i, nj=nj: (0, j + nj)),
                  pl.BlockSpec((FFN_CONV, tn), lambda j, i: (0, j)),
                  pl.BlockSpec((1, tn), lambda j, i: (0, j))],
        out_specs=pl.BlockSpec((tm, tn), lambda j, i: (i, j)),
        out_shape=jax.ShapeDtypeStruct((M, D_FF), BF16),
        scratch_shapes=[pltpu.VMEM((tm + SUBLANES, tn), F32)],
        compiler_params=_cparams(("arbitrary", "arbitrary")),
        name=name,
    )(h, w_gu, w_gu, conv_w, conv_b.reshape(1, D_FF))


def _compress_body(x_ref, pe_ref, w1_ref, w2_ref, o_ref):
    xb = (x_ref[...] + pe_ref[...]).astype(BF16)
    hid = _silu(_dot(xb, w1_ref[...]))
    o_ref[...] = _dot(hid.astype(BF16), w2_ref[...])


def _compress(x, pe, w1, w2, *, tm, name):
    rows, kdim = x.shape
    return pl.pallas_call(
        _compress_body,
        grid=(rows // tm,),
        in_specs=[pl.BlockSpec((tm, kdim), lambda i: (i, 0)), pl.BlockSpec((1, kdim), lambda i: (0, 0)),
                  pl.BlockSpec((kdim, CMP_HID), lambda i: (0, 0)),
                  pl.BlockSpec((CMP_HID, HEAD_DIM), lambda i: (0, 0))],
        out_specs=pl.BlockSpec((tm, HEAD_DIM), lambda i: (i, 0)),
        out_shape=jax.ShapeDtypeStruct((rows, HEAD_DIM), F32),
        compiler_params=_cparams(("parallel",)),
        name=name,
    )(x, pe.reshape(1, kdim), w1, w2)


def _attend(s, vt):
    m = jnp.max(s, axis=0, keepdims=True)
    pv = _dot(vt, jnp.exp2(s - m).astype(BF16))
    return m, pv[HEAD_DIM:HEAD_DIM + 1, :], pv[0:HEAD_DIM, :]


def _lane_concat(tiles3):
    return jnp.concatenate([tiles3[t] for t in range(tiles3.shape[0])], axis=1)


def _nsa_prompt_body(q_ref, ksel_ref, vsel_ref, kwin_ref, vwin_ref, kc_ref, vct_ref, gate_ref,
                     tzw_ref, tzn_ref, band_ref, caug_ref, o_ref,
                     qaug_ref, m_ref, l_ref, acc_ref, sel0_ref, selfar_ref, sa_scr, sb_scr, *, n_blocks, n_cmp, top_n):
    qi = pl.program_id(2)
    q0 = qi * TQ
    n_grp_rows = n_blocks // SUBLANES

    @pl.when(qi == 0)
    def _():
        selfar_ref[:, SUBLANES:2 * SUBLANES, :] = jnp.broadcast_to(
            caug_ref[...][None], (selfar_ref.shape[0], SUBLANES, WIDTH))
        selfar_ref[n_grp_rows:, 0:SUBLANES, :] = jnp.full(
            (selfar_ref.shape[0] - n_grp_rows, SUBLANES, WIDTH), NEG_INF, F32)
        sel0_ref[0:SUBLANES, :] = jnp.zeros((SUBLANES, WIDTH), F32)

    qaug_ref[0:HEAD_DIM, :] = q_ref[...]
    qaug_ref[HEAD_DIM:, :] = jnp.zeros((qaug_ref.shape[0] - HEAD_DIM, WIDTH), BF16)
    q = q_ref[...]

    kw = kwin_ref[pl.ds(qi, WIN_TILES)].reshape(WIN_TILES * TQ, LANES)
    m_w, l_w, a_w = _attend(_dot(kw, qaug_ref[...]) + tzw_ref[...], _lane_concat(vwin_ref[pl.ds(qi, WIN_TILES)]))
    o_win = a_w * (1.0 / l_w)

    half = n_cmp // 2
    row = lax.broadcasted_iota(jnp.int32, (n_cmp, 16), 0)
    cidx = jnp.where(row < half, 2 * row, 2 * (row - half) + 1)
    crel = cidx - (TQ // L_CMP) * qi
    cat = jnp.where(crel <= -5, 8, jnp.where(crel >= 4, 9, crel + 4))
    place = (cat == lax.broadcasted_iota(jnp.int32, (n_cmp, 16), 1)).astype(BF16)
    b_hi, b_mid, b_lo = _split3(band_ref[...])
    bias_c = _dot(place, b_hi) + _dot(place, b_mid) + _dot(place, b_lo)
    s = _dot(kc_ref[...], q) + bias_c
    m = jnp.max(s, axis=0, keepdims=True)
    e = jnp.exp2(s - m)
    denom = jnp.sum(e, axis=0, keepdims=True)
    anyvalid = (m > 0.5 * NEG_INF).astype(F32)
    p = e * (anyvalid / denom)
    o_cmp = _dot(vct_ref[...], p.astype(BF16))
    psum = p[:, 0:TQ]
    for r in range(1, Q_PER_KV):
        psum = psum + p[:, r * TQ:(r + 1) * TQ]
    imp = psum[0:half, :] + psum[half:n_cmp, :]

    blk = lax.broadcasted_iota(jnp.int32, (n_blocks, TQ), 0)
    blk_f = blk.astype(F32)
    qpos = q0 + lax.broadcasted_iota(jnp.int32, (n_blocks, TQ), 1)
    forced = (blk == 0) | (blk == lax.shift_right_logical(qpos, 6))
    valid = blk * L_SEL <= qpos
    score = jnp.where(forced, 1e30, jnp.where(valid, imp, -1.0))
    picked = jnp.zeros((n_blocks, TQ), jnp.bool_)
    for _ in range(top_n):
        mx = jnp.max(score, axis=0, keepdims=True)
        first = jnp.min(jnp.where(score == mx, blk_f, float(n_blocks)), axis=0, keepdims=True)
        hit = blk_f == first
        picked = picked | hit
        score = jnp.where(hit, -2.0, score)
    seladd = jnp.where(picked, 0.0, NEG_INF)
    selfar = jnp.where(blk < 2 * (qi - 1), seladd, NEG_INF)
    sel0_ref[SUBLANES:, :] = jnp.concatenate([seladd] * Q_PER_KV, axis=1)
    selfar_ref[0:n_grp_rows, 0:SUBLANES, :] = jnp.concatenate([selfar] * Q_PER_KV, axis=1).reshape(
        n_grp_rows, SUBLANES, WIDTH)

    near_mask = jnp.concatenate(
        [jnp.broadcast_to(sel0_ref[pl.ds(SUBLANES + 2 * (qi - 1) + i, 1), :], (L_SEL, WIDTH)) for i in range(4)],
        axis=0)
    kn = ksel_ref[pl.ds(qi, 2)].reshape(2 * TQ, LANES)
    m_n, l_n, a_n = _attend(_dot(kn, qaug_ref[...]) + tzn_ref[...] + near_mask, _lane_concat(vsel_ref[pl.ds(qi, 2)]))
    m_ref[...] = m_n
    l_ref[...] = l_n
    acc_ref[...] = a_n

    def far_scores(gi):
        qaug_ref[AUG_ROW0:AUG_ROW0 + BF16_ROWS, :] = selfar_ref[gi].astype(BF16)
        kg = ksel_ref[pl.ds(1 + FAR_TILES * gi, FAR_TILES)].reshape(FAR_TILES * TQ, LANES)
        return _dot(kg, qaug_ref[...])

    def far_update(s, gi):
        m_g, l_g, a_g = _attend(s, _lane_concat(vsel_ref[pl.ds(1 + FAR_TILES * gi, FAR_TILES)]))
        m_old = m_ref[...]
        m_new = jnp.maximum(m_old, m_g)
        a_old = jnp.exp2(m_old - m_new)
        a_grp = jnp.exp2(m_g - m_new)
        l_ref[...] = l_ref[...] * a_old + l_g * a_grp
        acc_ref[...] = acc_ref[...] * a_old + a_g * a_grp
        m_ref[...] = m_new

    sa_scr[...] = far_scores(0)

    def far_body(j, carry):
        sb_scr[...] = far_scores(2 * j + 1)
        far_update(sa_scr[...], 2 * j)
        sa_scr[...] = far_scores(2 * j + 2)
        far_update(sb_scr[...], 2 * j + 1)
        return carry

    n_far = jnp.maximum(qi - 1, 0)
    n_groups = lax.shift_right_logical(n_far + (FAR_TILES - 1), 2)
    lax.fori_loop(0, lax.shift_right_logical(n_groups + 1, 1), far_body, 0)
    o_sel = acc_ref[...] * (1.0 / l_ref[...])

    g = jax.nn.sigmoid(gate_ref[...])
    o = g[0:1, :] * o_cmp + g[1:2, :] * o_sel + g[2:3, :] * o_win
    for r in range(Q_PER_KV):
        o_ref[r * HEAD_DIM:(r + 1) * HEAD_DIM, :] = o[:, r * TQ:(r + 1) * TQ].astype(o_ref.dtype)


def _nsa_prompt(q2, ksel, vsel, kwin, vwin, kc, vct, gates, tables, *, name):
    tzw, tzn, band, caug = tables
    B, G, nq, D, _ = q2.shape
    n_cmp = kc.shape[2]
    n_blocks = n_cmp // (L_SEL // L_CMP)
    assert n_blocks % SUBLANES == 0 and nq % FAR_TILES == 0
    T = nq * TQ
    whole = lambda arr: pl.BlockSpec((None, None) + arr.shape[2:], lambda b, g, i: (b, g, 0, 0, 0))
    return pl.pallas_call(
        functools.partial(_nsa_prompt_body, n_blocks=n_blocks, n_cmp=n_cmp, top_n=min(TOP_N, n_blocks)),
        grid=(B, G, nq),
        in_specs=[pl.BlockSpec((None, None, None, D, WIDTH), lambda b, g, i: (b, g, i, 0, 0)),
                  whole(ksel), whole(vsel), whole(kwin), whole(vwin),
                  pl.BlockSpec((None, None, n_cmp, D), lambda b, g, i: (b, g, 0, 0)),
                  pl.BlockSpec((None, None, D, n_cmp), lambda b, g, i: (b, g, 0, 0)),
                  pl.BlockSpec((None, None, None, 3, WIDTH), lambda b, g, i: (b, g, i, 0, 0)),
                  pl.BlockSpec((None, None, WIN_TILES * TQ, WIDTH),
                               lambda b, g, i: (g, jnp.minimum(i, WIN_TILES - 1), 0, 0)),
                  pl.BlockSpec((None, None, 2 * TQ, WIDTH), lambda b, g, i: (g, jnp.minimum(i, 1), 0, 0)),
                  pl.BlockSpec((None, 16, WIDTH), lambda b, g, i: (g, 0, 0)),
                  pl.BlockSpec((None, SUBLANES, WIDTH), lambda b, g, i: (g, 0, 0))],
        out_specs=pl.BlockSpec((None, Q_PER_KV * D, TQ), lambda b, g, i: (b, g, i)),
        out_shape=jax.ShapeDtypeStruct((B, G * Q_PER_KV * D, T), BF16),
        scratch_shapes=[pltpu.VMEM((LANES, WIDTH), BF16),
                        pltpu.VMEM((1, WIDTH), F32), pltpu.VMEM((1, WIDTH), F32),
                        pltpu.VMEM((D, WIDTH), F32),
                        pltpu.VMEM((n_blocks + SUBLANES, WIDTH), F32),
                        pltpu.VMEM((nq // FAR_TILES + 2, 2 * SUBLANES, WIDTH), F32),
                        pltpu.VMEM((FAR_TILES * TQ, WIDTH), F32), pltpu.VMEM((FAR_TILES * TQ, WIDTH), F32)],
        compiler_params=_cparams(("arbitrary", "arbitrary", "arbitrary")),
        name=name,
    )(q2, ksel, vsel, kwin, vwin, kc, vct, gates, tzw, tzn, band, caug)


SSD_QUAD = 4

def _ssd_body(xbc_ref, dt_ref, z_ref, cw_ref, cb_ref, dtb_ref, alog_ref, dskip_ref, ng_ref,
              exp_h_ref, exp_p_ref, y_ref, st_ref, pad_ref, x_scr, colh_scr, colp_scr, dtp_scr, y_scr):
    c = pl.program_id(1)
    L = SSM_CHUNK
    P = SSM_HEADDIM
    heads_per_group = SSM_HEADS // SSM_GROUPS

    @pl.when(c == 0)
    def _():
        pad_ref[0:SUBLANES, :] = jnp.zeros((SUBLANES, CONV_DIM), F32)
        st_ref[...] = jnp.zeros(st_ref.shape, F32)

    pad_ref[SUBLANES:SUBLANES + L, :] = xbc_ref[...]
    conv = cb_ref[...]
    for k in range(SSM_CONV):
        off = SUBLANES - (SSM_CONV - 1) + k
        conv = conv + cw_ref[k:k + 1, :] * pad_ref[off:off + L, :]
    pad_ref[0:SUBLANES, :] = pad_ref[L:L + SUBLANES, :]
    xbc = _silu(conv)
    x_scr[...] = xbc[:, 0:D_INNER]
    bmat = [xbc[:, D_INNER + g * D_STATE:D_INNER + (g + 1) * D_STATE].astype(BF16) for g in range(SSM_GROUPS)]
    cmat = [xbc[:, D_INNER + (SSM_GROUPS + g) * D_STATE:D_INNER + (SSM_GROUPS + g + 1) * D_STATE].astype(BF16)
            for g in range(SSM_GROUPS)]

    x = dt_ref[:, 0:SSM_HEADS] + dtb_ref[...]
    dt = jnp.maximum(x, 0.0) + jnp.log1p(jnp.exp(-jnp.abs(x)))
    a = dt * (-jnp.exp(alog_ref[...]))
    tri = lax.broadcasted_iota(jnp.int32, (L, L), 0) >= lax.broadcasted_iota(jnp.int32, (L, L), 1)
    tri_b = tri.astype(BF16)
    a_hi, a_mid, a_lo = _split3(a)
    a_cs = _dot(tri_b, a_hi) + _dot(tri_b, a_mid) + _dot(tri_b, a_lo)
    acs_parts = jnp.concatenate(_split3(a_cs), axis=1)
    dt_parts = jnp.concatenate(_split3(dt), axis=1)
    colh_scr[...] = _dot(acs_parts, exp_h_ref[...])
    colp_scr[...] = _dot(acs_parts, exp_p_ref[...])
    dtp_scr[...] = _dot(dt_parts, exp_p_ref[...])
    acs_t = a_cs.T
    dt_t = dt.T
    tri_q = jnp.concatenate([tri] * SSD_QUAD, axis=1)

    for qd in range(SSM_HEADS // SSD_QUAD):
        g = (qd * SSD_QUAD) // heads_per_group
        h0 = qd * SSD_QUAD
        b_g, c_g = bmat[g], cmat[g]
        cb = _dot_nt(c_g, b_g)
        col = colh_scr[:, h0 * LANES:(h0 + SSD_QUAD) * LANES]
        row = jnp.concatenate([acs_t[h0 + i:h0 + i + 1, :] for i in range(SSD_QUAD)], axis=1)
        dtrow = jnp.concatenate([dt_t[h0 + i:h0 + i + 1, :] for i in range(SSD_QUAD)], axis=1)
        decay = jnp.exp(jnp.where(tri_q, col - row, NEG_INF))
        w = (jnp.concatenate([cb] * SSD_QUAD, axis=1) * decay * dtrow).astype(BF16)
        xq = x_scr[:, h0 * P:(h0 + SSD_QUAD) * P]
        xq_b = xq.astype(BF16)
        y_diag = jnp.concatenate([_dot(w[:, i * LANES:(i + 1) * LANES], xq_b[:, i * P:(i + 1) * P])
                                  for i in range(SSD_QUAD)], axis=1)
        h_prev = st_ref[h0:h0 + SSD_QUAD].reshape(SSD_QUAD * P, D_STATE)
        colp = colp_scr[:, h0 * P:(h0 + SSD_QUAD) * P]
        y_off = _dot_nt(c_g, h_prev.astype(BF16)) * jnp.exp(colp)
        y_scr[:, h0 * P:(h0 + SSD_QUAD) * P] = y_diag + y_off
        decay_end = jnp.exp(colp[L - 1:L, :] - colp)
        xw = (decay_end * dtp_scr[:, h0 * P:(h0 + SSD_QUAD) * P]) * xq
        st = _dot_tn(xw.astype(BF16), b_g)
        chunk_decay = jnp.exp(col[L - 1:L, :])
        cd = jnp.concatenate([jnp.broadcast_to(chunk_decay[:, i * LANES:(i + 1) * LANES], (P, D_STATE))
                              for i in range(SSD_QUAD)], axis=0)
        st_ref[h0:h0 + SSD_QUAD] = (h_prev * cd + st).reshape(SSD_QUAD, P, D_STATE)

    group_w = D_INNER // SSM_GROUPS
    y = y_scr[...] + dskip_ref[...] * x_scr[...]
    y = y * _silu(z_ref[...])
    for g in range(SSM_GROUPS):
        yg = y[:, g * group_w:(g + 1) * group_w]
        inv = lax.rsqrt(jnp.mean(yg * yg, axis=-1, keepdims=True) + EPS)
        y_ref[:, g * group_w:(g + 1) * group_w] = (
            yg * inv * ng_ref[:, g * group_w:(g + 1) * group_w]).astype(y_ref.dtype)


def _ssd_prompt(xbc, dtg, z, conv_w, conv_b, dt_bias, a_log, d_skip, norm_g, *, batch, name):
    M = xbc.shape[0]
    L = SSM_CHUNK
    nc = M // batch // L
    heads = jnp.arange(SSM_HEADS)

    def expand(lanes):
        e = (heads[:, None, None] == heads[None, :, None])
        e = jnp.broadcast_to(e, (SSM_HEADS, SSM_HEADS, lanes)).reshape(SSM_HEADS, SSM_HEADS * lanes)
        return jnp.concatenate([e] * 3, axis=0).astype(BF16)

    row = lambda v: v.reshape(1, -1)
    const2 = lambda shape: pl.BlockSpec(shape, lambda b, c: (0, 0))
    return pl.pallas_call(
        _ssd_body,
        grid=(batch, nc),
        in_specs=[pl.BlockSpec((L, CONV_DIM), lambda b, c: (b * nc + c, 0)),
                  pl.BlockSpec((L, LANES), lambda b, c: (b * nc + c, 1)),
                  pl.BlockSpec((L, D_INNER), lambda b, c: (b * nc + c, 0)),
                  const2((SSM_CONV, CONV_DIM)), const2((1, CONV_DIM)),
                  const2((1, SSM_HEADS)), const2((1, SSM_HEADS)),
                  const2((1, D_INNER)), const2((1, D_INNER)),
                  const2((3 * SSM_HEADS, SSM_HEADS * LANES)), const2((3 * SSM_HEADS, D_INNER))],
        out_specs=[pl.BlockSpec((L, D_INNER), lambda b, c: (b * nc + c, 0)),
                   pl.BlockSpec((None, SSM_HEADS, SSM_HEADDIM, D_STATE), lambda b, c: (b, 0, 0, 0))],
        out_shape=[jax.ShapeDtypeStruct((M, D_INNER), BF16),
                   jax.ShapeDtypeStruct((batch, SSM_HEADS, SSM_HEADDIM, D_STATE), F32)],
        scratch_shapes=[pltpu.VMEM((L + SUBLANES, CONV_DIM), F32),
                        pltpu.VMEM((L, D_INNER), F32),
                        pltpu.VMEM((L, SSM_HEADS * LANES), F32),
                        pltpu.VMEM((L, D_INNER), F32),
                        pltpu.VMEM((L, D_INNER), F32),
                        pltpu.VMEM((L, D_INNER), F32)],
        compiler_params=_cparams(("arbitrary", "arbitrary")),
        name=name,
    )(xbc, dtg, z, conv_w, row(conv_b), row(dt_bias), row(a_log),
      row(jnp.repeat(d_skip, SSM_HEADDIM)), row(norm_g), expand(LANES), expand(SSM_HEADDIM))


def _t5_bucket(dist):
    n = jnp.maximum(dist, 0)
    max_exact = N_BUCKETS // 2
    nf = jnp.maximum(n, max_exact).astype(F32)
    large = max_exact + (jnp.log(nf / max_exact) / math.log(MAX_DISTANCE / max_exact)
                         * (N_BUCKETS - max_exact)).astype(jnp.int32)
    return jnp.where(n < max_exact, n, jnp.minimum(large, N_BUCKETS - 1))


def _bias_tables(rel_bias):
    by_dist = rel_bias[_t5_bucket(jnp.arange((WIN_TILES + 1) * TQ))]
    qry = jnp.arange(TQ)[None, :]
    delta = (jnp.arange(WIN_TILES - 1, -1, -1) * TQ)[:, None]
    dist = delta - (TQ - 1) + jnp.arange(2 * TQ - 1)[None, :]
    ok = (dist >= 0) & (dist <= WINDOW)
    vals = jnp.where(ok[..., None], by_dist[jnp.clip(dist, 0)], NEG_INF).transpose(0, 2, 1)
    hank = jnp.tile(vals, (1, 1, TQ + 1))[..., :TQ * 2 * TQ].reshape(WIN_TILES, N_HEADS, TQ, 2 * TQ)
    tz = hank[:, :, ::-1, :TQ]
    tz = tz.reshape(WIN_TILES, N_KV, Q_PER_KV, TQ, TQ).transpose(1, 0, 3, 2, 4)
    tz = tz.reshape(N_KV, WIN_TILES, TQ, WIDTH)
    tile = jnp.arange(WIN_TILES)[None, :, None, None]

    def variants(tiles, n_var):
        n_tiles = tiles.shape[1]
        v = jnp.arange(n_var)[:, None, None, None]
        masked = tile[:, :n_tiles] < (n_tiles - 1 - v)
        out = jnp.where(masked[None], NEG_INF, tiles[:, None])
        return out.reshape(N_KV, n_var, n_tiles * TQ, WIDTH)

    tzw = variants(tz, WIN_TILES)
    tzn = variants(tz[:, WIN_TILES - 2:], 2)
    far = rel_bias[N_BUCKETS - 1]
    rel = jnp.arange(8)[:, None] - 4
    cdist = qry - L_CMP * rel - (L_CMP - 1)
    band = jnp.where((cdist >= 0)[..., None], by_dist[jnp.clip(cdist, 0)], NEG_INF)
    far_rows = jnp.broadcast_to(far[None, None, :], (1, TQ, N_HEADS))
    band = jnp.concatenate([band, far_rows, jnp.full((1, TQ, N_HEADS), NEG_INF, F32),
                            jnp.zeros((6, TQ, N_HEADS), F32)], axis=0)
    band = band.reshape(16, TQ, N_KV, Q_PER_KV).transpose(2, 0, 3, 1).reshape(N_KV, 16, WIDTH)
    parts = jnp.stack([p.astype(F32) for p in _split3(far)] + [jnp.zeros_like(far)] * (SUBLANES - 3))
    caug = jnp.broadcast_to(parts.reshape(SUBLANES, N_KV, Q_PER_KV, 1), (SUBLANES, N_KV, Q_PER_KV, TQ))
    caug = caug.transpose(1, 0, 2, 3).reshape(N_KV, SUBLANES, WIDTH)
    return tzw, tzn, band, caug


def _layer_weights(l, w_ada, w_in, cmp_pe, cmp_w1, cmp_w2, w_a, w_b, w_o, w_gu, w_down):
    wi = w_in[l]
    small = jnp.zeros((D_MODEL, 2 * LANES), F32)
    small = small.at[:, 0:NSA_GATE_COLS].set(wi[:, OFF_G:OFF_Z])
    small = small.at[:, LANES:LANES + SSM_HEADS].set(wi[:, OFF_DT:OFF_MG])
    c = lambda w: w.astype(BF16)
    w1r = cmp_w1[l].reshape(2, L_CMP, HEAD_DIM, CMP_HID)
    w1_pair = jnp.zeros((2, L_CMP, 2 * HEAD_DIM, 2 * CMP_HID), F32)
    w1_pair = w1_pair.at[:, :, 0:HEAD_DIM, 0:CMP_HID].set(w1r).at[:, :, HEAD_DIM:, CMP_HID:].set(w1r)
    pe_pair = jnp.tile(cmp_pe[l], (1, 1, 2))
    return dict(ada=c(w_ada[l]), q=c(wi[:, 0:OFF_KV]), kv=c(wi[:, OFF_KV:OFF_G]), small=c(small),
                w1_pair=c(w1_pair), pe_pair=pe_pair,
                z=c(wi[:, OFF_Z:OFF_XBC]), xbc=c(wi[:, OFF_XBC:OFF_DT]), mg=c(wi[:, OFF_MG:P_IN]),
                cmp_w1=c(cmp_w1[l]), cmp_w2=c(cmp_w2[l]), a=c(w_a[l]), b=c(w_b[l]), o=c(w_o[l]),
                gu=c(w_gu[l]), down=c(w_down[l]))


def _prompt_layer(x, mod, l, Wl, P, tables, batch, T):
    sh1, sc1, g1, sh2, sc2, g2 = [m.reshape(batch, 1, D_MODEL) for m in jnp.split(mod, 6, axis=-1)]
    tm = 1024 if T % 1024 == 0 else 512
    h = _norm_mod(x, P['norm1_g'][l], sc1, sh1, tm=512, rows_per_batch=T, name=f"norm1_p{l}")
    q = _mm(h, Wl['q'], tm=tm, tn=512, name=f"proj_q_p{l}", out_dtype=BF16,
            epilogue=lambda acc: acc * (SCALE * LOG2E))
    kv = _mm(h, Wl['kv'], tm=tm, tn=512, name=f"proj_kv_p{l}")
    small = _mm(h, Wl['small'], tm=tm, tn=2 * LANES, name=f"proj_small_p{l}")
    z = _mm(h, Wl['z'], tm=tm, tn=512, name=f"proj_z_p{l}")
    xbc = _mm(h, Wl['xbc'], tm=tm, tn=512, name=f"proj_xbc_p{l}")
    gates = _mm(h, Wl['mg'], tm=tm, tn=512, name=f"proj_mg_p{l}", epilogue=jax.nn.sigmoid)

    o_nsa = _prompt_attention(q, kv, small, l, Wl, P, tables, batch, T)
    kv6 = kv.reshape(batch, T, 6, N_KV, HEAD_DIM)

    y, ssm_state = _ssd_prompt(xbc, small, z, P['ssm_conv_w'][l], P['ssm_conv_b'][l], P['dt_bias'][l],
                               P['a_log'][l], P['d_skip'][l], P['ssm_norm_g'][l], batch=batch, name=f"ssd_p{l}")

    u = _merge(o_nsa, y, Wl['a'], Wl['b'], gates, tm=tm, tn=512, name=f"merge_p{l}")
    x = _mm(u, Wl['o'], tm=tm, tn=512, name=f"proj_o_p{l}", rows_per_batch=T,
            epilogue=lambda acc, xr, gr: xr + gr * acc, extras=[('full', x), ('batch', g1)])

    h2 = _norm_mod(x, P['norm2_g'][l], sc2, sh2, tm=512, rows_per_batch=T, name=f"norm2_p{l}")
    act = _ffn_up(h2, Wl['gu'], P['ffn_conv_w'][l], P['ffn_conv_b'][l], tm=512, tn=D_FF // 2,
                  rows_per_batch=T, name=f"ffn_up_p{l}")
    h2_last = h2.reshape(batch, T, D_MODEL)[:, T - SUBLANES:].reshape(batch * SUBLANES, D_MODEL)
    gate_last = _mm(h2_last, Wl['gu'][:, 0:D_FF], tm=batch * SUBLANES, tn=D_FF // 2, name=f"ffn_state_p{l}")
    ffn_state = gate_last.reshape(batch, SUBLANES, D_FF)[:, SUBLANES - (FFN_CONV - 1):]
    x = _mm(act, Wl['down'], tm=tm, tn=512, name=f"ffn_down_p{l}", rows_per_batch=T,
            epilogue=lambda acc, xr, gr: xr + gr * acc, extras=[('full', x), ('batch', g2)])

    kv_rows = kv6[:, :, :4]
    win_state = kv6[:, T - min(WINDOW, T):, 4:]
    conv_state = xbc.reshape(batch, T, CONV_DIM)[:, T - (SSM_CONV - 1):]
    return x, (kv_rows, win_state, conv_state, ssm_state, ffn_state)


def _prompt_attention(q, kv, small, l, Wl, P, tables, batch, T):
    nq = T // TQ
    kv6 = kv.reshape(batch, T, 6, N_KV, HEAD_DIM)
    n_cmp = T // L_CMP

    def cmp_in(slot):
        xb = kv6[:, :, slot].reshape(batch, n_cmp, L_CMP, N_KV, HEAD_DIM).transpose(0, 3, 1, 2, 4)
        return xb.reshape(batch * N_KV * n_cmp, L_CMP * HEAD_DIM)

    ctm = min(512, batch * N_KV * n_cmp)
    kc = _compress(cmp_in(0), P['cmp_pe'][l, 0], Wl['cmp_w1'][0], Wl['cmp_w2'][0], tm=ctm, name=f"cmp_k_p{l}")
    vc = _compress(cmp_in(1), P['cmp_pe'][l, 1], Wl['cmp_w1'][1], Wl['cmp_w2'][1], tm=ctm, name=f"cmp_v_p{l}")

    def even_odd(c):
        c = c.reshape(batch, N_KV, n_cmp // 2, 2, HEAD_DIM).transpose(0, 1, 3, 2, 4)
        return c.reshape(batch, N_KV, n_cmp, HEAD_DIM)

    kc = even_odd(kc.reshape(batch, N_KV, n_cmp, HEAD_DIM)).astype(BF16)
    vct = even_odd(vc.reshape(batch, N_KV, n_cmp, HEAD_DIM)).astype(BF16).transpose(0, 1, 3, 2)

    def k_tiles(slot, front, back, aug):
        k = kv6[:, :, slot].astype(BF16).reshape(batch, nq, TQ, N_KV, HEAD_DIM).transpose(0, 3, 1, 2, 4)
        k = jnp.pad(k, ((0, 0), (0, 0), (front, back), (0, 0), (0, LANES - HEAD_DIM)))
        if aug:
            kt = jnp.arange(front + nq + back) - front
            blk_col = AUG_ROW0 + 2 * (kt % FAR_TILES)[:, None] + (jnp.arange(TQ) // L_SEL)[None, :]
            lane = jnp.arange(LANES)[None, None, :]
            ones = (lane == blk_col[..., None]) | ((lane >= AUG_ROW0 + SUBLANES) & (lane < AUG_ROW0 + SUBLANES + 3))
            k = jnp.where(ones[None, None], jnp.ones((), BF16), k)
        return k

    def vt_tiles(slot, front, back):
        v = kv6[:, :, slot].astype(BF16).reshape(batch, nq, TQ, N_KV, HEAD_DIM).transpose(0, 3, 1, 4, 2)
        v = jnp.pad(v, ((0, 0), (0, 0), (front, back), (0, BF16_ROWS), (0, 0)))
        return v.at[:, :, :, HEAD_DIM, :].set(1.0)

    q2 = q.reshape(batch, nq, TQ, N_KV, Q_PER_KV, HEAD_DIM).transpose(0, 3, 1, 5, 4, 2)
    q2 = q2.reshape(batch, N_KV, nq, HEAD_DIM, WIDTH)
    ng = small[:, 0:NSA_GATE_COLS].reshape(batch, nq, TQ, 3, N_KV, Q_PER_KV).transpose(0, 4, 1, 3, 5, 2)
    ng = ng.reshape(batch, N_KV, nq, 3, WIDTH)
    o_t = _nsa_prompt(q2, k_tiles(2, 1, 2 * FAR_TILES, True), vt_tiles(3, 1, 2 * FAR_TILES),
                      k_tiles(4, WIN_TILES - 1, 0, False), vt_tiles(5, WIN_TILES - 1, 0),
                      kc, vct, ng, tables, name=f"nsa_p{l}")
    return o_t.transpose(0, 2, 1).reshape(batch * T, Q_COLS)


def _pool_compress_body(x_ref, pe_ref, w1_ref, w2_ref, o_ref, rows_scr, *, pages, page):
    rows = pages * page // L_CMP
    for s in range(2):
        outs = []
        for pair in range(N_KV // 2):
            for p in range(pages):
                rows_scr[p * page:(p + 1) * page, :] = x_ref[p, s, 2 * pair:2 * pair + 2].reshape(
                    2 * HEAD_DIM, page).T
            xs = [(rows_scr[pl.ds(l, rows, stride=L_CMP), :] + pe_ref[s, l:l + 1, :]).astype(BF16)
                  for l in range(L_CMP)]
            w1 = w1_ref[s].reshape(L_CMP * 2 * HEAD_DIM, 2 * CMP_HID)
            hid = _silu(_dot(jnp.concatenate(xs, axis=1), w1)).astype(BF16)
            for gg in range(2):
                outs.append(_dot(hid[:, gg * CMP_HID:(gg + 1) * CMP_HID], w2_ref[s]))
        o_ref[:, s * N_KV * HEAD_DIM:(s + 1) * N_KV * HEAD_DIM] = jnp.concatenate(outs, axis=1)


def _pool_compress(cache_t, l, pe2, w1pair, w2, *, pages_per_step, name):
    n_pool, page = cache_t.shape[1], cache_t.shape[5]
    rows_out = pages_per_step * page // L_CMP
    half = 2 * N_KV * HEAD_DIM
    return pl.pallas_call(
        functools.partial(_pool_compress_body, pages=pages_per_step, page=page),
        grid=(n_pool // pages_per_step,),
        in_specs=[pl.BlockSpec((None, pages_per_step, 2, N_KV, HEAD_DIM, page), lambda i: (l, i, 0, 0, 0, 0)),
                  pl.BlockSpec(pe2.shape, lambda i: (0, 0, 0)),
                  pl.BlockSpec(w1pair.shape, lambda i: (0, 0, 0, 0)),
                  pl.BlockSpec(w2.shape, lambda i: (0, 0, 0))],
        out_specs=pl.BlockSpec((rows_out, half), lambda i: (i, 0)),
        out_shape=jax.ShapeDtypeStruct((n_pool * page // L_CMP, half), F32),
        scratch_shapes=[pltpu.VMEM((pages_per_step * page, 2 * HEAD_DIM), F32)],
        compiler_params=_cparams(("parallel",)),
        name=name,
    )(cache_t, pe2, w1pair, w2)


def _softmax_lanes(s):
    m = jnp.max(s, axis=1, keepdims=True)
    e = jnp.exp(s - m)
    return m, e, jnp.sum(e, axis=1, keepdims=True)


def _sum3(lhs_bf16, x):
    hi, mid, lo = _split3(x)
    return _dot(lhs_bf16, hi) + _dot(lhs_bf16, mid) + _dot(lhs_bf16, lo)


def _nsa_decode_body(pt_ref, q_ref, gate_ref, kvnew_ref, kcn_ref, wincol_ref, *refs,
                     n_pages, n_blocks, top_n, win_len, has_prev):
    sel_pages = refs[0:n_pages]
    kc_pages = refs[n_pages:2 * n_pages]
    rest = refs[2 * n_pages:]
    win_ref, bsel_ref, bwin_ref, bcmp_ref, e_ref, pair_ref, rsum_ref, rexp_ref = rest[0:8]
    o_ref, wout_ref, kc_scr, s_scr = rest[8 + int(has_prev):]
    gd = N_KV * HEAD_DIM

    last = lax.broadcasted_iota(jnp.int32, (gd, win_len), 1) == win_len - 1
    for s in range(2):
        shifted = pltpu.roll(win_ref[s].reshape(gd, win_len), win_len - 1, axis=1)
        wout_ref[s] = jnp.where(last, wincol_ref[s * gd:(s + 1) * gd, :], shifted).reshape(
            N_KV, HEAD_DIM, win_len)
    page = sel_pages[0].shape[3]
    blocks_per_page = kc_pages[0].shape[0]
    n_past_blocks = blocks_per_page * n_pages

    @pl.when(pl.program_id(0) == 0)
    def _():
        kc_scr[...] = jnp.zeros(kc_scr.shape, F32)

    q = q_ref[...]
    q_f = q.astype(F32)
    lane0 = lax.broadcasted_iota(jnp.int32, (N_HEADS, LANES), 1) == 0
    rb = lambda v: v.astype(BF16).astype(F32)

    def new_scores(slot):
        k_new = rb(kvnew_ref[:, slot * gd:(slot + 1) * gd])
        return jnp.where(lane0, jnp.sum(q_f * k_new, axis=1, keepdims=True), 0.0)

    def new_value(e_tile, slot):
        return rb(e_tile[:, 0:1]) * rb(kvnew_ref[:, slot * gd:(slot + 1) * gd])

    for p in range(n_pages):
        kc_scr[p * blocks_per_page:(p + 1) * blocks_per_page, :] = kc_pages[p][...]
    kc_scr[n_past_blocks:n_past_blocks + SUBLANES, :] = kcn_ref[...]
    kcv = kc_scr[...]
    m_c, e_c, den_c = _softmax_lanes(_dot_nt(q, kcv[:, 0:gd].astype(BF16)) + bcmp_ref[...])
    p_c = e_c * ((m_c > 0.5 * NEG_INF).astype(F32) / den_c)
    o_cmp = _dot(p_c.astype(BF16), kcv[:, gd:2 * gd].astype(BF16))
    imp = _sum3(rsum_ref[...], _sum3_rhs(p_c, pair_ref[...]))

    lane_i = lax.broadcasted_iota(jnp.int32, (SUBLANES, LANES), 1)
    forced = (lane_i == 0) | (lane_i == n_blocks - 1)
    score = jnp.where(lane_i < n_blocks, jnp.where(forced, 1e30, imp), -1.0)
    score_t = score.T
    ii = lax.broadcasted_iota(jnp.int32, (LANES, LANES), 0)
    jj = lax.broadcasted_iota(jnp.int32, (LANES, LANES), 1)
    sel_rows = []
    for g in range(N_KV):
        col = jnp.broadcast_to(score_t[:, g:g + 1], (LANES, LANES))
        rowv = jnp.broadcast_to(score[g:g + 1, :], (LANES, LANES))
        ahead = (col > rowv) | ((col == rowv) & (ii < jj))
        rank = jnp.sum(ahead.astype(F32), axis=0, keepdims=True)
        sel_rows.append((rank < float(top_n)).astype(F32))
    sel = jnp.concatenate(sel_rows + [jnp.zeros((SUBLANES - N_KV, LANES), F32)], axis=0)
    sel = jnp.where(lane_i < n_blocks, sel, 0.0)
    sel_h = _dot(rexp_ref[...], sel.astype(BF16))
    sel_keys = _dot(sel_h.astype(BF16), e_ref[...])

    for p in range(n_pages):
        s_scr[:, p * page:(p + 1) * page] = _dot(q, sel_pages[p][0].reshape(gd, page).astype(BF16))
    s_scr[:, n_pages * page:n_pages * page + LANES] = new_scores(2)
    s = jnp.where(sel_keys > 0.5, s_scr[...] + bsel_ref[...], NEG_INF)
    _, e_s, den_s = _softmax_lanes(s)
    e_b = e_s.astype(BF16)
    o_sel = new_value(e_s[:, n_pages * page:n_pages * page + LANES], 3)
    for p in range(n_pages):
        o_sel = o_sel + _dot_nt(e_b[:, p * page:(p + 1) * page], sel_pages[p][1].reshape(gd, page).astype(BF16))
    o_sel = o_sel * (1.0 / den_s)

    s_w = jnp.concatenate([_dot(q, win_ref[0].reshape(gd, win_len).astype(BF16)), new_scores(4)], axis=1)
    _, e_w, den_w = _softmax_lanes(s_w + bwin_ref[...])
    o_win = (_dot_nt(e_w[:, 0:win_len].astype(BF16), win_ref[1].reshape(gd, win_len).astype(BF16))
             + new_value(e_w[:, win_len:win_len + LANES], 5)) * (1.0 / den_w)

    g = jax.nn.sigmoid(gate_ref[...])
    o = g[:, 0:1] * o_cmp + g[:, 1:2] * o_sel + g[:, 2:3] * o_win
    own = (lax.broadcasted_iota(jnp.int32, (N_HEADS, gd), 1) // HEAD_DIM
           == lax.broadcasted_iota(jnp.int32, (N_HEADS, gd), 0) // Q_PER_KV)
    o = jnp.where(own, o, 0.0)
    acc = o[:, 0:HEAD_DIM]
    for gi in range(1, N_KV):
        acc = acc + o[:, gi * HEAD_DIM:(gi + 1) * HEAD_DIM]
    o_ref[...] = acc


def _sum3_rhs(x, rhs_bf16):
    hi, mid, lo = _split3(x)
    return _dot(hi, rhs_bf16) + _dot(mid, rhs_bf16) + _dot(lo, rhs_bf16)


def _nsa_decode(page_table, q_bd, gates_t, kvnew, kcn, cache_t, kcpool, win_t, win_out, l, tabs, *, name):
    B, n_pages = page_table.shape
    page = cache_t.shape[5]
    gd = N_KV * HEAD_DIM
    win_len = win_t.shape[5]
    bsel, bwin, bcmp, e_mat, pair, rsum, rexp, n_blocks = tabs
    blocks_per_page = kcpool.shape[1]
    const = lambda a: pl.BlockSpec(a.shape, lambda b, pt: (0,) * a.ndim)
    sel_specs = [pl.BlockSpec((None, None, 2, N_KV, HEAD_DIM, page), lambda b, pt, p=p: (l, pt[b, p], 1, 0, 0, 0))
                 for p in range(n_pages)]
    kc_specs = [pl.BlockSpec((None, blocks_per_page, 2 * gd), lambda b, pt, p=p: (pt[b, p], 0, 0))
                for p in range(n_pages)]
    win_spec = pl.BlockSpec((None, None, 2, N_KV, HEAD_DIM, win_len), lambda b, pt: (l, b, 0, 0, 0, 0))
    ops = [page_table, q_bd, gates_t, kvnew, kcn, kvnew[:, 0, 4 * gd:].reshape(B, 2 * gd, 1),
           *([cache_t] * n_pages), *([kcpool] * n_pages), win_t, bsel, bwin, bcmp, e_mat, pair, rsum, rexp]
    in_specs = ([pl.BlockSpec((None, N_HEADS, gd), lambda b, pt: (b, 0, 0)),
                 pl.BlockSpec((None, N_HEADS, 3), lambda b, pt: (b, 0, 0)),
                 pl.BlockSpec((None, 1, kvnew.shape[2]), lambda b, pt: (b, 0, 0)),
                 pl.BlockSpec((None, SUBLANES, 2 * gd), lambda b, pt: (b, 0, 0)),
                 pl.BlockSpec((None, 2 * gd, 1), lambda b, pt: (b, 0, 0))]
                + sel_specs + kc_specs
                + [win_spec, const(bsel), const(bwin), const(bcmp), const(e_mat), const(pair), const(rsum),
                   const(rexp)])
    aliases = {}
    if win_out is not None:
        aliases = {len(ops): 1}
        ops.append(win_out)
        in_specs.append(pl.BlockSpec(memory_space=pl.ANY))
    grid_spec = pltpu.PrefetchScalarGridSpec(
        num_scalar_prefetch=1,
        grid=(B,),
        in_specs=in_specs,
        out_specs=[pl.BlockSpec((None, N_HEADS, HEAD_DIM), lambda b, pt: (b, 0, 0)), win_spec],
        scratch_shapes=[pltpu.VMEM((LANES, 2 * gd), F32),
                        pltpu.VMEM((N_HEADS, n_pages * page + LANES), F32)])
    return pl.pallas_call(
        functools.partial(_nsa_decode_body, n_pages=n_pages, n_blocks=n_blocks,
                          top_n=min(TOP_N, n_blocks), win_len=win_len, has_prev=win_out is not None),
        grid_spec=grid_spec,
        out_shape=[jax.ShapeDtypeStruct((B, N_HEADS, HEAD_DIM), F32), jax.ShapeDtypeStruct(win_t.shape, F32)],
        input_output_aliases=aliases,
        compiler_params=_cparams(("arbitrary",)),
        name=name,
    )(*ops)


def _decode_tables(rel_bias, past_len, n_pages, win_len):
    q_pos = past_len
    n_blocks = (past_len + 1 + L_SEL - 1) // L_SEL
    n_cmp = n_blocks * (L_SEL // L_CMP)
    by_dist = rel_bias[_t5_bucket(jnp.arange(q_pos + 1))].T
    keys = (n_pages + 1) * LANES
    kpos = jnp.arange(keys)
    bsel = jnp.where((kpos <= q_pos)[None], by_dist[:, jnp.clip(q_pos - kpos, 0)], NEG_INF)
    wl = jnp.arange(win_len + LANES)
    wdist = win_len - wl
    bwin = jnp.where((wdist >= 0)[None], by_dist[:, jnp.clip(wdist, 0, q_pos)], NEG_INF)
    c = jnp.arange(LANES)
    cdist = q_pos - (c * L_CMP + L_CMP - 1)
    bcmp = jnp.where(((cdist >= 0) & (c < n_cmp))[None], by_dist[:, jnp.clip(cdist, 0)], NEG_INF)
    e_mat = (jnp.arange(LANES)[:, None] == (kpos // L_SEL)[None, :]).astype(BF16)
    pair = ((c[:, None] // (L_SEL // L_CMP) == c[None, :]) & (c[:, None] < n_cmp)).astype(BF16)
    heads = jnp.arange(N_HEADS)
    rsum = (jnp.arange(SUBLANES)[:, None] == (heads // Q_PER_KV)[None, :]).astype(BF16)
    return bsel, bwin, bcmp, e_mat, pair, rsum, rsum.T, n_blocks


def _conv_step_body(x_ref, p0_ref, p1_ref, p2_ref, cw_ref, cb_ref, dt_ref, dtb_ref, alog_ref,
                    xact_ref, dto_ref, dec_ref):
    y = cb_ref[...]
    for k, ref in enumerate((p0_ref, p1_ref, p2_ref, x_ref)):
        y = y + cw_ref[k:k + 1, :] * ref[...]
    xact_ref[...] = _silu(y)
    x = dt_ref[:, 0:SSM_HEADS] + dtb_ref[...]
    dt = jnp.maximum(x, 0.0) + jnp.log1p(jnp.exp(-jnp.abs(x)))
    dto_ref[...] = dt
    dec_ref[...] = jnp.exp(dt * (-jnp.exp(alog_ref[...])))


def _conv_step(xbc, prev, conv_w, conv_b, dtg, dt_bias, a_log, *, name):
    B = xbc.shape[0]
    full = lambda a: pl.BlockSpec(a.shape, lambda i: (0,) * a.ndim)
    ops = [xbc, prev[:, 0], prev[:, 1], prev[:, 2], conv_w, conv_b.reshape(1, -1), dtg,
           dt_bias.reshape(1, -1), a_log.reshape(1, -1)]
    in_specs = [full(a) for a in ops]
    in_specs[6] = pl.BlockSpec((B, LANES), lambda i: (0, 1))
    return pl.pallas_call(
        _conv_step_body,
        grid=(1,),
        in_specs=in_specs,
        out_specs=[pl.BlockSpec((B, CONV_DIM), lambda i: (0, 0)), pl.BlockSpec((B, SSM_HEADS), lambda i: (0, 0)),
                   pl.BlockSpec((B, SSM_HEADS), lambda i: (0, 0))],
        out_shape=[jax.ShapeDtypeStruct((B, CONV_DIM), F32), jax.ShapeDtypeStruct((B, SSM_HEADS), F32),
                   jax.ShapeDtypeStruct((B, SSM_HEADS), F32)],
        compiler_params=_cparams(("arbitrary",)),
        name=name,
    )(*ops)


def _ssd_step_body(dts_ref, decs_ref, h0_ref, xs_ref, xst_ref, bm_ref, cm_ref, z_ref, dtc_ref, dskip_ref, ng_ref,
                   *refs, has_prev):
    y_ref, st_ref, yoff_scr = refs[int(has_prev):]
    b = pl.program_id(0)
    heads_per_group = SSM_HEADS // SSM_GROUPS
    bm = bm_ref[...]
    cm = cm_ref[...]
    rb = lambda v: v.astype(BF16).astype(F32)
    cb = jnp.sum(rb(cm) * rb(bm), axis=-1, keepdims=True)
    cm8 = jnp.concatenate([cm, jnp.zeros((SUBLANES - SSM_GROUPS, D_STATE), F32)], axis=0).astype(BF16)
    for h in range(SSM_HEADS):
        g = h // heads_per_group
        h_prev = h0_ref[h]
        dec = decs_ref[b, h]
        y_off = _dot_nt(cm8, h_prev.astype(BF16))
        yoff_scr[h:h + 1, :] = y_off[g:g + 1, :] * dec
        xdt = xst_ref[:, h:h + 1] * dts_ref[b, h]
        st_ref[h] = h_prev * dec + xdt * bm[g:g + 1, :]
    xs = xs_ref[...]
    cbh = jnp.concatenate([jnp.broadcast_to(cb[g:g + 1, :], (heads_per_group, 1)) for g in range(SSM_GROUPS)], axis=0)
    y = (cbh * dtc_ref[...]) * xs + yoff_scr[...] + dskip_ref[...] * xs
    y = y * _silu(z_ref[...])
    for g in range(SSM_GROUPS):
        yg = y[g * heads_per_group:(g + 1) * heads_per_group, :]
        ms = jnp.sum(jnp.sum(yg * yg, axis=1, keepdims=True), axis=0, keepdims=True) * (1.0 / (heads_per_group * SSM_HEADDIM))
        y_ref[g * heads_per_group:(g + 1) * heads_per_group, :] = (
            yg * lax.rsqrt(ms + EPS) * ng_ref[g * heads_per_group:(g + 1) * heads_per_group, :]).astype(y_ref.dtype)


def _ssd_step(state, state_out, l, xact, dt, dec, z, d_skip, norm_g, *, name):
    B = xact.shape[0]
    H, Pd, N = SSM_HEADS, SSM_HEADDIM, D_STATE
    xs = xact[:, 0:D_INNER].reshape(B, H, Pd)
    bm = xact[:, D_INNER:D_INNER + SSM_GROUPS * N].reshape(B, SSM_GROUPS, N)
    cm = xact[:, D_INNER + SSM_GROUPS * N:].reshape(B, SSM_GROUPS, N)
    per_b = lambda shape: pl.BlockSpec((None,) + shape, lambda b, *_: (b,) + (0,) * len(shape))
    const = lambda shape: pl.BlockSpec(shape, lambda b, *_: (0,) * len(shape))
    state_spec = pl.BlockSpec((None, None, H, Pd, N), lambda b, *_: (l, b, 0, 0, 0))
    ops = [dt, dec, state, xs, xs.transpose(0, 2, 1), bm, cm, z.reshape(B, H, Pd), dt.reshape(B, H, 1),
           d_skip.reshape(H, 1), norm_g.reshape(H, Pd)]
    in_specs = [state_spec, per_b((H, Pd)), per_b((Pd, H)), per_b((SSM_GROUPS, N)), per_b((SSM_GROUPS, N)),
                per_b((H, Pd)), per_b((H, 1)), const((H, 1)), const((H, Pd))]
    aliases = {}
    if state_out is not None:
        aliases = {len(ops): 1}
        ops.append(state_out)
        in_specs.append(pl.BlockSpec(memory_space=pl.ANY))
    grid_spec = pltpu.PrefetchScalarGridSpec(
        num_scalar_prefetch=2,
        grid=(B,),
        in_specs=in_specs,
        out_specs=[per_b((H, Pd)), state_spec],
        scratch_shapes=[pltpu.VMEM((H, Pd), F32)])
    return pl.pallas_call(
        functools.partial(_ssd_step_body, has_prev=state_out is not None),
        grid_spec=grid_spec,
        out_shape=[jax.ShapeDtypeStruct((B, H, Pd), BF16), jax.ShapeDtypeStruct(state.shape, F32)],
        input_output_aliases=aliases,
        compiler_params=_cparams(("arbitrary",)),
        name=name,
    )(*ops)


def _ffn_up_step_body(h_ref, wg_ref, wu_ref, p0_ref, p1_ref, cw_ref, cb_ref, act_ref, gate_ref):
    h = h_ref[...]
    gate = _dot(h, wg_ref[...])
    up = _dot(h, wu_ref[...])
    y = cb_ref[...] + cw_ref[0:1, :] * p0_ref[...] + cw_ref[1:2, :] * p1_ref[...] + cw_ref[2:3, :] * gate
    gate_ref[...] = gate
    act_ref[...] = (_silu(y) * up).astype(act_ref.dtype)


def _ffn_up_step(h, w_gu, prev, conv_w, conv_b, *, tn, name):
    B, K = h.shape
    nj = D_FF // tn
    col = lambda r: pl.BlockSpec((r, tn), lambda j: (0, j))
    return pl.pallas_call(
        _ffn_up_step_body,
        grid=(nj,),
        in_specs=[pl.BlockSpec((B, K), lambda j: (0, 0)), col(K),
                  pl.BlockSpec((K, tn), lambda j, nj=nj: (0, j + nj)),
                  col(B), col(B), col(FFN_CONV), col(1)],
        out_specs=[col(B), col(B)],
        out_shape=[jax.ShapeDtypeStruct((B, D_FF), BF16), jax.ShapeDtypeStruct((B, D_FF), F32)],
        compiler_params=_cparams(("parallel",)),
        name=name,
    )(h, w_gu, w_gu, prev[:, 0], prev[:, 1], conv_w, conv_b.reshape(1, D_FF))


def _pages_per_step(n_pool, limit=32):
    return max(p for p in range(2, limit + 1, 2) if n_pool % p == 0)


def _sample_layer(x, mod, l, Wl, P, C):
    B = x.shape[0]
    gd = N_KV * HEAD_DIM
    sh1, sc1, g1, sh2, sc2, g2 = jnp.split(mod, 6, axis=-1)
    h = _norm_mod(x, P['norm1_g'][l], sc1, sh1, tm=B, rows_per_batch=1, name=f"norm1_s{l}")
    q = _mm(h, Wl['q'], tm=B, tn=512, name=f"proj_q_s{l}", out_dtype=BF16, epilogue=lambda acc: acc * SCALE)
    kv = _mm(h, Wl['kv'], tm=B, tn=512, name=f"proj_kv_s{l}")
    small = _mm(h, Wl['small'], tm=B, tn=2 * LANES, name=f"proj_small_s{l}")
    z = _mm(h, Wl['z'], tm=B, tn=512, name=f"proj_z_s{l}")
    xbc = _mm(h, Wl['xbc'], tm=B, tn=512, name=f"proj_xbc_s{l}")
    gates = _mm(h, Wl['mg'], tm=B, tn=512, name=f"proj_mg_s{l}", epilogue=jax.nn.sigmoid)

    cache_kv, page_table = C['cache_kv'], C['page_table']
    n_pool, page = cache_kv.shape[1], cache_kv.shape[2]
    cache_t = cache_kv.transpose(0, 1, 3, 4, 5, 2)
    win_t = C['cache_win_kv'].transpose(0, 1, 3, 4, 5, 2)
    kcpool = _pool_compress(cache_t, l, Wl['pe_pair'], Wl['w1_pair'], Wl['cmp_w2'],
                            pages_per_step=_pages_per_step(n_pool), name=f"cmp_pool_s{l}")
    kcpool = kcpool.reshape(n_pool, page // L_CMP, 2 * gd)

    def new_blocks(slot):
        first = jnp.pad(kv[:, slot * gd:(slot + 1) * gd].reshape(B, N_KV, 1, HEAD_DIM),
                        ((0, 0), (0, 0), (0, 1), (0, (L_CMP - 1) * HEAD_DIM)))
        rows = B * N_KV * 2
        out = _compress(first.reshape(rows, L_CMP * HEAD_DIM), P['cmp_pe'][l, slot], Wl['cmp_w1'][slot],
                        Wl['cmp_w2'][slot], tm=512 if rows % 512 == 0 else rows, name=f"cmp{slot}_new_s{l}")
        return out.reshape(B, N_KV, 2, HEAD_DIM).transpose(0, 2, 1, 3).reshape(B, 2, gd)

    kcn = jnp.pad(jnp.concatenate([new_blocks(0), new_blocks(1)], axis=-1), ((0, 0), (0, SUBLANES - 2), (0, 0)))
    qh = q.reshape(B, N_KV, Q_PER_KV, 1, HEAD_DIM)
    q_bd = (qh * jnp.eye(N_KV, dtype=BF16)[None, :, None, :, None]).reshape(B, N_HEADS, gd)
    gates_t = small[:, 0:NSA_GATE_COLS].reshape(B, 3, N_HEADS).transpose(0, 2, 1)
    o_nsa, C['win_out'] = _nsa_decode(page_table, q_bd, gates_t, kv.reshape(B, 1, KV_COLS), kcn, cache_t, kcpool,
                                      win_t, C.get('win_out'), l, C['tables'], name=f"nsa_s{l}")
    o_nsa = o_nsa.reshape(B, Q_COLS)
    new_rows = kv[:, 0:4 * gd].reshape(B, 1, 4, N_KV, HEAD_DIM)

    conv_prev = C['state_ssm_conv'][l]
    xact, dt, dec = _conv_step(xbc, conv_prev, P['ssm_conv_w'][l], P['ssm_conv_b'][l], small,
                               P['dt_bias'][l], P['a_log'][l], name=f"conv_s{l}")
    conv_state = jnp.concatenate([conv_prev[:, 1:], xbc[:, None]], axis=1)
    y, C['ssm_out'] = _ssd_step(C['state_ssm'], C.get('ssm_out'), l, xact, dt, dec, z, P['d_skip'][l],
                                P['ssm_norm_g'][l], name=f"ssd_s{l}")

    u = _merge(o_nsa.astype(BF16), y.reshape(B, D_INNER), Wl['a'], Wl['b'], gates, tm=B, tn=512, name=f"merge_s{l}")
    x = _mm(u, Wl['o'], tm=B, tn=512, name=f"proj_o_s{l}",
            epilogue=lambda acc, xr, gr: xr + gr * acc, extras=[('full', x), ('full', g1)])

    h2 = _norm_mod(x, P['norm2_g'][l], sc2, sh2, tm=B, rows_per_batch=1, name=f"norm2_s{l}")
    ffn_prev = C['state_ffn_conv'][l]
    act, gate_raw = _ffn_up_step(h2, Wl['gu'], ffn_prev, P['ffn_conv_w'][l], P['ffn_conv_b'][l],
                                 tn=D_FF // 2, name=f"ffn_up_s{l}")
    ffn_state = jnp.concatenate([ffn_prev[:, 1:], gate_raw[:, None]], axis=1)
    x = _mm(act, Wl['down'], tm=B, tn=512, name=f"ffn_down_s{l}",
            epilogue=lambda acc, xr, gr: xr + gr * acc, extras=[('full', x), ('full', g2)])
    return x, (new_rows, conv_state, ffn_state)


def kernel(x_prompt, x_sample, cache_kv, cache_win_kv, state_ssm_conv, state_ssm, state_ffn_conv, page_table, c_prompt, c_sample, w_ada, b_ada, norm1_g, norm2_g, final_g, w_in, rel_bias, cmp_pe, cmp_w1, cmp_w2, ssm_conv_w, ssm_conv_b, dt_bias, a_log, d_skip, ssm_norm_g, w_a, w_b, w_o, w_gu, ffn_conv_w, ffn_conv_b, w_down):
    P = dict(norm1_g=norm1_g, norm2_g=norm2_g, rel_bias=rel_bias, cmp_pe=cmp_pe.reshape(DEPTH, 2, -1),
             ssm_conv_w=ssm_conv_w, ssm_conv_b=ssm_conv_b, dt_bias=dt_bias, a_log=a_log, d_skip=d_skip,
             ssm_norm_g=ssm_norm_g, ffn_conv_w=ffn_conv_w, ffn_conv_b=ffn_conv_b)
    Bp, T, _ = x_prompt.shape
    Bs = x_sample.shape[0]
    past_len = page_table.shape[1] * cache_kv.shape[2]
    tables = _bias_tables(rel_bias * LOG2E)
    n_c = Bp + Bs
    c_rows = -(-n_c // SUBLANES) * SUBLANES
    c_all = jnp.zeros((c_rows, D_MODEL), F32).at[:Bp].set(c_prompt).at[Bp:n_c].set(c_sample).astype(BF16)
    xp = x_prompt.reshape(Bp * T, D_MODEL)
    xs = x_sample.reshape(Bs, D_MODEL)
    C = dict(cache_kv=cache_kv, page_table=page_table, cache_win_kv=cache_win_kv, state_ssm_conv=state_ssm_conv,
             state_ssm=state_ssm, state_ffn_conv=state_ffn_conv,
             tables=_decode_tables(rel_bias, past_len, page_table.shape[1], cache_win_kv.shape[2]))
    st_p, st_s = [], []
    for l in range(DEPTH):
        Wl = _layer_weights(l, w_ada, w_in, cmp_pe, cmp_w1, cmp_w2, w_a, w_b, w_o, w_gu, w_down)
        mod = _mm(c_all, Wl['ada'], tm=c_rows, tn=512, name=f"ada{l}",
                  epilogue=lambda acc, b: acc + b, extras=[('col', b_ada[l].reshape(1, -1))])
        xp, sp = _prompt_layer(xp, mod[:Bp], l, Wl, P, tables, Bp, T)
        st_p.append(sp)
        xs, ss = _sample_layer(xs, mod[Bp:n_c], l, Wl, P, C)
        st_s.append(ss)
    y_prompt = _final_norm(xp, final_g, tm=512, name="final_norm_p").reshape(Bp, T, D_MODEL)
    y_sample = _final_norm(xs, final_g, tm=Bs, name="final_norm_s").reshape(Bs, 1, D_MODEL)
    kv_p, win_p, conv_p, ssm_p, ffn_p = [jnp.stack([s[i] for s in st_p]) for i in range(5)]
    kv_s, conv_s, ffn_s = [jnp.stack([s[i] for s in st_s]) for i in range(3)]
    win_s = C['win_out'].transpose(0, 1, 5, 2, 3, 4)
    ssm_s = C['ssm_out']
    return (y_prompt, y_sample, kv_p, win_p, conv_p, ssm_p, ffn_p, kv_s, win_s, conv_s, ssm_s, ffn_s)
```

```python
import functools
import math

import jax
import jax.numpy as jnp
from jax import lax
from jax.experimental import pallas as pl
from jax.experimental.pallas import tpu as pltpu

D_MODEL = 1024
N_HEADS = 16
N_KV = 4
HEAD_DIM = 64
Q_PER_KV = N_HEADS // N_KV
L_CMP = 32
L_SEL = 64
TOP_N = 16
WINDOW = 512
CMP_HID = 128
Q_BLOCK = 128
SCALE = HEAD_DIM ** -0.5
N_BUCKETS = 32
MAX_DISTANCE = 128
D_INNER = 2 * D_MODEL
SSM_HEADDIM = 64
SSM_HEADS = D_INNER // SSM_HEADDIM
SSM_GROUPS = 4
D_STATE = 128
SSM_CONV = 4
SSM_CHUNK = 128
CONV_DIM = D_INNER + 2 * SSM_GROUPS * D_STATE
D_FF = 2816
FFN_CONV = 3
EPS = 1e-6
NEG_INF = -1e30
DEPTH = 2

Q_COLS = N_HEADS * HEAD_DIM
KV_COLS = 6 * N_KV * HEAD_DIM
NSA_GATE_COLS = 3 * N_HEADS
OFF_KV = Q_COLS
OFF_G = OFF_KV + KV_COLS
OFF_Z = OFF_G + NSA_GATE_COLS
OFF_XBC = OFF_Z + D_INNER
OFF_DT = OFF_XBC + CONV_DIM
OFF_MG = OFF_DT + SSM_HEADS
P_IN = OFF_MG + 2 * D_MODEL

LOG2E = math.log2(math.e)

LANES = 128
SUBLANES = 8
BF16_ROWS = 16
TQ = 128
WIDTH = Q_PER_KV * TQ
FAR_TILES = 4
WIN_TILES = WINDOW // TQ + 1
AUG_ROW0 = HEAD_DIM
VMEM_LIMIT = 48 * 1024 * 1024

BF16 = jnp.bfloat16
F32 = jnp.float32


def _cparams(sem):
    return pltpu.CompilerParams(dimension_semantics=sem, vmem_limit_bytes=VMEM_LIMIT)


def _silu(x):
    return x * jax.nn.sigmoid(x)


def _dot(a, b):
    return jnp.dot(a, b, preferred_element_type=F32)


def _dot_nt(a, b):
    return lax.dot_general(a, b, (((1,), (1,)), ((), ())), preferred_element_type=F32)


def _dot_tn(a, b):
    return lax.dot_general(a, b, (((0,), (0,)), ((), ())), preferred_element_type=F32)


def _split3(x):
    hi = x.astype(BF16)
    r1 = x - hi.astype(F32)
    mid = r1.astype(BF16)
    lo = (r1 - mid.astype(F32)).astype(BF16)
    return hi, mid, lo


def _mm_body(a_ref, w_ref, *refs, epilogue, n_extra):
    acc = _dot(a_ref[...], w_ref[...])
    extras = [r[...] for r in refs[:n_extra]]
    o_ref = refs[n_extra]
    o_ref[...] = epilogue(acc, *extras).astype(o_ref.dtype)


def _mm(a, w, *, tm, tn, name, out_dtype=F32, epilogue=None, extras=(), rows_per_batch=None):
    M, K = a.shape
    N = w.shape[1]
    assert M % tm == 0 and N % tn == 0, (M, N, tm, tn)
    in_specs = [pl.BlockSpec((tm, K), lambda i, j: (i, 0)),
                pl.BlockSpec((K, tn), lambda i, j: (0, j))]
    ops = [a, w]
    for kind, arr in extras:
        if kind == 'col':
            in_specs.append(pl.BlockSpec((1, tn), lambda i, j: (0, j)))
        elif kind == 'full':
            in_specs.append(pl.BlockSpec((tm, tn), lambda i, j: (i, j)))
        else:
            assert kind == 'batch' and rows_per_batch % tm == 0
            tpb = rows_per_batch // tm
            in_specs.append(pl.BlockSpec((None, 1, tn), lambda i, j, tpb=tpb: (i // tpb, 0, j)))
        ops.append(arr)
    if epilogue is None:
        epilogue = lambda acc: acc
    return pl.pallas_call(
        functools.partial(_mm_body, epilogue=epilogue, n_extra=len(extras)),
        grid=(M // tm, N // tn),
        in_specs=in_specs,
        out_specs=pl.BlockSpec((tm, tn), lambda i, j: (i, j)),
        out_shape=jax.ShapeDtypeStruct((M, N), out_dtype),
        compiler_params=_cparams(("parallel", "parallel")),
        name=name,
    )(*ops)


def _mm_t_body(wt_ref, a_ref, o_ref, *, scale):
    o_ref[...] = (_dot_nt(wt_ref[...], a_ref[...]) * scale).astype(o_ref.dtype)


def _mm_t(a, wt, *, tm, tn, name, out_dtype=F32, scale=1.0):
    M, K = a.shape
    N = wt.shape[0]
    assert M % tm == 0 and N % tn == 0, (M, N, tm, tn)
    return pl.pallas_call(
        functools.partial(_mm_t_body, scale=scale),
        grid=(M // tm, N // tn),
        in_specs=[pl.BlockSpec((tn, K), lambda i, j: (j, 0)), pl.BlockSpec((tm, K), lambda i, j: (i, 0))],
        out_specs=pl.BlockSpec((tn, tm), lambda i, j: (j, i)),
        out_shape=jax.ShapeDtypeStruct((N, M), out_dtype),
        compiler_params=_cparams(("parallel", "parallel")),
        name=name,
    )(wt, a)


def _norm_mod_body(x_ref, g_ref, sc_ref, sh_ref, o_ref):
    x = x_ref[...]
    y = x * lax.rsqrt(jnp.mean(x * x, axis=-1, keepdims=True) + EPS) * g_ref[...]
    o_ref[...] = (y * (1.0 + sc_ref[...]) + sh_ref[...]).astype(o_ref.dtype)


def _norm_mod(x, g, sc, sh, *, tm, rows_per_batch, name):
    M, D = x.shape
    if sc.ndim == 3:
        tpb = rows_per_batch // tm
        mod_spec = pl.BlockSpec((None, 1, D), lambda i: (i // tpb, 0, 0))
    else:
        mod_spec = pl.BlockSpec((tm, D), lambda i: (i, 0))
    return pl.pallas_call(
        _norm_mod_body,
        grid=(M // tm,),
        in_specs=[pl.BlockSpec((tm, D), lambda i: (i, 0)), pl.BlockSpec((1, D), lambda i: (0, 0)),
                  mod_spec, mod_spec],
        out_specs=pl.BlockSpec((tm, D), lambda i: (i, 0)),
        out_shape=jax.ShapeDtypeStruct((M, D), BF16),
        compiler_params=_cparams(("parallel",)),
        name=name,
    )(x, g.reshape(1, D), sc, sh)


def _final_norm_body(x_ref, g_ref, o_ref):
    x = x_ref[...]
    o_ref[...] = x * lax.rsqrt(jnp.mean(x * x, axis=-1, keepdims=True) + EPS) * g_ref[...]


def _final_norm(x, g, *, tm, name):
    M, D = x.shape
    return pl.pallas_call(
        _final_norm_body,
        grid=(M // tm,),
        in_specs=[pl.BlockSpec((tm, D), lambda i: (i, 0)), pl.BlockSpec((1, D), lambda i: (0, 0))],
        out_specs=pl.BlockSpec((tm, D), lambda i: (i, 0)),
        out_shape=jax.ShapeDtypeStruct((M, D), F32),
        compiler_params=_cparams(("parallel",)),
        name=name,
    )(x, g.reshape(1, D))


def _merge_body(o_ref, y_ref, wa_ref, wb_ref, ga_ref, gb_ref, u_ref):
    pa = _dot(o_ref[...], wa_ref[...])
    pb = _dot(y_ref[...], wb_ref[...])
    u_ref[...] = (ga_ref[...] * pa + gb_ref[...] * pb).astype(u_ref.dtype)


def _merge(o_nsa, y, w_a, w_b, gates, *, tm, tn, name):
    M = o_nsa.shape[0]
    N = w_a.shape[1]
    nj = N // tn
    return pl.pallas_call(
        _merge_body,
        grid=(M // tm, nj),
        in_specs=[pl.BlockSpec((tm, o_nsa.shape[1]), lambda i, j: (i, 0)),
                  pl.BlockSpec((tm, y.shape[1]), lambda i, j: (i, 0)),
                  pl.BlockSpec((w_a.shape[0], tn), lambda i, j: (0, j)),
                  pl.BlockSpec((w_b.shape[0], tn), lambda i, j: (0, j)),
                  pl.BlockSpec((tm, tn), lambda i, j: (i, j)),
                  pl.BlockSpec((tm, tn), lambda i, j, nj=nj: (i, j + nj))],
        out_specs=pl.BlockSpec((tm, tn), lambda i, j: (i, j)),
        out_shape=jax.ShapeDtypeStruct((M, N), BF16),
        compiler_params=_cparams(("parallel", "parallel")),
        name=name,
    )(o_nsa, y, w_a, w_b, gates, gates)


def _ffn_up_body(h_ref, wg_ref, wu_ref, cw_ref, cb_ref, o_ref, pad_ref, *, tm, tiles_per_batch):
    i = pl.program_id(1)
    h = h_ref[...]
    gate = _dot(h, wg_ref[...])
    up = _dot(h, wu_ref[...])

    @pl.when(i % tiles_per_batch == 0)
    def _():
        pad_ref[0:SUBLANES, :] = jnp.zeros((SUBLANES, pad_ref.shape[1]), F32)

    pad_ref[SUBLANES:SUBLANES + tm, :] = gate
    y = cb_ref[...]
    for k in range(FFN_CONV):
        off = SUBLANES - (FFN_CONV - 1) + k
        y = y + cw_ref[k:k + 1, :] * pad_ref[off:off + tm, :]
    pad_ref[0:SUBLANES, :] = pad_ref[tm:tm + SUBLANES, :]
    o_ref[...] = (_silu(y) * up).astype(o_ref.dtype)


def _ffn_up(h, w_gu, conv_w, conv_b, *, tm, tn, rows_per_batch, name):
    M, K = h.shape
    nj = D_FF // tn
    return pl.pallas_call(
        functools.partial(_ffn_up_body, tm=tm, tiles_per_batch=rows_per_batch // tm),
        grid=(nj, M // tm),
        in_specs=[pl.BlockSpec((tm, K), lambda j, i: (i, 0)),
                  pl.BlockSpec((K, tn), lambda j, i: (0, j)),
                  pl.BlockSpec((K, tn), lambda j, i, nj=nj: (0, j + nj)),
                  pl.BlockSpec((FFN_CONV, tn), lambda j, i: (0, j)),
                  pl.BlockSpec((1, tn), lambda j, i: (0, j))],
        out_specs=pl.BlockSpec((tm, tn), lambda j, i: (i, j)),
        out_shape=jax.ShapeDtypeStruct((M, D_FF), BF16),
        scratch_shapes=[pltpu.VMEM((tm + SUBLANES, tn), F32)],
        compiler_params=_cparams(("arbitrary", "arbitrary")),
        name=name,
    )(h, w_gu, w_gu, conv_w, conv_b.reshape(1, D_FF))


def _compress_body(x_ref, pe_ref, w1_ref, w2_ref, o_ref):
    xb = (x_ref[...] + pe_ref[...]).astype(BF16)
    hid = _silu(_dot(xb, w1_ref[...]))
    o_ref[...] = _dot(hid.astype(BF16), w2_ref[...])


def _compress(x, pe, w1, w2, *, tm, name):
    rows, kdim = x.shape
    return pl.pallas_call(
        _compress_body,
        grid=(rows // tm,),
        in_specs=[pl.BlockSpec((tm, kdim), lambda i: (i, 0)), pl.BlockSpec((1, kdim), lambda i: (0, 0)),
                  pl.BlockSpec((kdim, CMP_HID), lambda i: (0, 0)),
                  pl.BlockSpec((CMP_HID, HEAD_DIM), lambda i: (0, 0))],
        out_specs=pl.BlockSpec((tm, HEAD_DIM), lambda i: (i, 0)),
        out_shape=jax.ShapeDtypeStruct((rows, HEAD_DIM), F32),
        compiler_params=_cparams(("parallel",)),
        name=name,
    )(x, pe.reshape(1, kdim), w1, w2)


def _attend(s, vt):
    m = jnp.max(s, axis=0, keepdims=True)
    pv = _dot(vt, jnp.exp2(s - m).astype(BF16))
    return m, pv[HEAD_DIM:HEAD_DIM + 1, :], pv[0:HEAD_DIM, :]


def _lane_concat(tiles3):
    return jnp.concatenate([tiles3[t] for t in range(tiles3.shape[0])], axis=1)


SEL_TILES = 4

def _nsa_select_body(q_ref, kc_ref, vct_ref, band_ref, ocmp_ref, mask_ref, *, n_blocks, n_cmp, top_n):
    step = pl.program_id(2)
    half = n_cmp // 2
    row = lax.broadcasted_iota(jnp.int32, (n_cmp, 16), 0)
    cidx = jnp.where(row < half, 2 * row, 2 * (row - half) + 1)
    col16 = lax.broadcasted_iota(jnp.int32, (n_cmp, 16), 1)
    b_hi, b_mid, b_lo = _split3(band_ref[...])
    imps = []
    for t in range(SEL_TILES):
        qi = SEL_TILES * step + t
        crel = cidx - (TQ // L_CMP) * qi
        cat = jnp.where(crel <= -5, 8, jnp.where(crel >= 4, 9, crel + 4))
        place = (cat == col16).astype(BF16)
        bias_c = _dot(place, b_hi) + _dot(place, b_mid) + _dot(place, b_lo)
        s = _dot(kc_ref[...], q_ref[t]) + bias_c
        m = jnp.max(s, axis=0, keepdims=True)
        e = jnp.exp2(s - m)
        denom = jnp.sum(e, axis=0, keepdims=True)
        anyvalid = (m > 0.5 * NEG_INF).astype(F32)
        p = e * (anyvalid / denom)
        ocmp_ref[t] = _dot(vct_ref[...], p.astype(BF16))
        psum = p[:, 0:TQ]
        for r in range(1, Q_PER_KV):
            psum = psum + p[:, r * TQ:(r + 1) * TQ]
        imps.append(psum[0:half, :] + psum[half:n_cmp, :])
    imp = jnp.concatenate(imps, axis=1)

    wide = SEL_TILES * TQ
    blk = lax.broadcasted_iota(jnp.int32, (n_blocks, wide), 0)
    blk_f = blk.astype(F32)
    qpos = step * wide + lax.broadcasted_iota(jnp.int32, (n_blocks, wide), 1)
    forced = (blk == 0) | (blk == lax.shift_right_logical(qpos, 6))
    valid = blk * L_SEL <= qpos
    score = jnp.where(forced, 1e30, jnp.where(valid, imp, -1.0))
    picked = jnp.zeros((n_blocks, wide), jnp.bool_)
    for _ in range(top_n):
        mx = jnp.max(score, axis=0, keepdims=True)
        first = jnp.min(jnp.where(score == mx, blk_f, float(n_blocks)), axis=0, keepdims=True)
        hit = blk_f == first
        picked = picked | hit
        score = jnp.where(hit, -2.0, score)
    seladd = jnp.where(picked, 0.0, NEG_INF)
    for t in range(SEL_TILES):
        mask_ref[t] = seladd[:, t * TQ:(t + 1) * TQ]


def _nsa_select(q2, kc, vct, band, *, name):
    B, G, nq, D, _ = q2.shape
    n_cmp = kc.shape[2]
    n_blocks = n_cmp // (L_SEL // L_CMP)
    assert nq % SEL_TILES == 0
    tiles = lambda shape: pl.BlockSpec((None, None, SEL_TILES) + shape, lambda b, g, i: (b, g, i, 0, 0))
    return pl.pallas_call(
        functools.partial(_nsa_select_body, n_blocks=n_blocks, n_cmp=n_cmp, top_n=min(TOP_N, n_blocks)),
        grid=(B, G, nq // SEL_TILES),
        in_specs=[tiles((D, WIDTH)),
                  pl.BlockSpec((None, None, n_cmp, D), lambda b, g, i: (b, g, 0, 0)),
                  pl.BlockSpec((None, None, D, n_cmp), lambda b, g, i: (b, g, 0, 0)),
                  pl.BlockSpec((None, 16, WIDTH), lambda b, g, i: (g, 0, 0))],
        out_specs=[tiles((D, WIDTH)), tiles((n_blocks, TQ))],
        out_shape=[jax.ShapeDtypeStruct((B, G, nq, D, WIDTH), F32),
                   jax.ShapeDtypeStruct((B, G, nq, n_blocks, TQ), F32)],
        compiler_params=_cparams(("parallel", "parallel", "parallel")),
        name=name,
    )(q2, kc, vct, band)


def _nsa_prompt_body(q_ref, ksel_ref, vsel_ref, kwin_ref, vwin_ref, ocmp_ref, mask_ref, gate_ref,
                     tzw_ref, tzn_ref, caug_ref, o_ref,
                     qaug_ref, m_ref, l_ref, acc_ref, sel0_ref, selfar_ref, sa_scr, sb_scr, *, n_blocks):
    qi = pl.program_id(2)
    n_grp_rows = n_blocks // SUBLANES

    @pl.when(qi == 0)
    def _():
        selfar_ref[:, SUBLANES:2 * SUBLANES, :] = jnp.broadcast_to(
            caug_ref[...][None], (selfar_ref.shape[0], SUBLANES, WIDTH))
        selfar_ref[n_grp_rows:, 0:SUBLANES, :] = jnp.full(
            (selfar_ref.shape[0] - n_grp_rows, SUBLANES, WIDTH), NEG_INF, F32)
        sel0_ref[0:SUBLANES, :] = jnp.zeros((SUBLANES, WIDTH), F32)

    qaug_ref[0:HEAD_DIM, :] = q_ref[...]
    qaug_ref[HEAD_DIM:, :] = jnp.zeros((qaug_ref.shape[0] - HEAD_DIM, WIDTH), BF16)

    kw = kwin_ref[pl.ds(qi, WIN_TILES)].reshape(WIN_TILES * TQ, LANES)
    m_w, l_w, a_w = _attend(_dot(kw, qaug_ref[...]) + tzw_ref[...], _lane_concat(vwin_ref[pl.ds(qi, WIN_TILES)]))
    o_win = a_w * (1.0 / l_w)

    seladd = mask_ref[...]
    blk = lax.broadcasted_iota(jnp.int32, (n_blocks, TQ), 0)
    selfar = jnp.where(blk < 2 * (qi - 1), seladd, NEG_INF)
    sel0_ref[SUBLANES:, :] = jnp.concatenate([seladd] * Q_PER_KV, axis=1)
    selfar_ref[0:n_grp_rows, 0:SUBLANES, :] = jnp.concatenate([selfar] * Q_PER_KV, axis=1).reshape(
        n_grp_rows, SUBLANES, WIDTH)

    near_mask = jnp.concatenate(
        [jnp.broadcast_to(sel0_ref[pl.ds(SUBLANES + 2 * (qi - 1) + i, 1), :], (L_SEL, WIDTH)) for i in range(4)],
        axis=0)
    kn = ksel_ref[pl.ds(qi, 2)].reshape(2 * TQ, LANES)
    m_n, l_n, a_n = _attend(_dot(kn, qaug_ref[...]) + tzn_ref[...] + near_mask, _lane_concat(vsel_ref[pl.ds(qi, 2)]))
    m_ref[...] = m_n
    l_ref[...] = l_n
    acc_ref[...] = a_n

    def far_scores(gi):
        qaug_ref[AUG_ROW0:AUG_ROW0 + BF16_ROWS, :] = selfar_ref[gi].astype(BF16)
        kg = ksel_ref[pl.ds(1 + FAR_TILES * gi, FAR_TILES)].reshape(FAR_TILES * TQ, LANES)
        return _dot(kg, qaug_ref[...])

    def far_update(s, gi):
        m_g, l_g, a_g = _attend(s, _lane_concat(vsel_ref[pl.ds(1 + FAR_TILES * gi, FAR_TILES)]))
        m_old = m_ref[...]
        m_new = jnp.maximum(m_old, m_g)
        a_old = jnp.exp2(m_old - m_new)
        a_grp = jnp.exp2(m_g - m_new)
        l_ref[...] = l_ref[...] * a_old + l_g * a_grp
        acc_ref[...] = acc_ref[...] * a_old + a_g * a_grp
        m_ref[...] = m_new

    sa_scr[...] = far_scores(0)

    def far_body(j, carry):
        sb_scr[...] = far_scores(2 * j + 1)
        far_update(sa_scr[...], 2 * j)
        sa_scr[...] = far_scores(2 * j + 2)
        far_update(sb_scr[...], 2 * j + 1)
        return carry

    n_far = jnp.maximum(qi - 1, 0)
    n_groups = lax.shift_right_logical(n_far + (FAR_TILES - 1), 2)
    lax.fori_loop(0, lax.shift_right_logical(n_groups + 1, 1), far_body, 0)
    o_sel = acc_ref[...] * (1.0 / l_ref[...])

    g = jax.nn.sigmoid(gate_ref[...])
    o = g[0:1, :] * ocmp_ref[...] + g[1:2, :] * o_sel + g[2:3, :] * o_win
    for r in range(Q_PER_KV):
        o_ref[r * HEAD_DIM:(r + 1) * HEAD_DIM, :] = o[:, r * TQ:(r + 1) * TQ].astype(o_ref.dtype)


def _nsa_prompt(q2, ksel, vsel, kwin, vwin, kc, vct, gates, tables, *, name):
    tzw, tzn, band, caug = tables
    B, G, nq, D, _ = q2.shape
    n_blocks = kc.shape[2] // (L_SEL // L_CMP)
    assert n_blocks % SUBLANES == 0 and nq % FAR_TILES == 0
    T = nq * TQ
    o_cmp, masks = _nsa_select(q2, kc, vct, band, name=name + "_select")
    whole = lambda arr: pl.BlockSpec((None, None) + arr.shape[2:], lambda b, g, i: (b, g, 0, 0, 0))
    tile = lambda shape: pl.BlockSpec((None, None, None) + shape, lambda b, g, i: (b, g, i, 0, 0))
    return pl.pallas_call(
        functools.partial(_nsa_prompt_body, n_blocks=n_blocks),
        grid=(B, G, nq),
        in_specs=[tile((D, WIDTH)),
                  whole(ksel), whole(vsel), whole(kwin), whole(vwin),
                  tile((D, WIDTH)), tile((n_blocks, TQ)), tile((3, WIDTH)),
                  pl.BlockSpec((None, None, WIN_TILES * TQ, WIDTH),
                               lambda b, g, i: (g, jnp.minimum(i, WIN_TILES - 1), 0, 0)),
                  pl.BlockSpec((None, None, 2 * TQ, WIDTH), lambda b, g, i: (g, jnp.minimum(i, 1), 0, 0)),
                  pl.BlockSpec((None, SUBLANES, WIDTH), lambda b, g, i: (g, 0, 0))],
        out_specs=pl.BlockSpec((None, Q_PER_KV * D, TQ), lambda b, g, i: (b, g, i)),
        out_shape=jax.ShapeDtypeStruct((B, G * Q_PER_KV * D, T), BF16),
        scratch_shapes=[pltpu.VMEM((LANES, WIDTH), BF16),
                        pltpu.VMEM((1, WIDTH), F32), pltpu.VMEM((1, WIDTH), F32),
                        pltpu.VMEM((D, WIDTH), F32),
                        pltpu.VMEM((n_blocks + SUBLANES, WIDTH), F32),
                        pltpu.VMEM((nq // FAR_TILES + 2, 2 * SUBLANES, WIDTH), F32),
                        pltpu.VMEM((FAR_TILES * TQ, WIDTH), F32), pltpu.VMEM((FAR_TILES * TQ, WIDTH), F32)],
        compiler_params=_cparams(("arbitrary", "arbitrary", "arbitrary")),
        name=name,
    )(q2, ksel, vsel, kwin, vwin, o_cmp, masks, gates, tzw, tzn, caug)


SSD_QUAD = 4

def _ssd_body(xbc_ref, dt_ref, z_ref, cw_ref, cb_ref, dtb_ref, alog_ref, dskip_ref, ng_ref,
              exp_h_ref, exp_p_ref, y_ref, st_ref, pad_ref, x_scr, colh_scr, colp_scr, dtp_scr, y_scr):
    c = pl.program_id(1)
    L = SSM_CHUNK
    P = SSM_HEADDIM
    heads_per_group = SSM_HEADS // SSM_GROUPS

    @pl.when(c == 0)
    def _():
        pad_ref[0:SUBLANES, :] = jnp.zeros((SUBLANES, CONV_DIM), F32)
        st_ref[...] = jnp.zeros(st_ref.shape, F32)

    pad_ref[SUBLANES:SUBLANES + L, :] = xbc_ref[...]
    conv = cb_ref[...]
    for k in range(SSM_CONV):
        off = SUBLANES - (SSM_CONV - 1) + k
        conv = conv + cw_ref[k:k + 1, :] * pad_ref[off:off + L, :]
    pad_ref[0:SUBLANES, :] = pad_ref[L:L + SUBLANES, :]
    xbc = _silu(conv)
    x_scr[...] = xbc[:, 0:D_INNER]
    bmat = [xbc[:, D_INNER + g * D_STATE:D_INNER + (g + 1) * D_STATE].astype(BF16) for g in range(SSM_GROUPS)]
    cmat = [xbc[:, D_INNER + (SSM_GROUPS + g) * D_STATE:D_INNER + (SSM_GROUPS + g + 1) * D_STATE].astype(BF16)
            for g in range(SSM_GROUPS)]

    x = dt_ref[:, 0:SSM_HEADS] + dtb_ref[...]
    dt = jnp.maximum(x, 0.0) + jnp.log1p(jnp.exp(-jnp.abs(x)))
    a = dt * (-jnp.exp(alog_ref[...]))
    tri = lax.broadcasted_iota(jnp.int32, (L, L), 0) >= lax.broadcasted_iota(jnp.int32, (L, L), 1)
    tri_b = tri.astype(BF16)
    a_hi, a_mid, a_lo = _split3(a)
    a_cs = _dot(tri_b, a_hi) + _dot(tri_b, a_mid) + _dot(tri_b, a_lo)
    acs_parts = jnp.concatenate(_split3(a_cs), axis=1)
    dt_parts = jnp.concatenate(_split3(dt), axis=1)
    colh_scr[...] = _dot(acs_parts, exp_h_ref[...])
    colp_scr[...] = _dot(acs_parts, exp_p_ref[...])
    dtp_scr[...] = _dot(dt_parts, exp_p_ref[...])
    acs_t = a_cs.T
    dt_t = dt.T
    tri_q = jnp.concatenate([tri] * SSD_QUAD, axis=1)

    for qd in range(SSM_HEADS // SSD_QUAD):
        g = (qd * SSD_QUAD) // heads_per_group
        h0 = qd * SSD_QUAD
        b_g, c_g = bmat[g], cmat[g]
        cb = _dot_nt(c_g, b_g)
        col = colh_scr[:, h0 * LANES:(h0 + SSD_QUAD) * LANES]
        row = jnp.concatenate([acs_t[h0 + i:h0 + i + 1, :] for i in range(SSD_QUAD)], axis=1)
        dtrow = jnp.concatenate([dt_t[h0 + i:h0 + i + 1, :] for i in range(SSD_QUAD)], axis=1)
        decay = jnp.exp(jnp.where(tri_q, col - row, NEG_INF))
        w = (jnp.concatenate([cb] * SSD_QUAD, axis=1) * decay * dtrow).astype(BF16)
        xq = x_scr[:, h0 * P:(h0 + SSD_QUAD) * P]
        xq_b = xq.astype(BF16)
        y_diag = jnp.concatenate([_dot(w[:, i * LANES:(i + 1) * LANES], xq_b[:, i * P:(i + 1) * P])
                                  for i in range(SSD_QUAD)], axis=1)
        h_prev = st_ref[h0:h0 + SSD_QUAD].reshape(SSD_QUAD * P, D_STATE)
        colp = colp_scr[:, h0 * P:(h0 + SSD_QUAD) * P]
        y_off = _dot_nt(c_g, h_prev.astype(BF16)) * jnp.exp(colp)
        y_scr[:, h0 * P:(h0 + SSD_QUAD) * P] = y_diag + y_off
        decay_end = jnp.exp(colp[L - 1:L, :] - colp)
        xw = (decay_end * dtp_scr[:, h0 * P:(h0 + SSD_QUAD) * P]) * xq
        st = _dot_tn(xw.astype(BF16), b_g)
        chunk_decay = jnp.exp(col[L - 1:L, :])
        cd = jnp.concatenate([jnp.broadcast_to(chunk_decay[:, i * LANES:(i + 1) * LANES], (P, D_STATE))
                              for i in range(SSD_QUAD)], axis=0)
        st_ref[h0:h0 + SSD_QUAD] = (h_prev * cd + st).reshape(SSD_QUAD, P, D_STATE)

    group_w = D_INNER // SSM_GROUPS
    y = y_scr[...] + dskip_ref[...] * x_scr[...]
    y = y * _silu(z_ref[...])
    for g in range(SSM_GROUPS):
        yg = y[:, g * group_w:(g + 1) * group_w]
        inv = lax.rsqrt(jnp.mean(yg * yg, axis=-1, keepdims=True) + EPS)
        y_ref[:, g * group_w:(g + 1) * group_w] = (
            yg * inv * ng_ref[:, g * group_w:(g + 1) * group_w]).astype(y_ref.dtype)


def _ssd_prompt(xbc, dtg, z, conv_w, conv_b, dt_bias, a_log, d_skip, norm_g, *, batch, name):
    M = xbc.shape[0]
    L = SSM_CHUNK
    nc = M // batch // L
    heads = jnp.arange(SSM_HEADS)

    def expand(lanes):
        e = (heads[:, None, None] == heads[None, :, None])
        e = jnp.broadcast_to(e, (SSM_HEADS, SSM_HEADS, lanes)).reshape(SSM_HEADS, SSM_HEADS * lanes)
        return jnp.concatenate([e] * 3, axis=0).astype(BF16)

    row = lambda v: v.reshape(1, -1)
    const2 = lambda shape: pl.BlockSpec(shape, lambda b, c: (0, 0))
    return pl.pallas_call(
        _ssd_body,
        grid=(batch, nc),
        in_specs=[pl.BlockSpec((L, CONV_DIM), lambda b, c: (b * nc + c, 0)),
                  pl.BlockSpec((L, LANES), lambda b, c: (b * nc + c, 1)),
                  pl.BlockSpec((L, D_INNER), lambda b, c: (b * nc + c, 0)),
                  const2((SSM_CONV, CONV_DIM)), const2((1, CONV_DIM)),
                  const2((1, SSM_HEADS)), const2((1, SSM_HEADS)),
                  const2((1, D_INNER)), const2((1, D_INNER)),
                  const2((3 * SSM_HEADS, SSM_HEADS * LANES)), const2((3 * SSM_HEADS, D_INNER))],
        out_specs=[pl.BlockSpec((L, D_INNER), lambda b, c: (b * nc + c, 0)),
                   pl.BlockSpec((None, SSM_HEADS, SSM_HEADDIM, D_STATE), lambda b, c: (b, 0, 0, 0))],
        out_shape=[jax.ShapeDtypeStruct((M, D_INNER), BF16),
                   jax.ShapeDtypeStruct((batch, SSM_HEADS, SSM_HEADDIM, D_STATE), F32)],
        scratch_shapes=[pltpu.VMEM((L + SUBLANES, CONV_DIM), F32),
                        pltpu.VMEM((L, D_INNER), F32),
                        pltpu.VMEM((L, SSM_HEADS * LANES), F32),
                        pltpu.VMEM((L, D_INNER), F32),
                        pltpu.VMEM((L, D_INNER), F32),
                        pltpu.VMEM((L, D_INNER), F32)],
        compiler_params=_cparams(("arbitrary", "arbitrary")),
        name=name,
    )(xbc, dtg, z, conv_w, row(conv_b), row(dt_bias), row(a_log),
      row(jnp.repeat(d_skip, SSM_HEADDIM)), row(norm_g), expand(LANES), expand(SSM_HEADDIM))


def _t5_bucket(dist):
    n = jnp.maximum(dist, 0)
    max_exact = N_BUCKETS // 2
    nf = jnp.maximum(n, max_exact).astype(F32)
    large = max_exact + (jnp.log(nf / max_exact) / math.log(MAX_DISTANCE / max_exact)
                         * (N_BUCKETS - max_exact)).astype(jnp.int32)
    return jnp.where(n < max_exact, n, jnp.minimum(large, N_BUCKETS - 1))


def _bias_tables(rel_bias):
    by_dist = rel_bias[_t5_bucket(jnp.arange((WIN_TILES + 1) * TQ))]
    qry = jnp.arange(TQ)[None, :]
    delta = (jnp.arange(WIN_TILES - 1, -1, -1) * TQ)[:, None]
    dist = delta - (TQ - 1) + jnp.arange(2 * TQ - 1)[None, :]
    ok = (dist >= 0) & (dist <= WINDOW)
    vals = jnp.where(ok[..., None], by_dist[jnp.clip(dist, 0)], NEG_INF).transpose(0, 2, 1)
    hank = jnp.tile(vals, (1, 1, TQ + 1))[..., :TQ * 2 * TQ].reshape(WIN_TILES, N_HEADS, TQ, 2 * TQ)
    tz = hank[:, :, ::-1, :TQ]
    tz = tz.reshape(WIN_TILES, N_KV, Q_PER_KV, TQ, TQ).transpose(1, 0, 3, 2, 4)
    tz = tz.reshape(N_KV, WIN_TILES, TQ, WIDTH)
    tile = jnp.arange(WIN_TILES)[None, :, None, None]

    def variants(tiles, n_var):
        n_tiles = tiles.shape[1]
        v = jnp.arange(n_var)[:, None, None, None]
        masked = tile[:, :n_tiles] < (n_tiles - 1 - v)
        out = jnp.where(masked[None], NEG_INF, tiles[:, None])
        return out.reshape(N_KV, n_var, n_tiles * TQ, WIDTH)

    tzw = variants(tz, WIN_TILES)
    tzn = variants(tz[:, WIN_TILES - 2:], 2)
    far = rel_bias[N_BUCKETS - 1]
    rel = jnp.arange(8)[:, None] - 4
    cdist = qry - L_CMP * rel - (L_CMP - 1)
    band = jnp.where((cdist >= 0)[..., None], by_dist[jnp.clip(cdist, 0)], NEG_INF)
    far_rows = jnp.broadcast_to(far[None, None, :], (1, TQ, N_HEADS))
    band = jnp.concatenate([band, far_rows, jnp.full((1, TQ, N_HEADS), NEG_INF, F32),
                            jnp.zeros((6, TQ, N_HEADS), F32)], axis=0)
    band = band.reshape(16, TQ, N_KV, Q_PER_KV).transpose(2, 0, 3, 1).reshape(N_KV, 16, WIDTH)
    parts = jnp.stack([p.astype(F32) for p in _split3(far)] + [jnp.zeros_like(far)] * (SUBLANES - 3))
    caug = jnp.broadcast_to(parts.reshape(SUBLANES, N_KV, Q_PER_KV, 1), (SUBLANES, N_KV, Q_PER_KV, TQ))
    caug = caug.transpose(1, 0, 2, 3).reshape(N_KV, SUBLANES, WIDTH)
    return tzw, tzn, band, caug


def _layer_weights(l, w_ada, w_in, cmp_pe, cmp_w1, cmp_w2, w_a, w_b, w_o, w_gu, w_down):
    wi = w_in[l]
    small = jnp.zeros((D_MODEL, 2 * LANES), F32)
    small = small.at[:, 0:NSA_GATE_COLS].set(wi[:, OFF_G:OFF_Z])
    small = small.at[:, LANES:LANES + SSM_HEADS].set(wi[:, OFF_DT:OFF_MG])
    c = lambda w: w.astype(BF16)
    w1r = cmp_w1[l].reshape(2, L_CMP, HEAD_DIM, CMP_HID)
    w1_pair = jnp.zeros((2, L_CMP, 2 * HEAD_DIM, 2 * CMP_HID), F32)
    w1_pair = w1_pair.at[:, :, 0:HEAD_DIM, 0:CMP_HID].set(w1r).at[:, :, HEAD_DIM:, CMP_HID:].set(w1r)
    pe_pair = jnp.tile(cmp_pe[l], (1, 1, 2))
    return dict(ada=c(w_ada[l]), q=c(wi[:, 0:OFF_KV]), kv=c(wi[:, OFF_KV:OFF_G]), small=c(small),
                q_t=c(wi[:, 0:OFF_KV].T), kv_t=c(wi[:, OFF_KV:OFF_G].T),
                w1_pair=c(w1_pair), pe_pair=pe_pair,
                z=c(wi[:, OFF_Z:OFF_XBC]), xbc=c(wi[:, OFF_XBC:OFF_DT]), mg=c(wi[:, OFF_MG:P_IN]),
                cmp_w1=c(cmp_w1[l]), cmp_w2=c(cmp_w2[l]), a=c(w_a[l]), b=c(w_b[l]), o=c(w_o[l]),
                gu=c(w_gu[l]), down=c(w_down[l]))


def _prompt_layer(x, mod, l, Wl, P, tables, batch, T):
    sh1, sc1, g1, sh2, sc2, g2 = [m.reshape(batch, 1, D_MODEL) for m in jnp.split(mod, 6, axis=-1)]
    tm = 1024 if T % 1024 == 0 else 512
    h = _norm_mod(x, P['norm1_g'][l], sc1, sh1, tm=512, rows_per_batch=T, name=f"norm1_p{l}")
    q = _mm_t(h, Wl['q_t'], tm=tm, tn=512, name=f"proj_q_p{l}", out_dtype=BF16, scale=SCALE * LOG2E).T
    kv = _mm_t(h, Wl['kv_t'], tm=tm, tn=512, name=f"proj_kv_p{l}").T
    small = _mm(h, Wl['small'], tm=tm, tn=2 * LANES, name=f"proj_small_p{l}")
    z = _mm(h, Wl['z'], tm=tm, tn=512, name=f"proj_z_p{l}")
    xbc = _mm(h, Wl['xbc'], tm=tm, tn=512, name=f"proj_xbc_p{l}")
    gates = _mm(h, Wl['mg'], tm=tm, tn=512, name=f"proj_mg_p{l}", epilogue=jax.nn.sigmoid)

    o_nsa = _prompt_attention(q, kv, small, l, Wl, P, tables, batch, T)
    kv6 = kv.reshape(batch, T, 6, N_KV, HEAD_DIM)

    y, ssm_state = _ssd_prompt(xbc, small, z, P['ssm_conv_w'][l], P['ssm_conv_b'][l], P['dt_bias'][l],
                               P['a_log'][l], P['d_skip'][l], P['ssm_norm_g'][l], batch=batch, name=f"ssd_p{l}")

    u = _merge(o_nsa, y, Wl['a'], Wl['b'], gates, tm=tm, tn=512, name=f"merge_p{l}")
    x = _mm(u, Wl['o'], tm=tm, tn=512, name=f"proj_o_p{l}", rows_per_batch=T,
            epilogue=lambda acc, xr, gr: xr + gr * acc, extras=[('full', x), ('batch', g1)])

    h2 = _norm_mod(x, P['norm2_g'][l], sc2, sh2, tm=512, rows_per_batch=T, name=f"norm2_p{l}")
    act = _ffn_up(h2, Wl['gu'], P['ffn_conv_w'][l], P['ffn_conv_b'][l], tm=512, tn=D_FF // 2,
                  rows_per_batch=T, name=f"ffn_up_p{l}")
    h2_last = h2.reshape(batch, T, D_MODEL)[:, T - SUBLANES:].reshape(batch * SUBLANES, D_MODEL)
    gate_last = _mm(h2_last, Wl['gu'][:, 0:D_FF], tm=batch * SUBLANES, tn=D_FF // 2, name=f"ffn_state_p{l}")
    ffn_state = gate_last.reshape(batch, SUBLANES, D_FF)[:, SUBLANES - (FFN_CONV - 1):]
    x = _mm(act, Wl['down'], tm=tm, tn=512, name=f"ffn_down_p{l}", rows_per_batch=T,
            epilogue=lambda acc, xr, gr: xr + gr * acc, extras=[('full', x), ('batch', g2)])

    kv_rows = kv6[:, :, :4]
    win_state = kv6[:, T - min(WINDOW, T):, 4:]
    conv_state = xbc.reshape(batch, T, CONV_DIM)[:, T - (SSM_CONV - 1):]
    return x, (kv_rows, win_state, conv_state, ssm_state, ffn_state)


def _prompt_attention(q, kv, small, l, Wl, P, tables, batch, T):
    nq = T // TQ
    kv6 = kv.reshape(batch, T, 6, N_KV, HEAD_DIM)
    n_cmp = T // L_CMP

    def cmp_in(slot):
        xb = kv6[:, :, slot].reshape(batch, n_cmp, L_CMP, N_KV, HEAD_DIM).transpose(0, 3, 1, 2, 4)
        return xb.reshape(batch * N_KV * n_cmp, L_CMP * HEAD_DIM)

    ctm = min(512, batch * N_KV * n_cmp)
    kc = _compress(cmp_in(0), P['cmp_pe'][l, 0], Wl['cmp_w1'][0], Wl['cmp_w2'][0], tm=ctm, name=f"cmp_k_p{l}")
    vc = _compress(cmp_in(1), P['cmp_pe'][l, 1], Wl['cmp_w1'][1], Wl['cmp_w2'][1], tm=ctm, name=f"cmp_v_p{l}")

    def even_odd(c):
        c = c.reshape(batch, N_KV, n_cmp // 2, 2, HEAD_DIM).transpose(0, 1, 3, 2, 4)
        return c.reshape(batch, N_KV, n_cmp, HEAD_DIM)

    kc = even_odd(kc.reshape(batch, N_KV, n_cmp, HEAD_DIM)).astype(BF16)
    vct = even_odd(vc.reshape(batch, N_KV, n_cmp, HEAD_DIM)).astype(BF16).transpose(0, 1, 3, 2)

    def k_tiles(slot, front, back, aug):
        k = kv6[:, :, slot].astype(BF16).reshape(batch, nq, TQ, N_KV, HEAD_DIM).transpose(0, 3, 1, 2, 4)
        k = jnp.pad(k, ((0, 0), (0, 0), (front, back), (0, 0), (0, LANES - HEAD_DIM)))
        if aug:
            kt = jnp.arange(front + nq + back) - front
            blk_col = AUG_ROW0 + 2 * (kt % FAR_TILES)[:, None] + (jnp.arange(TQ) // L_SEL)[None, :]
            lane = jnp.arange(LANES)[None, None, :]
            ones = (lane == blk_col[..., None]) | ((lane >= AUG_ROW0 + SUBLANES) & (lane < AUG_ROW0 + SUBLANES + 3))
            k = jnp.where(ones[None, None], jnp.ones((), BF16), k)
        return k

    def vt_tiles(slot, front, back):
        v = kv6[:, :, slot].astype(BF16).reshape(batch, nq, TQ, N_KV, HEAD_DIM).transpose(0, 3, 1, 4, 2)
        v = jnp.pad(v, ((0, 0), (0, 0), (front, back), (0, BF16_ROWS), (0, 0)))
        return v.at[:, :, :, HEAD_DIM, :].set(1.0)

    q2 = q.reshape(batch, nq, TQ, N_KV, Q_PER_KV, HEAD_DIM).transpose(0, 3, 1, 5, 4, 2)
    q2 = q2.reshape(batch, N_KV, nq, HEAD_DIM, WIDTH)
    ng = small[:, 0:NSA_GATE_COLS].reshape(batch, nq, TQ, 3, N_KV, Q_PER_KV).transpose(0, 4, 1, 3, 5, 2)
    ng = ng.reshape(batch, N_KV, nq, 3, WIDTH)
    o_t = _nsa_prompt(q2, k_tiles(2, 1, 2 * FAR_TILES, True), vt_tiles(3, 1, 2 * FAR_TILES),
                      k_tiles(4, WIN_TILES - 1, 0, False), vt_tiles(5, WIN_TILES - 1, 0),
                      kc, vct, ng, tables, name=f"nsa_p{l}")
    return o_t.transpose(0, 2, 1).reshape(batch * T, Q_COLS)


def _pool_compress_body(x_ref, pe_ref, w1_ref, w2_ref, o_ref, rows_scr, *, pages, page):
    rows = pages * page // L_CMP
    for s in range(2):
        outs = []
        for pair in range(N_KV // 2):
            for p in range(pages):
                rows_scr[p * page:(p + 1) * page, :] = x_ref[p, s, 2 * pair:2 * pair + 2].reshape(
                    2 * HEAD_DIM, page).T
            xs = [(rows_scr[pl.ds(l, rows, stride=L_CMP), :] + pe_ref[s, l:l + 1, :]).astype(BF16)
                  for l in range(L_CMP)]
            w1 = w1_ref[s].reshape(L_CMP * 2 * HEAD_DIM, 2 * CMP_HID)
            hid = _silu(_dot(jnp.concatenate(xs, axis=1), w1)).astype(BF16)
            for gg in range(2):
                outs.append(_dot(hid[:, gg * CMP_HID:(gg + 1) * CMP_HID], w2_ref[s]))
        o_ref[:, s * N_KV * HEAD_DIM:(s + 1) * N_KV * HEAD_DIM] = jnp.concatenate(outs, axis=1)


def _pool_compress(cache_t, l, pe2, w1pair, w2, *, pages_per_step, name):
    n_pool, page = cache_t.shape[1], cache_t.shape[5]
    rows_out = pages_per_step * page // L_CMP
    half = 2 * N_KV * HEAD_DIM
    return pl.pallas_call(
        functools.partial(_pool_compress_body, pages=pages_per_step, page=page),
        grid=(n_pool // pages_per_step,),
        in_specs=[pl.BlockSpec((None, pages_per_step, 2, N_KV, HEAD_DIM, page), lambda i: (l, i, 0, 0, 0, 0)),
                  pl.BlockSpec(pe2.shape, lambda i: (0, 0, 0)),
                  pl.BlockSpec(w1pair.shape, lambda i: (0, 0, 0, 0)),
                  pl.BlockSpec(w2.shape, lambda i: (0, 0, 0))],
        out_specs=pl.BlockSpec((rows_out, half), lambda i: (i, 0)),
        out_shape=jax.ShapeDtypeStruct((n_pool * page // L_CMP, half), F32),
        scratch_shapes=[pltpu.VMEM((pages_per_step * page, 2 * HEAD_DIM), F32)],
        compiler_params=_cparams(("parallel",)),
        name=name,
    )(cache_t, pe2, w1pair, w2)


def _softmax_lanes(s):
    m = jnp.max(s, axis=1, keepdims=True)
    e = jnp.exp(s - m)
    return m, e, jnp.sum(e, axis=1, keepdims=True)


def _sum3(lhs_bf16, x):
    hi, mid, lo = _split3(x)
    return _dot(lhs_bf16, hi) + _dot(lhs_bf16, mid) + _dot(lhs_bf16, lo)


def _nsa_decode_body(pt_ref, q_ref, gate_ref, kvnew_ref, kcn_ref, wincol_ref, *refs,
                     n_pages, n_blocks, top_n, win_len, has_prev):
    sel_pages = refs[0:n_pages]
    kc_pages = refs[n_pages:2 * n_pages]
    rest = refs[2 * n_pages:]
    win_ref, bsel_ref, bwin_ref, bcmp_ref, e_ref, pair_ref, rsum_ref, rexp_ref = rest[0:8]
    o_ref, wout_ref, kc_scr, s_scr = rest[8 + int(has_prev):]
    gd = N_KV * HEAD_DIM

    last = lax.broadcasted_iota(jnp.int32, (gd, win_len), 1) == win_len - 1
    for s in range(2):
        shifted = pltpu.roll(win_ref[s].reshape(gd, win_len), win_len - 1, axis=1)
        wout_ref[s] = jnp.where(last, wincol_ref[s * gd:(s + 1) * gd, :], shifted).reshape(
            N_KV, HEAD_DIM, win_len)
    page = sel_pages[0].shape[3]
    blocks_per_page = kc_pages[0].shape[0]
    n_past_blocks = blocks_per_page * n_pages

    @pl.when(pl.program_id(0) == 0)
    def _():
        kc_scr[...] = jnp.zeros(kc_scr.shape, F32)

    q = q_ref[...]
    q_f = q.astype(F32)
    lane0 = lax.broadcasted_iota(jnp.int32, (N_HEADS, LANES), 1) == 0
    rb = lambda v: v.astype(BF16).astype(F32)

    def new_scores(slot):
        k_new = rb(kvnew_ref[:, slot * gd:(slot + 1) * gd])
        return jnp.where(lane0, jnp.sum(q_f * k_new, axis=1, keepdims=True), 0.0)

    def new_value(e_tile, slot):
        return rb(e_tile[:, 0:1]) * rb(kvnew_ref[:, slot * gd:(slot + 1) * gd])

    for p in range(n_pages):
        kc_scr[p * blocks_per_page:(p + 1) * blocks_per_page, :] = kc_pages[p][...]
    kc_scr[n_past_blocks:n_past_blocks + SUBLANES, :] = kcn_ref[...]
    kcv = kc_scr[...]
    m_c, e_c, den_c = _softmax_lanes(_dot_nt(q, kcv[:, 0:gd].astype(BF16)) + bcmp_ref[...])
    p_c = e_c * ((m_c > 0.5 * NEG_INF).astype(F32) / den_c)
    o_cmp = _dot(p_c.astype(BF16), kcv[:, gd:2 * gd].astype(BF16))
    imp = _sum3(rsum_ref[...], _sum3_rhs(p_c, pair_ref[...]))

    lane_i = lax.broadcasted_iota(jnp.int32, (SUBLANES, LANES), 1)
    forced = (lane_i == 0) | (lane_i == n_blocks - 1)
    score = jnp.where(lane_i < n_blocks, jnp.where(forced, 1e30, imp), -1.0)
    score_t = score.T
    ii = lax.broadcasted_iota(jnp.int32, (LANES, LANES), 0)
    jj = lax.broadcasted_iota(jnp.int32, (LANES, LANES), 1)
    sel_rows = []
    for g in range(N_KV):
        col = jnp.broadcast_to(score_t[:, g:g + 1], (LANES, LANES))
        rowv = jnp.broadcast_to(score[g:g + 1, :], (LANES, LANES))
        ahead = (col > rowv) | ((col == rowv) & (ii < jj))
        rank = jnp.sum(ahead.astype(F32), axis=0, keepdims=True)
        sel_rows.append((rank < float(top_n)).astype(F32))
    sel = jnp.concatenate(sel_rows + [jnp.zeros((SUBLANES - N_KV, LANES), F32)], axis=0)
    sel = jnp.where(lane_i < n_blocks, sel, 0.0)
    sel_h = _dot(rexp_ref[...], sel.astype(BF16))
    sel_keys = _dot(sel_h.astype(BF16), e_ref[...])

    for p in range(n_pages):
        s_scr[:, p * page:(p + 1) * page] = _dot(q, sel_pages[p][0].reshape(gd, page).astype(BF16))
    s_scr[:, n_pages * page:n_pages * page + LANES] = new_scores(2)
    s = jnp.where(sel_keys > 0.5, s_scr[...] + bsel_ref[...], NEG_INF)
    _, e_s, den_s = _softmax_lanes(s)
    e_b = e_s.astype(BF16)
    o_sel = new_value(e_s[:, n_pages * page:n_pages * page + LANES], 3)
    for p in range(n_pages):
        o_sel = o_sel + _dot_nt(e_b[:, p * page:(p + 1) * page], sel_pages[p][1].reshape(gd, page).astype(BF16))
    o_sel = o_sel * (1.0 / den_s)

    s_w = jnp.concatenate([_dot(q, win_ref[0].reshape(gd, win_len).astype(BF16)), new_scores(4)], axis=1)
    _, e_w, den_w = _softmax_lanes(s_w + bwin_ref[...])
    o_win = (_dot_nt(e_w[:, 0:win_len].astype(BF16), win_ref[1].reshape(gd, win_len).astype(BF16))
             + new_value(e_w[:, win_len:win_len + LANES], 5)) * (1.0 / den_w)

    g = jax.nn.sigmoid(gate_ref[...])
    o = g[:, 0:1] * o_cmp + g[:, 1:2] * o_sel + g[:, 2:3] * o_win
    own = (lax.broadcasted_iota(jnp.int32, (N_HEADS, gd), 1) // HEAD_DIM
           == lax.broadcasted_iota(jnp.int32, (N_HEADS, gd), 0) // Q_PER_KV)
    o = jnp.where(own, o, 0.0)
    acc = o[:, 0:HEAD_DIM]
    for gi in range(1, N_KV):
        acc = acc + o[:, gi * HEAD_DIM:(gi + 1) * HEAD_DIM]
    o_ref[...] = acc


def _sum3_rhs(x, rhs_bf16):
    hi, mid, lo = _split3(x)
    return _dot(hi, rhs_bf16) + _dot(mid, rhs_bf16) + _dot(lo, rhs_bf16)


def _nsa_decode(page_table, q_bd, gates_t, kvnew, kcn, cache_t, kcpool, win_t, win_out, l, tabs, *, name):
    B, n_pages = page_table.shape
    page = cache_t.shape[5]
    gd = N_KV * HEAD_DIM
    win_len = win_t.shape[5]
    bsel, bwin, bcmp, e_mat, pair, rsum, rexp, n_blocks = tabs
    blocks_per_page = kcpool.shape[1]
    const = lambda a: pl.BlockSpec(a.shape, lambda b, pt: (0,) * a.ndim)
    sel_specs = [pl.BlockSpec((None, None, 2, N_KV, HEAD_DIM, page), lambda b, pt, p=p: (l, pt[b, p], 1, 0, 0, 0))
                 for p in range(n_pages)]
    kc_specs = [pl.BlockSpec((None, blocks_per_page, 2 * gd), lambda b, pt, p=p: (pt[b, p], 0, 0))
                for p in range(n_pages)]
    win_spec = pl.BlockSpec((None, None, 2, N_KV, HEAD_DIM, win_len), lambda b, pt: (l, b, 0, 0, 0, 0))
    ops = [page_table, q_bd, gates_t, kvnew, kcn, kvnew[:, 0, 4 * gd:].reshape(B, 2 * gd, 1),
           *([cache_t] * n_pages), *([kcpool] * n_pages), win_t, bsel, bwin, bcmp, e_mat, pair, rsum, rexp]
    in_specs = ([pl.BlockSpec((None, N_HEADS, gd), lambda b, pt: (b, 0, 0)),
                 pl.BlockSpec((None, N_HEADS, 3), lambda b, pt: (b, 0, 0)),
                 pl.BlockSpec((None, 1, kvnew.shape[2]), lambda b, pt: (b, 0, 0)),
                 pl.BlockSpec((None, SUBLANES, 2 * gd), lambda b, pt: (b, 0, 0)),
                 pl.BlockSpec((None, 2 * gd, 1), lambda b, pt: (b, 0, 0))]
                + sel_specs + kc_specs
                + [win_spec, const(bsel), const(bwin), const(bcmp), const(e_mat), const(pair), const(rsum),
                   const(rexp)])
    aliases = {}
    if win_out is not None:
        aliases = {len(ops): 1}
        ops.append(win_out)
        in_specs.append(pl.BlockSpec(memory_space=pl.ANY))
    grid_spec = pltpu.PrefetchScalarGridSpec(
        num_scalar_prefetch=1,
        grid=(B,),
        in_specs=in_specs,
        out_specs=[pl.BlockSpec((None, N_HEADS, HEAD_DIM), lambda b, pt: (b, 0, 0)), win_spec],
        scratch_shapes=[pltpu.VMEM((LANES, 2 * gd), F32),
                        pltpu.VMEM((N_HEADS, n_pages * page + LANES), F32)])
    return pl.pallas_call(
        functools.partial(_nsa_decode_body, n_pages=n_pages, n_blocks=n_blocks,
                          top_n=min(TOP_N, n_blocks), win_len=win_len, has_prev=win_out is not None),
        grid_spec=grid_spec,
        out_shape=[jax.ShapeDtypeStruct((B, N_HEADS, HEAD_DIM), F32), jax.ShapeDtypeStruct(win_t.shape, F32)],
        input_output_aliases=aliases,
        compiler_params=_cparams(("arbitrary",)),
        name=name,
    )(*ops)


def _decode_tables(rel_bias, past_len, n_pages, win_len):
    q_pos = past_len
    n_blocks = (past_len + 1 + L_SEL - 1) // L_SEL
    n_cmp = n_blocks * (L_SEL // L_CMP)
    by_dist = rel_bias[_t5_bucket(jnp.arange(q_pos + 1))].T
    keys = (n_pages + 1) * LANES
    kpos = jnp.arange(keys)
    bsel = jnp.where((kpos <= q_pos)[None], by_dist[:, jnp.clip(q_pos - kpos, 0)], NEG_INF)
    wl = jnp.arange(win_len + LANES)
    wdist = win_len - wl
    bwin = jnp.where((wdist >= 0)[None], by_dist[:, jnp.clip(wdist, 0, q_pos)], NEG_INF)
    c = jnp.arange(LANES)
    cdist = q_pos - (c * L_CMP + L_CMP - 1)
    bcmp = jnp.where(((cdist >= 0) & (c < n_cmp))[None], by_dist[:, jnp.clip(cdist, 0)], NEG_INF)
    e_mat = (jnp.arange(LANES)[:, None] == (kpos // L_SEL)[None, :]).astype(BF16)
    pair = ((c[:, None] // (L_SEL // L_CMP) == c[None, :]) & (c[:, None] < n_cmp)).astype(BF16)
    heads = jnp.arange(N_HEADS)
    rsum = (jnp.arange(SUBLANES)[:, None] == (heads // Q_PER_KV)[None, :]).astype(BF16)
    return bsel, bwin, bcmp, e_mat, pair, rsum, rsum.T, n_blocks


def _conv_step_body(x_ref, p0_ref, p1_ref, p2_ref, cw_ref, cb_ref, dt_ref, dtb_ref, alog_ref,
                    xact_ref, dto_ref, dec_ref):
    y = cb_ref[...]
    for k, ref in enumerate((p0_ref, p1_ref, p2_ref, x_ref)):
        y = y + cw_ref[k:k + 1, :] * ref[...]
    xact_ref[...] = _silu(y)
    x = dt_ref[:, 0:SSM_HEADS] + dtb_ref[...]
    dt = jnp.maximum(x, 0.0) + jnp.log1p(jnp.exp(-jnp.abs(x)))
    dto_ref[...] = dt
    dec_ref[...] = jnp.exp(dt * (-jnp.exp(alog_ref[...])))


def _conv_step(xbc, prev, conv_w, conv_b, dtg, dt_bias, a_log, *, name):
    B = xbc.shape[0]
    full = lambda a: pl.BlockSpec(a.shape, lambda i: (0,) * a.ndim)
    ops = [xbc, prev[:, 0], prev[:, 1], prev[:, 2], conv_w, conv_b.reshape(1, -1), dtg,
           dt_bias.reshape(1, -1), a_log.reshape(1, -1)]
    in_specs = [full(a) for a in ops]
    in_specs[6] = pl.BlockSpec((B, LANES), lambda i: (0, 1))
    return pl.pallas_call(
        _conv_step_body,
        grid=(1,),
        in_specs=in_specs,
        out_specs=[pl.BlockSpec((B, CONV_DIM), lambda i: (0, 0)), pl.BlockSpec((B, SSM_HEADS), lambda i: (0, 0)),
                   pl.BlockSpec((B, SSM_HEADS), lambda i: (0, 0))],
        out_shape=[jax.ShapeDtypeStruct((B, CONV_DIM), F32), jax.ShapeDtypeStruct((B, SSM_HEADS), F32),
                   jax.ShapeDtypeStruct((B, SSM_HEADS), F32)],
        compiler_params=_cparams(("arbitrary",)),
        name=name,
    )(*ops)


def _ssd_step_body(dts_ref, decs_ref, h0_ref, xs_ref, xst_ref, bm_ref, cm_ref, z_ref, dtc_ref, dskip_ref, ng_ref,
                   *refs, has_prev):
    y_ref, st_ref, yoff_scr = refs[int(has_prev):]
    b = pl.program_id(0)
    heads_per_group = SSM_HEADS // SSM_GROUPS
    bm = bm_ref[...]
    cm = cm_ref[...]
    rb = lambda v: v.astype(BF16).astype(F32)
    cb = jnp.sum(rb(cm) * rb(bm), axis=-1, keepdims=True)
    cm8 = jnp.concatenate([cm, jnp.zeros((SUBLANES - SSM_GROUPS, D_STATE), F32)], axis=0).astype(BF16)
    for h in range(SSM_HEADS):
        g = h // heads_per_group
        h_prev = h0_ref[h]
        dec = decs_ref[b, h]
        y_off = _dot_nt(cm8, h_prev.astype(BF16))
        yoff_scr[h:h + 1, :] = y_off[g:g + 1, :] * dec
        xdt = xst_ref[:, h:h + 1] * dts_ref[b, h]
        st_ref[h] = h_prev * dec + xdt * bm[g:g + 1, :]
    xs = xs_ref[...]
    cbh = jnp.concatenate([jnp.broadcast_to(cb[g:g + 1, :], (heads_per_group, 1)) for g in range(SSM_GROUPS)], axis=0)
    y = (cbh * dtc_ref[...]) * xs + yoff_scr[...] + dskip_ref[...] * xs
    y = y * _silu(z_ref[...])
    for g in range(SSM_GROUPS):
        yg = y[g * heads_per_group:(g + 1) * heads_per_group, :]
        ms = jnp.sum(jnp.sum(yg * yg, axis=1, keepdims=True), axis=0, keepdims=True) * (1.0 / (heads_per_group * SSM_HEADDIM))
        y_ref[g * heads_per_group:(g + 1) * heads_per_group, :] = (
            yg * lax.rsqrt(ms + EPS) * ng_ref[g * heads_per_group:(g + 1) * heads_per_group, :]).astype(y_ref.dtype)


def _ssd_step(state, state_out, l, xact, dt, dec, z, d_skip, norm_g, *, name):
    B = xact.shape[0]
    H, Pd, N = SSM_HEADS, SSM_HEADDIM, D_STATE
    xs = xact[:, 0:D_INNER].reshape(B, H, Pd)
    bm = xact[:, D_INNER:D_INNER + SSM_GROUPS * N].reshape(B, SSM_GROUPS, N)
    cm = xact[:, D_INNER + SSM_GROUPS * N:].reshape(B, SSM_GROUPS, N)
    per_b = lambda shape: pl.BlockSpec((None,) + shape, lambda b, *_: (b,) + (0,) * len(shape))
    const = lambda shape: pl.BlockSpec(shape, lambda b, *_: (0,) * len(shape))
    state_spec = pl.BlockSpec((None, None, H, Pd, N), lambda b, *_: (l, b, 0, 0, 0))
    ops = [dt, dec, state, xs, xs.transpose(0, 2, 1), bm, cm, z.reshape(B, H, Pd), dt.reshape(B, H, 1),
           d_skip.reshape(H, 1), norm_g.reshape(H, Pd)]
    in_specs = [state_spec, per_b((H, Pd)), per_b((Pd, H)), per_b((SSM_GROUPS, N)), per_b((SSM_GROUPS, N)),
                per_b((H, Pd)), per_b((H, 1)), const((H, 1)), const((H, Pd))]
    aliases = {}
    if state_out is not None:
        aliases = {len(ops): 1}
        ops.append(state_out)
        in_specs.append(pl.BlockSpec(memory_space=pl.ANY))
    grid_spec = pltpu.PrefetchScalarGridSpec(
        num_scalar_prefetch=2,
        grid=(B,),
        in_specs=in_specs,
        out_specs=[per_b((H, Pd)), state_spec],
        scratch_shapes=[pltpu.VMEM((H, Pd), F32)])
    return pl.pallas_call(
        functools.partial(_ssd_step_body, has_prev=state_out is not None),
        grid_spec=grid_spec,
        out_shape=[jax.ShapeDtypeStruct((B, H, Pd), BF16), jax.ShapeDtypeStruct(state.shape, F32)],
        input_output_aliases=aliases,
        compiler_params=_cparams(("arbitrary",)),
        name=name,
    )(*ops)


def _ffn_up_step_body(h_ref, wg_ref, wu_ref, p0_ref, p1_ref, cw_ref, cb_ref, act_ref, gate_ref):
    h = h_ref[...]
    gate = _dot(h, wg_ref[...])
    up = _dot(h, wu_ref[...])
    y = cb_ref[...] + cw_ref[0:1, :] * p0_ref[...] + cw_ref[1:2, :] * p1_ref[...] + cw_ref[2:3, :] * gate
    gate_ref[...] = gate
    act_ref[...] = (_silu(y) * up).astype(act_ref.dtype)


def _ffn_up_step(h, w_gu, prev, conv_w, conv_b, *, tn, name):
    B, K = h.shape
    nj = D_FF // tn
    col = lambda r: pl.BlockSpec((r, tn), lambda j: (0, j))
    return pl.pallas_call(
        _ffn_up_step_body,
        grid=(nj,),
        in_specs=[pl.BlockSpec((B, K), lambda j: (0, 0)), col(K),
                  pl.BlockSpec((K, tn), lambda j, nj=nj: (0, j + nj)),
                  col(B), col(B), col(FFN_CONV), col(1)],
        out_specs=[col(B), col(B)],
        out_shape=[jax.ShapeDtypeStruct((B, D_FF), BF16), jax.ShapeDtypeStruct((B, D_FF), F32)],
        compiler_params=_cparams(("parallel",)),
        name=name,
    )(h, w_gu, w_gu, prev[:, 0], prev[:, 1], conv_w, conv_b.reshape(1, D_FF))


def _pages_per_step(n_pool, limit=32):
    return max(p for p in range(2, limit + 1, 2) if n_pool % p == 0)


def _sample_layer(x, mod, l, Wl, P, C):
    B = x.shape[0]
    gd = N_KV * HEAD_DIM
    sh1, sc1, g1, sh2, sc2, g2 = jnp.split(mod, 6, axis=-1)
    h = _norm_mod(x, P['norm1_g'][l], sc1, sh1, tm=B, rows_per_batch=1, name=f"norm1_s{l}")
    q = _mm(h, Wl['q'], tm=B, tn=512, name=f"proj_q_s{l}", out_dtype=BF16, epilogue=lambda acc: acc * SCALE)
    kv = _mm(h, Wl['kv'], tm=B, tn=512, name=f"proj_kv_s{l}")
    small = _mm(h, Wl['small'], tm=B, tn=2 * LANES, name=f"proj_small_s{l}")
    z = _mm(h, Wl['z'], tm=B, tn=512, name=f"proj_z_s{l}")
    xbc = _mm(h, Wl['xbc'], tm=B, tn=512, name=f"proj_xbc_s{l}")
    gates = _mm(h, Wl['mg'], tm=B, tn=512, name=f"proj_mg_s{l}", epilogue=jax.nn.sigmoid)

    cache_kv, page_table = C['cache_kv'], C['page_table']
    n_pool, page = cache_kv.shape[1], cache_kv.shape[2]
    cache_t = cache_kv.transpose(0, 1, 3, 4, 5, 2)
    win_t = C['cache_win_kv'].transpose(0, 1, 3, 4, 5, 2)
    kcpool = _pool_compress(cache_t, l, Wl['pe_pair'], Wl['w1_pair'], Wl['cmp_w2'],
                            pages_per_step=_pages_per_step(n_pool), name=f"cmp_pool_s{l}")
    kcpool = kcpool.reshape(n_pool, page // L_CMP, 2 * gd)

    def new_blocks(slot):
        first = jnp.pad(kv[:, slot * gd:(slot + 1) * gd].reshape(B, N_KV, 1, HEAD_DIM),
                        ((0, 0), (0, 0), (0, 1), (0, (L_CMP - 1) * HEAD_DIM)))
        rows = B * N_KV * 2
        out = _compress(first.reshape(rows, L_CMP * HEAD_DIM), P['cmp_pe'][l, slot], Wl['cmp_w1'][slot],
                        Wl['cmp_w2'][slot], tm=512 if rows % 512 == 0 else rows, name=f"cmp{slot}_new_s{l}")
        return out.reshape(B, N_KV, 2, HEAD_DIM).transpose(0, 2, 1, 3).reshape(B, 2, gd)

    kcn = jnp.pad(jnp.concatenate([new_blocks(0), new_blocks(1)], axis=-1), ((0, 0), (0, SUBLANES - 2), (0, 0)))
    qh = q.reshape(B, N_KV, Q_PER_KV, 1, HEAD_DIM)
    q_bd = (qh * jnp.eye(N_KV, dtype=BF16)[None, :, None, :, None]).reshape(B, N_HEADS, gd)
    gates_t = small[:, 0:NSA_GATE_COLS].reshape(B, 3, N_HEADS).transpose(0, 2, 1)
    o_nsa, C['win_out'] = _nsa_decode(page_table, q_bd, gates_t, kv.reshape(B, 1, KV_COLS), kcn, cache_t, kcpool,
                                      win_t, C.get('win_out'), l, C['tables'], name=f"nsa_s{l}")
    o_nsa = o_nsa.reshape(B, Q_COLS)
    new_rows = kv[:, 0:4 * gd].reshape(B, 1, 4, N_KV, HEAD_DIM)

    conv_prev = C['state_ssm_conv'][l]
    xact, dt, dec = _conv_step(xbc, conv_prev, P['ssm_conv_w'][l], P['ssm_conv_b'][l], small,
                               P['dt_bias'][l], P['a_log'][l], name=f"conv_s{l}")
    conv_state = jnp.concatenate([conv_prev[:, 1:], xbc[:, None]], axis=1)
    y, C['ssm_out'] = _ssd_step(C['state_ssm'], C.get('ssm_out'), l, xact, dt, dec, z, P['d_skip'][l],
                                P['ssm_norm_g'][l], name=f"ssd_s{l}")

    u = _merge(o_nsa.astype(BF16), y.reshape(B, D_INNER), Wl['a'], Wl['b'], gates, tm=B, tn=512, name=f"merge_s{l}")
    x = _mm(u, Wl['o'], tm=B, tn=512, name=f"proj_o_s{l}",
            epilogue=lambda acc, xr, gr: xr + gr * acc, extras=[('full', x), ('full', g1)])

    h2 = _norm_mod(x, P['norm2_g'][l], sc2, sh2, tm=B, rows_per_batch=1, name=f"norm2_s{l}")
    ffn_prev = C['state_ffn_conv'][l]
    act, gate_raw = _ffn_up_step(h2, Wl['gu'], ffn_prev, P['ffn_conv_w'][l], P['ffn_conv_b'][l],
                                 tn=D_FF // 2, name=f"ffn_up_s{l}")
    ffn_state = jnp.concatenate([ffn_prev[:, 1:], gate_raw[:, None]], axis=1)
    x = _mm(act, Wl['down'], tm=B, tn=512, name=f"ffn_down_s{l}",
            epilogue=lambda acc, xr, gr: xr + gr * acc, extras=[('full', x), ('full', g2)])
    return x, (new_rows, conv_state, ffn_state)


def kernel(x_prompt, x_sample, cache_kv, cache_win_kv, state_ssm_conv, state_ssm, state_ffn_conv, page_table, c_prompt, c_sample, w_ada, b_ada, norm1_g, norm2_g, final_g, w_in, rel_bias, cmp_pe, cmp_w1, cmp_w2, ssm_conv_w, ssm_conv_b, dt_bias, a_log, d_skip, ssm_norm_g, w_a, w_b, w_o, w_gu, ffn_conv_w, ffn_conv_b, w_down):
    P = dict(norm1_g=norm1_g, norm2_g=norm2_g, rel_bias=rel_bias, cmp_pe=cmp_pe.reshape(DEPTH, 2, -1),
             ssm_conv_w=ssm_conv_w, ssm_conv_b=ssm_conv_b, dt_bias=dt_bias, a_log=a_log, d_skip=d_skip,
             ssm_norm_g=ssm_norm_g, ffn_conv_w=ffn_conv_w, ffn_conv_b=ffn_conv_b)
    Bp, T, _ = x_prompt.shape
    Bs = x_sample.shape[0]
    past_len = page_table.shape[1] * cache_kv.shape[2]
    tables = _bias_tables(rel_bias * LOG2E)
    n_c = Bp + Bs
    c_rows = -(-n_c // SUBLANES) * SUBLANES
    c_all = jnp.zeros((c_rows, D_MODEL), F32).at[:Bp].set(c_prompt).at[Bp:n_c].set(c_sample).astype(BF16)
    xp = x_prompt.reshape(Bp * T, D_MODEL)
    xs = x_sample.reshape(Bs, D_MODEL)
    C = dict(cache_kv=cache_kv, page_table=page_table, cache_win_kv=cache_win_kv, state_ssm_conv=state_ssm_conv,
             state_ssm=state_ssm, state_ffn_conv=state_ffn_conv,
             tables=_decode_tables(rel_bias, past_len, page_table.shape[1], cache_win_kv.shape[2]))
    st_p, st_s = [], []
    for l in range(DEPTH):
        Wl = _layer_weights(l, w_ada, w_in, cmp_pe, cmp_w1, cmp_w2, w_a, w_b, w_o, w_gu, w_down)
        mod = _mm(c_all, Wl['ada'], tm=c_rows, tn=512, name=f"ada{l}",
                  epilogue=lambda acc, b: acc + b, extras=[('col', b_ada[l].reshape(1, -1))])
        xp, sp = _prompt_layer(xp, mod[:Bp], l, Wl, P, tables, Bp, T)
        st_p.append(sp)
        xs, ss = _sample_layer(xs, mod[Bp:n_c], l, Wl, P, C)
        st_s.append(ss)
    y_prompt = _final_norm(xp, final_g, tm=512, name="final_norm_p").reshape(Bp, T, D_MODEL)
    y_sample = _final_norm(xs, final_g, tm=Bs, name="final_norm_s").reshape(Bs, 1, D_MODEL)
    kv_p, win_p, conv_p, ssm_p, ffn_p = [jnp.stack([s[i] for s in st_p]) for i in range(5)]
    kv_s, conv_s, ffn_s = [jnp.stack([s[i] for s in st_s]) for i in range(3)]
    win_s = C['win_out'].transpose(0, 1, 5, 2, 3, 4)
    ssm_s = C['ssm_out']
    return (y_prompt, y_sample, kv_p, win_p, conv_p, ssm_p, ffn_p, kv_s, win_s, conv_s, ssm_s, ffn_s)
```

```python
import functools
import math

import jax
import jax.numpy as jnp
from jax import lax
from jax.experimental import pallas as pl
from jax.experimental.pallas import tpu as pltpu

D_MODEL = 1024
N_HEADS = 16
N_KV = 4
HEAD_DIM = 64
Q_PER_KV = N_HEADS // N_KV
L_CMP = 32
L_SEL = 64
TOP_N = 16
WINDOW = 512
CMP_HID = 128
Q_BLOCK = 128
SCALE = HEAD_DIM ** -0.5
N_BUCKETS = 32
MAX_DISTANCE = 128
D_INNER = 2 * D_MODEL
SSM_HEADDIM = 64
SSM_HEADS = D_INNER // SSM_HEADDIM
SSM_GROUPS = 4
D_STATE = 128
SSM_CONV = 4
SSM_CHUNK = 128
CONV_DIM = D_INNER + 2 * SSM_GROUPS * D_STATE
D_FF = 2816
FFN_CONV = 3
EPS = 1e-6
NEG_INF = -1e30
DEPTH = 2

Q_COLS = N_HEADS * HEAD_DIM
KV_COLS = 6 * N_KV * HEAD_DIM
NSA_GATE_COLS = 3 * N_HEADS
OFF_KV = Q_COLS
OFF_G = OFF_KV + KV_COLS
OFF_Z = OFF_G + NSA_GATE_COLS
OFF_XBC = OFF_Z + D_INNER
OFF_DT = OFF_XBC + CONV_DIM
OFF_MG = OFF_DT + SSM_HEADS
P_IN = OFF_MG + 2 * D_MODEL

LOG2E = math.log2(math.e)

LANES = 128
SUBLANES = 8
BF16_ROWS = 16
TQ = 128
WIDTH = Q_PER_KV * TQ
FAR_TILES = 4
WIN_TILES = WINDOW // TQ + 1
AUG_ROW0 = HEAD_DIM
VMEM_LIMIT = 48 * 1024 * 1024

BF16 = jnp.bfloat16
F32 = jnp.float32


def _cparams(sem):
    return pltpu.CompilerParams(dimension_semantics=sem, vmem_limit_bytes=VMEM_LIMIT)


def _silu(x):
    return x * jax.nn.sigmoid(x)


def _dot(a, b):
    return jnp.dot(a, b, preferred_element_type=F32)


def _dot_nt(a, b):
    return lax.dot_general(a, b, (((1,), (1,)), ((), ())), preferred_element_type=F32)


def _dot_tn(a, b):
    return lax.dot_general(a, b, (((0,), (0,)), ((), ())), preferred_element_type=F32)


def _split3(x):
    hi = x.astype(BF16)
    r1 = x - hi.astype(F32)
    mid = r1.astype(BF16)
    lo = (r1 - mid.astype(F32)).astype(BF16)
    return hi, mid, lo


def _mm_body(a_ref, w_ref, *refs, epilogue, n_extra):
    acc = _dot(a_ref[...], w_ref[...])
    extras = [r[...] for r in refs[:n_extra]]
    o_ref = refs[n_extra]
    o_ref[...] = epilogue(acc, *extras).astype(o_ref.dtype)


def _mm(a, w, *, tm, tn, name, out_dtype=F32, epilogue=None, extras=(), rows_per_batch=None):
    M, K = a.shape
    N = w.shape[1]
    assert M % tm == 0 and N % tn == 0, (M, N, tm, tn)
    in_specs = [pl.BlockSpec((tm, K), lambda i, j: (i, 0)),
                pl.BlockSpec((K, tn), lambda i, j: (0, j))]
    ops = [a, w]
    for kind, arr in extras:
        if kind == 'col':
            in_specs.append(pl.BlockSpec((1, tn), lambda i, j: (0, j)))
        elif kind == 'full':
            in_specs.append(pl.BlockSpec((tm, tn), lambda i, j: (i, j)))
        else:
            assert kind == 'batch' and rows_per_batch % tm == 0
            tpb = rows_per_batch // tm
            in_specs.append(pl.BlockSpec((None, 1, tn), lambda i, j, tpb=tpb: (i // tpb, 0, j)))
        ops.append(arr)
    if epilogue is None:
        epilogue = lambda acc: acc
    return pl.pallas_call(
        functools.partial(_mm_body, epilogue=epilogue, n_extra=len(extras)),
        grid=(M // tm, N // tn),
        in_specs=in_specs,
        out_specs=pl.BlockSpec((tm, tn), lambda i, j: (i, j)),
        out_shape=jax.ShapeDtypeStruct((M, N), out_dtype),
        compiler_params=_cparams(("parallel", "parallel")),
        name=name,
    )(*ops)


def _mm_t_body(wt_ref, a_ref, *refs, scale):
    o_ref = refs[-1]
    o_ref[...] = (_dot_nt(wt_ref[...], a_ref[...]) * scale).astype(o_ref.dtype)


def _mm_t(a, wt, *, tm, tn, rows_per_batch, name, out_dtype=F32, scale=1.0, layer=None, out_buf=None):
    M, K = a.shape
    N = wt.shape[0]
    T = rows_per_batch
    assert T % tm == 0 and N % tn == 0, (T, N, tm, tn)
    tpb = T // tm
    if layer is None:
        shape = (M // T, N, T)
        out_spec = pl.BlockSpec((None, tn, tm), lambda i, j: (i // tpb, j, i % tpb))
    else:
        shape = (DEPTH, M // T, N, T)
        out_spec = pl.BlockSpec((None, None, tn, tm), lambda i, j: (layer, i // tpb, j, i % tpb))
    ops = [wt, a]
    in_specs = [pl.BlockSpec((tn, K), lambda i, j: (j, 0)), pl.BlockSpec((tm, K), lambda i, j: (i, 0))]
    aliases = {}
    if out_buf is not None:
        aliases = {len(ops): 0}
        ops.append(out_buf)
        in_specs.append(pl.BlockSpec(memory_space=pl.ANY))
    return pl.pallas_call(
        functools.partial(_mm_t_body, scale=scale),
        grid=(M // tm, N // tn),
        in_specs=in_specs,
        out_specs=out_spec,
        out_shape=jax.ShapeDtypeStruct(shape, out_dtype),
        input_output_aliases=aliases,
        compiler_params=_cparams(("parallel", "parallel")),
        name=name,
    )(*ops)


def _norm_mod_body(x_ref, g_ref, sc_ref, sh_ref, o_ref):
    x = x_ref[...]
    y = x * lax.rsqrt(jnp.mean(x * x, axis=-1, keepdims=True) + EPS) * g_ref[...]
    o_ref[...] = (y * (1.0 + sc_ref[...]) + sh_ref[...]).astype(o_ref.dtype)


def _norm_mod(x, g, sc, sh, *, tm, rows_per_batch, name):
    M, D = x.shape
    if sc.ndim == 3:
        tpb = rows_per_batch // tm
        mod_spec = pl.BlockSpec((None, 1, D), lambda i: (i // tpb, 0, 0))
    else:
        mod_spec = pl.BlockSpec((tm, D), lambda i: (i, 0))
    return pl.pallas_call(
        _norm_mod_body,
        grid=(M // tm,),
        in_specs=[pl.BlockSpec((tm, D), lambda i: (i, 0)), pl.BlockSpec((1, D), lambda i: (0, 0)),
                  mod_spec, mod_spec],
        out_specs=pl.BlockSpec((tm, D), lambda i: (i, 0)),
        out_shape=jax.ShapeDtypeStruct((M, D), BF16),
        compiler_params=_cparams(("parallel",)),
        name=name,
    )(x, g.reshape(1, D), sc, sh)


def _final_norm_body(x_ref, g_ref, o_ref):
    x = x_ref[...]
    o_ref[...] = x * lax.rsqrt(jnp.mean(x * x, axis=-1, keepdims=True) + EPS) * g_ref[...]


def _final_norm(x, g, *, tm, name):
    M, D = x.shape
    return pl.pallas_call(
        _final_norm_body,
        grid=(M // tm,),
        in_specs=[pl.BlockSpec((tm, D), lambda i: (i, 0)), pl.BlockSpec((1, D), lambda i: (0, 0))],
        out_specs=pl.BlockSpec((tm, D), lambda i: (i, 0)),
        out_shape=jax.ShapeDtypeStruct((M, D), F32),
        compiler_params=_cparams(("parallel",)),
        name=name,
    )(x, g.reshape(1, D))


def _merge_body(o_ref, y_ref, wa_ref, wb_ref, ga_ref, gb_ref, u_ref):
    pa = _dot(o_ref[...], wa_ref[...])
    pb = _dot(y_ref[...], wb_ref[...])
    u_ref[...] = (ga_ref[...] * pa + gb_ref[...] * pb).astype(u_ref.dtype)


def _merge(o_nsa, y, w_a, w_b, gates, *, tm, tn, name):
    M = o_nsa.shape[0]
    N = w_a.shape[1]
    nj = N // tn
    return pl.pallas_call(
        _merge_body,
        grid=(M // tm, nj),
        in_specs=[pl.BlockSpec((tm, o_nsa.shape[1]), lambda i, j: (i, 0)),
                  pl.BlockSpec((tm, y.shape[1]), lambda i, j: (i, 0)),
                  pl.BlockSpec((w_a.shape[0], tn), lambda i, j: (0, j)),
                  pl.BlockSpec((w_b.shape[0], tn), lambda i, j: (0, j)),
                  pl.BlockSpec((tm, tn), lambda i, j: (i, j)),
                  pl.BlockSpec((tm, tn), lambda i, j, nj=nj: (i, j + nj))],
        out_specs=pl.BlockSpec((tm, tn), lambda i, j: (i, j)),
        out_shape=jax.ShapeDtypeStruct((M, N), BF16),
        compiler_params=_cparams(("parallel", "parallel")),
        name=name,
    )(o_nsa, y, w_a, w_b, gates, gates)


def _ffn_up_body(h_ref, wg_ref, wu_ref, cw_ref, cb_ref, o_ref, pad_ref, *, tm, tiles_per_batch):
    i = pl.program_id(1)
    h = h_ref[...]
    gate = _dot(h, wg_ref[...])
    up = _dot(h, wu_ref[...])

    @pl.when(i % tiles_per_batch == 0)
    def _():
        pad_ref[0:SUBLANES, :] = jnp.zeros((SUBLANES, pad_ref.shape[1]), F32)

    pad_ref[SUBLANES:SUBLANES + tm, :] = gate
    y = cb_ref[...]
    for k in range(FFN_CONV):
        off = SUBLANES - (FFN_CONV - 1) + k
        y = y + cw_ref[k:k + 1, :] * pad_ref[off:off + tm, :]
    pad_ref[0:SUBLANES, :] = pad_ref[tm:tm + SUBLANES, :]
    o_ref[...] = (_silu(y) * up).astype(o_ref.dtype)


def _ffn_up(h, w_gu, conv_w, conv_b, *, tm, tn, rows_per_batch, name):
    M, K = h.shape
    nj = D_FF // tn
    return pl.pallas_call(
        functools.partial(_ffn_up_body, tm=tm, tiles_per_batch=rows_per_batch // tm),
        grid=(nj, M // tm),
        in_specs=[pl.BlockSpec((tm, K), lambda j, i: (i, 0)),
                  pl.BlockSpec((K, tn), lambda j, i: (0, j)),
                  pl.BlockSpec((K, tn), lambda j, i, nj=nj: (0, j + nj)),
                  pl.BlockSpec((FFN_CONV, tn), lambda j, i: (0, j)),
                  pl.BlockSpec((1, tn), lambda j, i: (0, j))],
        out_specs=pl.BlockSpec((tm, tn), lambda j, i: (i, j)),
        out_shape=jax.ShapeDtypeStruct((M, D_FF), BF16),
        scratch_shapes=[pltpu.VMEM((tm + SUBLANES, tn), F32)],
        compiler_params=_cparams(("arbitrary", "arbitrary")),
        name=name,
    )(h, w_gu, w_gu, conv_w, conv_b.reshape(1, D_FF))


def _proj_conv_body(h_ref, w_ref, cw_ref, cb_ref, o_ref, pad_ref, *, tm, taps, tiles_per_batch):
    i = pl.program_id(1)

    @pl.when(i % tiles_per_batch == 0)
    def _():
        pad_ref[0:SUBLANES, :] = jnp.zeros((SUBLANES, pad_ref.shape[1]), F32)

    pad_ref[SUBLANES:SUBLANES + tm, :] = _dot(h_ref[...], w_ref[...])
    y = cb_ref[...]
    for k in range(taps):
        off = SUBLANES - (taps - 1) + k
        y = y + cw_ref[k:k + 1, :] * pad_ref[off:off + tm, :]
    pad_ref[0:SUBLANES, :] = pad_ref[tm:tm + SUBLANES, :]
    o_ref[...] = _silu(y)


def _proj_conv(h, w, conv_w, conv_b, *, tm, tn, rows_per_batch, name):
    M, K = h.shape
    N = w.shape[1]
    taps = conv_w.shape[0]
    return pl.pallas_call(
        functools.partial(_proj_conv_body, tm=tm, taps=taps, tiles_per_batch=rows_per_batch // tm),
        grid=(N // tn, M // tm),
        in_specs=[pl.BlockSpec((tm, K), lambda j, i: (i, 0)),
                  pl.BlockSpec((K, tn), lambda j, i: (0, j)),
                  pl.BlockSpec((taps, tn), lambda j, i: (0, j)),
                  pl.BlockSpec((1, tn), lambda j, i: (0, j))],
        out_specs=pl.BlockSpec((tm, tn), lambda j, i: (i, j)),
        out_shape=jax.ShapeDtypeStruct((M, N), F32),
        scratch_shapes=[pltpu.VMEM((tm + SUBLANES, tn), F32)],
        compiler_params=_cparams(("arbitrary", "arbitrary")),
        name=name,
    )(h, w, conv_w, conv_b.reshape(1, N))


def _compress_body(x_ref, pe_ref, w1_ref, w2_ref, o_ref):
    xb = (x_ref[...] + pe_ref[...]).astype(BF16)
    hid = _silu(_dot(xb, w1_ref[...]))
    o_ref[...] = _dot(hid.astype(BF16), w2_ref[...])


def _compress(x, pe, w1, w2, *, tm, name):
    rows, kdim = x.shape
    return pl.pallas_call(
        _compress_body,
        grid=(rows // tm,),
        in_specs=[pl.BlockSpec((tm, kdim), lambda i: (i, 0)), pl.BlockSpec((1, kdim), lambda i: (0, 0)),
                  pl.BlockSpec((kdim, CMP_HID), lambda i: (0, 0)),
                  pl.BlockSpec((CMP_HID, HEAD_DIM), lambda i: (0, 0))],
        out_specs=pl.BlockSpec((tm, HEAD_DIM), lambda i: (i, 0)),
        out_shape=jax.ShapeDtypeStruct((rows, HEAD_DIM), F32),
        compiler_params=_cparams(("parallel",)),
        name=name,
    )(x, pe.reshape(1, kdim), w1, w2)


def _attend(s, vt):
    m = jnp.max(s, axis=0, keepdims=True)
    pv = _dot(vt, jnp.exp2(s - m).astype(BF16))
    return m, pv[HEAD_DIM:HEAD_DIM + 1, :], pv[0:HEAD_DIM, :]


def _lane_concat(tiles3):
    return jnp.concatenate([tiles3[t] for t in range(tiles3.shape[0])], axis=1)


SEL_TILES = 4

def _nsa_select_body(q_ref, kc_ref, vct_ref, band_ref, ocmp_ref, mask_ref, *, n_blocks, n_cmp, top_n):
    step = pl.program_id(2)
    half = n_cmp // 2
    row = lax.broadcasted_iota(jnp.int32, (n_cmp, 16), 0)
    cidx = jnp.where(row < half, 2 * row, 2 * (row - half) + 1)
    col16 = lax.broadcasted_iota(jnp.int32, (n_cmp, 16), 1)
    b_hi, b_mid, b_lo = _split3(band_ref[...])
    imps = []
    for t in range(SEL_TILES):
        qi = SEL_TILES * step + t
        crel = cidx - (TQ // L_CMP) * qi
        cat = jnp.where(crel <= -5, 8, jnp.where(crel >= 4, 9, crel + 4))
        place = (cat == col16).astype(BF16)
        bias_c = _dot(place, b_hi) + _dot(place, b_mid) + _dot(place, b_lo)
        s = _dot(kc_ref[...], q_ref[t]) + bias_c
        m = jnp.max(s, axis=0, keepdims=True)
        e = jnp.exp2(s - m)
        denom = jnp.sum(e, axis=0, keepdims=True)
        anyvalid = (m > 0.5 * NEG_INF).astype(F32)
        p = e * (anyvalid / denom)
        ocmp_ref[t] = _dot(vct_ref[...], p.astype(BF16))
        psum = p[:, 0:TQ]
        for r in range(1, Q_PER_KV):
            psum = psum + p[:, r * TQ:(r + 1) * TQ]
        imps.append(psum[0:half, :] + psum[half:n_cmp, :])
    imp = jnp.concatenate(imps, axis=1)

    wide = SEL_TILES * TQ
    blk = lax.broadcasted_iota(jnp.int32, (n_blocks, wide), 0)
    blk_f = blk.astype(F32)
    qpos = step * wide + lax.broadcasted_iota(jnp.int32, (n_blocks, wide), 1)
    forced = (blk == 0) | (blk == lax.shift_right_logical(qpos, 6))
    valid = blk * L_SEL <= qpos
    score = jnp.where(forced, 1e30, jnp.where(valid, imp, -1.0))
    picked = jnp.zeros((n_blocks, wide), jnp.bool_)
    for _ in range(top_n):
        mx = jnp.max(score, axis=0, keepdims=True)
        first = jnp.min(jnp.where(score == mx, blk_f, float(n_blocks)), axis=0, keepdims=True)
        hit = blk_f == first
        picked = picked | hit
        score = jnp.where(hit, -2.0, score)
    seladd = jnp.where(picked, 0.0, NEG_INF)
    for t in range(SEL_TILES):
        mask_ref[t] = seladd[:, t * TQ:(t + 1) * TQ]


def _nsa_select(q2, kc, vct, band, *, name):
    B, G, nq, D, _ = q2.shape
    n_cmp = kc.shape[2]
    n_blocks = n_cmp // (L_SEL // L_CMP)
    assert nq % SEL_TILES == 0
    tiles = lambda shape: pl.BlockSpec((None, None, SEL_TILES) + shape, lambda b, g, i: (b, g, i, 0, 0))
    return pl.pallas_call(
        functools.partial(_nsa_select_body, n_blocks=n_blocks, n_cmp=n_cmp, top_n=min(TOP_N, n_blocks)),
        grid=(B, G, nq // SEL_TILES),
        in_specs=[tiles((D, WIDTH)),
                  pl.BlockSpec((None, None, n_cmp, D), lambda b, g, i: (b, g, 0, 0)),
                  pl.BlockSpec((None, None, D, n_cmp), lambda b, g, i: (b, g, 0, 0)),
                  pl.BlockSpec((None, 16, WIDTH), lambda b, g, i: (g, 0, 0))],
        out_specs=[tiles((D, WIDTH)), tiles((n_blocks, TQ))],
        out_shape=[jax.ShapeDtypeStruct((B, G, nq, D, WIDTH), F32),
                   jax.ShapeDtypeStruct((B, G, nq, n_blocks, TQ), F32)],
        compiler_params=_cparams(("parallel", "parallel", "parallel")),
        name=name,
    )(q2, kc, vct, band)


def _nsa_prompt_body(q_ref, ksel_ref, vsel_ref, kwin_ref, vwin_ref, ocmp_ref, mask_ref, gate_ref,
                     tzw_ref, tzn_ref, caug_ref, o_ref,
                     qaug_ref, m_ref, l_ref, acc_ref, sel0_ref, selfar_ref, sa_scr, sb_scr, *, n_blocks):
    qi = pl.program_id(2)
    n_grp_rows = n_blocks // SUBLANES

    @pl.when(qi == 0)
    def _():
        selfar_ref[:, SUBLANES:2 * SUBLANES, :] = jnp.broadcast_to(
            caug_ref[...][None], (selfar_ref.shape[0], SUBLANES, WIDTH))
        selfar_ref[n_grp_rows:, 0:SUBLANES, :] = jnp.full(
            (selfar_ref.shape[0] - n_grp_rows, SUBLANES, WIDTH), NEG_INF, F32)
        sel0_ref[0:SUBLANES, :] = jnp.zeros((SUBLANES, WIDTH), F32)

    qaug_ref[0:HEAD_DIM, :] = q_ref[...]
    qaug_ref[HEAD_DIM:, :] = jnp.zeros((qaug_ref.shape[0] - HEAD_DIM, WIDTH), BF16)

    kw = kwin_ref[pl.ds(qi, WIN_TILES)].reshape(WIN_TILES * TQ, LANES)
    m_w, l_w, a_w = _attend(_dot(kw, qaug_ref[...]) + tzw_ref[...], _lane_concat(vwin_ref[pl.ds(qi, WIN_TILES)]))
    o_win = a_w * (1.0 / l_w)

    seladd = mask_ref[...]
    blk = lax.broadcasted_iota(jnp.int32, (n_blocks, TQ), 0)
    selfar = jnp.where(blk < 2 * (qi - 1), seladd, NEG_INF)
    sel0_ref[SUBLANES:, :] = jnp.concatenate([seladd] * Q_PER_KV, axis=1)
    selfar_ref[0:n_grp_rows, 0:SUBLANES, :] = jnp.concatenate([selfar] * Q_PER_KV, axis=1).reshape(
        n_grp_rows, SUBLANES, WIDTH)

    near_mask = jnp.concatenate(
        [jnp.broadcast_to(sel0_ref[pl.ds(SUBLANES + 2 * (qi - 1) + i, 1), :], (L_SEL, WIDTH)) for i in range(4)],
        axis=0)
    kn = ksel_ref[pl.ds(qi, 2)].reshape(2 * TQ, LANES)
    m_n, l_n, a_n = _attend(_dot(kn, qaug_ref[...]) + tzn_ref[...] + near_mask, _lane_concat(vsel_ref[pl.ds(qi, 2)]))
    m_ref[...] = m_n
    l_ref[...] = l_n
    acc_ref[...] = a_n

    def far_scores(gi):
        qaug_ref[AUG_ROW0:AUG_ROW0 + BF16_ROWS, :] = selfar_ref[gi].astype(BF16)
        kg = ksel_ref[pl.ds(1 + FAR_TILES * gi, FAR_TILES)].reshape(FAR_TILES * TQ, LANES)
        return _dot(kg, qaug_ref[...])

    def far_update(s, gi):
        m_g, l_g, a_g = _attend(s, _lane_concat(vsel_ref[pl.ds(1 + FAR_TILES * gi, FAR_TILES)]))
        m_old = m_ref[...]
        m_new = jnp.maximum(m_old, m_g)
        a_old = jnp.exp2(m_old - m_new)
        a_grp = jnp.exp2(m_g - m_new)
        l_ref[...] = l_ref[...] * a_old + l_g * a_grp
        acc_ref[...] = acc_ref[...] * a_old + a_g * a_grp
        m_ref[...] = m_new

    sa_scr[...] = far_scores(0)

    def far_body(j, carry):
        sb_scr[...] = far_scores(2 * j + 1)
        far_update(sa_scr[...], 2 * j)
        sa_scr[...] = far_scores(2 * j + 2)
        far_update(sb_scr[...], 2 * j + 1)
        return carry

    n_far = jnp.maximum(qi - 1, 0)
    n_groups = lax.shift_right_logical(n_far + (FAR_TILES - 1), 2)
    lax.fori_loop(0, lax.shift_right_logical(n_groups + 1, 1), far_body, 0)
    o_sel = acc_ref[...] * (1.0 / l_ref[...])

    g = jax.nn.sigmoid(gate_ref[...])
    o = g[0:1, :] * ocmp_ref[...] + g[1:2, :] * o_sel + g[2:3, :] * o_win
    for r in range(Q_PER_KV):
        o_ref[r * HEAD_DIM:(r + 1) * HEAD_DIM, :] = o[:, r * TQ:(r + 1) * TQ].astype(o_ref.dtype)


def _nsa_prompt(q2, ksel, vsel, kwin, vwin, kc, vct, gates, tables, *, name):
    tzw, tzn, band, caug = tables
    B, G, nq, D, _ = q2.shape
    n_blocks = kc.shape[2] // (L_SEL // L_CMP)
    assert n_blocks % SUBLANES == 0 and nq % FAR_TILES == 0
    T = nq * TQ
    o_cmp, masks = _nsa_select(q2, kc, vct, band, name=name + "_select")
    whole = lambda arr: pl.BlockSpec((None, None) + arr.shape[2:], lambda b, g, i: (b, g, 0, 0, 0))
    tile = lambda shape: pl.BlockSpec((None, None, None) + shape, lambda b, g, i: (b, g, i, 0, 0))
    return pl.pallas_call(
        functools.partial(_nsa_prompt_body, n_blocks=n_blocks),
        grid=(B, G, nq),
        in_specs=[tile((D, WIDTH)),
                  whole(ksel), whole(vsel), whole(kwin), whole(vwin),
                  tile((D, WIDTH)), tile((n_blocks, TQ)), tile((3, WIDTH)),
                  pl.BlockSpec((None, None, WIN_TILES * TQ, WIDTH),
                               lambda b, g, i: (g, jnp.minimum(i, WIN_TILES - 1), 0, 0)),
                  pl.BlockSpec((None, None, 2 * TQ, WIDTH), lambda b, g, i: (g, jnp.minimum(i, 1), 0, 0)),
                  pl.BlockSpec((None, SUBLANES, WIDTH), lambda b, g, i: (g, 0, 0))],
        out_specs=pl.BlockSpec((None, Q_PER_KV * D, TQ), lambda b, g, i: (b, g, i)),
        out_shape=jax.ShapeDtypeStruct((B, G * Q_PER_KV * D, T), BF16),
        scratch_shapes=[pltpu.VMEM((LANES, WIDTH), BF16),
                        pltpu.VMEM((1, WIDTH), F32), pltpu.VMEM((1, WIDTH), F32),
                        pltpu.VMEM((D, WIDTH), F32),
                        pltpu.VMEM((n_blocks + SUBLANES, WIDTH), F32),
                        pltpu.VMEM((nq // FAR_TILES + 2, 2 * SUBLANES, WIDTH), F32),
                        pltpu.VMEM((FAR_TILES * TQ, WIDTH), F32), pltpu.VMEM((FAR_TILES * TQ, WIDTH), F32)],
        compiler_params=_cparams(("arbitrary", "arbitrary", "arbitrary")),
        name=name,
    )(q2, ksel, vsel, kwin, vwin, o_cmp, masks, gates, tzw, tzn, caug)


SSD_QUAD = 4

def _ssd_body(xbc_ref, dt_ref, z_ref, dtb_ref, alog_ref, dskip_ref, ng_ref,
              exp_h_ref, exp_p_ref, y_ref, st_ref, x_scr, colh_scr, colp_scr, dtp_scr, y_scr):
    c = pl.program_id(1)
    L = SSM_CHUNK
    P = SSM_HEADDIM
    heads_per_group = SSM_HEADS // SSM_GROUPS

    @pl.when(c == 0)
    def _():
        st_ref[...] = jnp.zeros(st_ref.shape, F32)

    xbc = xbc_ref[...]
    x_scr[...] = xbc[:, 0:D_INNER]
    bmat = [xbc[:, D_INNER + g * D_STATE:D_INNER + (g + 1) * D_STATE].astype(BF16) for g in range(SSM_GROUPS)]
    cmat = [xbc[:, D_INNER + (SSM_GROUPS + g) * D_STATE:D_INNER + (SSM_GROUPS + g + 1) * D_STATE].astype(BF16)
            for g in range(SSM_GROUPS)]

    x = dt_ref[:, 0:SSM_HEADS] + dtb_ref[...]
    dt = jnp.maximum(x, 0.0) + jnp.log1p(jnp.exp(-jnp.abs(x)))
    a = dt * (-jnp.exp(alog_ref[...]))
    tri = lax.broadcasted_iota(jnp.int32, (L, L), 0) >= lax.broadcasted_iota(jnp.int32, (L, L), 1)
    tri_b = tri.astype(BF16)
    a_hi, a_mid, a_lo = _split3(a)
    a_cs = _dot(tri_b, a_hi) + _dot(tri_b, a_mid) + _dot(tri_b, a_lo)
    acs_parts = jnp.concatenate(_split3(a_cs), axis=1)
    dt_parts = jnp.concatenate(_split3(dt), axis=1)
    colh_scr[...] = _dot(acs_parts, exp_h_ref[...])
    colp_scr[...] = _dot(acs_parts, exp_p_ref[...])
    dtp_scr[...] = _dot(dt_parts, exp_p_ref[...])
    acs_t = a_cs.T
    dt_t = dt.T
    tri_q = jnp.concatenate([tri] * SSD_QUAD, axis=1)

    for qd in range(SSM_HEADS // SSD_QUAD):
        g = (qd * SSD_QUAD) // heads_per_group
        h0 = qd * SSD_QUAD
        b_g, c_g = bmat[g], cmat[g]
        cb = _dot_nt(c_g, b_g)
        col = colh_scr[:, h0 * LANES:(h0 + SSD_QUAD) * LANES]
        row = jnp.concatenate([acs_t[h0 + i:h0 + i + 1, :] for i in range(SSD_QUAD)], axis=1)
        dtrow = jnp.concatenate([dt_t[h0 + i:h0 + i + 1, :] for i in range(SSD_QUAD)], axis=1)
        decay = jnp.exp(jnp.where(tri_q, col - row, NEG_INF))
        w = (jnp.concatenate([cb] * SSD_QUAD, axis=1) * decay * dtrow).astype(BF16)
        xq = x_scr[:, h0 * P:(h0 + SSD_QUAD) * P]
        xq_b = xq.astype(BF16)
        y_diag = jnp.concatenate([_dot(w[:, i * LANES:(i + 1) * LANES], xq_b[:, i * P:(i + 1) * P])
                                  for i in range(SSD_QUAD)], axis=1)
        h_prev = st_ref[h0:h0 + SSD_QUAD].reshape(SSD_QUAD * P, D_STATE)
        colp = colp_scr[:, h0 * P:(h0 + SSD_QUAD) * P]
        y_off = _dot_nt(c_g, h_prev.astype(BF16)) * jnp.exp(colp)
        y_scr[:, h0 * P:(h0 + SSD_QUAD) * P] = y_diag + y_off
        decay_end = jnp.exp(colp[L - 1:L, :] - colp)
        xw = (decay_end * dtp_scr[:, h0 * P:(h0 + SSD_QUAD) * P]) * xq
        st = _dot_tn(xw.astype(BF16), b_g)
        chunk_decay = jnp.exp(col[L - 1:L, :])
        cd = jnp.concatenate([jnp.broadcast_to(chunk_decay[:, i * LANES:(i + 1) * LANES], (P, D_STATE))
                              for i in range(SSD_QUAD)], axis=0)
        st_ref[h0:h0 + SSD_QUAD] = (h_prev * cd + st).reshape(SSD_QUAD, P, D_STATE)

    group_w = D_INNER // SSM_GROUPS
    y = y_scr[...] + dskip_ref[...] * x_scr[...]
    y = y * _silu(z_ref[...])
    for g in range(SSM_GROUPS):
        yg = y[:, g * group_w:(g + 1) * group_w]
        inv = lax.rsqrt(jnp.mean(yg * yg, axis=-1, keepdims=True) + EPS)
        y_ref[:, g * group_w:(g + 1) * group_w] = (
            yg * inv * ng_ref[:, g * group_w:(g + 1) * group_w]).astype(y_ref.dtype)


def _ssd_prompt(xbc, dtg, z, dt_bias, a_log, d_skip, norm_g, *, batch, name):
    M = xbc.shape[0]
    L = SSM_CHUNK
    nc = M // batch // L
    heads = jnp.arange(SSM_HEADS)

    def expand(lanes):
        e = (heads[:, None, None] == heads[None, :, None])
        e = jnp.broadcast_to(e, (SSM_HEADS, SSM_HEADS, lanes)).reshape(SSM_HEADS, SSM_HEADS * lanes)
        return jnp.concatenate([e] * 3, axis=0).astype(BF16)

    row = lambda v: v.reshape(1, -1)
    const2 = lambda shape: pl.BlockSpec(shape, lambda b, c: (0, 0))
    return pl.pallas_call(
        _ssd_body,
        grid=(batch, nc),
        in_specs=[pl.BlockSpec((L, CONV_DIM), lambda b, c: (b * nc + c, 0)),
                  pl.BlockSpec((L, LANES), lambda b, c: (b * nc + c, 1)),
                  pl.BlockSpec((L, D_INNER), lambda b, c: (b * nc + c, 0)),
                  const2((1, SSM_HEADS)), const2((1, SSM_HEADS)),
                  const2((1, D_INNER)), const2((1, D_INNER)),
                  const2((3 * SSM_HEADS, SSM_HEADS * LANES)), const2((3 * SSM_HEADS, D_INNER))],
        out_specs=[pl.BlockSpec((L, D_INNER), lambda b, c: (b * nc + c, 0)),
                   pl.BlockSpec((None, SSM_HEADS, SSM_HEADDIM, D_STATE), lambda b, c: (b, 0, 0, 0))],
        out_shape=[jax.ShapeDtypeStruct((M, D_INNER), BF16),
                   jax.ShapeDtypeStruct((batch, SSM_HEADS, SSM_HEADDIM, D_STATE), F32)],
        scratch_shapes=[pltpu.VMEM((L, D_INNER), F32),
                        pltpu.VMEM((L, SSM_HEADS * LANES), F32),
                        pltpu.VMEM((L, D_INNER), F32),
                        pltpu.VMEM((L, D_INNER), F32),
                        pltpu.VMEM((L, D_INNER), F32)],
        compiler_params=_cparams(("arbitrary", "arbitrary")),
        name=name,
    )(xbc, dtg, z, row(dt_bias), row(a_log),
      row(jnp.repeat(d_skip, SSM_HEADDIM)), row(norm_g), expand(LANES), expand(SSM_HEADDIM))


def _t5_bucket(dist):
    n = jnp.maximum(dist, 0)
    max_exact = N_BUCKETS // 2
    nf = jnp.maximum(n, max_exact).astype(F32)
    large = max_exact + (jnp.log(nf / max_exact) / math.log(MAX_DISTANCE / max_exact)
                         * (N_BUCKETS - max_exact)).astype(jnp.int32)
    return jnp.where(n < max_exact, n, jnp.minimum(large, N_BUCKETS - 1))


def _bias_tables(rel_bias):
    by_dist = rel_bias[_t5_bucket(jnp.arange((WIN_TILES + 1) * TQ))]
    qry = jnp.arange(TQ)[None, :]
    delta = (jnp.arange(WIN_TILES - 1, -1, -1) * TQ)[:, None]
    dist = delta - (TQ - 1) + jnp.arange(2 * TQ - 1)[None, :]
    ok = (dist >= 0) & (dist <= WINDOW)
    vals = jnp.where(ok[..., None], by_dist[jnp.clip(dist, 0)], NEG_INF).transpose(0, 2, 1)
    hank = jnp.tile(vals, (1, 1, TQ + 1))[..., :TQ * 2 * TQ].reshape(WIN_TILES, N_HEADS, TQ, 2 * TQ)
    tz = hank[:, :, ::-1, :TQ]
    tz = tz.reshape(WIN_TILES, N_KV, Q_PER_KV, TQ, TQ).transpose(1, 0, 3, 2, 4)
    tz = tz.reshape(N_KV, WIN_TILES, TQ, WIDTH)
    tile = jnp.arange(WIN_TILES)[None, :, None, None]

    def variants(tiles, n_var):
        n_tiles = tiles.shape[1]
        v = jnp.arange(n_var)[:, None, None, None]
        masked = tile[:, :n_tiles] < (n_tiles - 1 - v)
        out = jnp.where(masked[None], NEG_INF, tiles[:, None])
        return out.reshape(N_KV, n_var, n_tiles * TQ, WIDTH)

    tzw = variants(tz, WIN_TILES)
    tzn = variants(tz[:, WIN_TILES - 2:], 2)
    far = rel_bias[N_BUCKETS - 1]
    rel = jnp.arange(8)[:, None] - 4
    cdist = qry - L_CMP * rel - (L_CMP - 1)
    band = jnp.where((cdist >= 0)[..., None], by_dist[jnp.clip(cdist, 0)], NEG_INF)
    far_rows = jnp.broadcast_to(far[None, None, :], (1, TQ, N_HEADS))
    band = jnp.concatenate([band, far_rows, jnp.full((1, TQ, N_HEADS), NEG_INF, F32),
                            jnp.zeros((6, TQ, N_HEADS), F32)], axis=0)
    band = band.reshape(16, TQ, N_KV, Q_PER_KV).transpose(2, 0, 3, 1).reshape(N_KV, 16, WIDTH)
    parts = jnp.stack([p.astype(F32) for p in _split3(far)] + [jnp.zeros_like(far)] * (SUBLANES - 3))
    caug = jnp.broadcast_to(parts.reshape(SUBLANES, N_KV, Q_PER_KV, 1), (SUBLANES, N_KV, Q_PER_KV, TQ))
    caug = caug.transpose(1, 0, 2, 3).reshape(N_KV, SUBLANES, WIDTH)
    return tzw, tzn, band, caug


def _layer_weights(l, w_ada, w_in, cmp_pe, cmp_w1, cmp_w2, w_a, w_b, w_o, w_gu, w_down):
    wi = w_in[l]
    small = jnp.zeros((D_MODEL, 2 * LANES), F32)
    small = small.at[:, 0:NSA_GATE_COLS].set(wi[:, OFF_G:OFF_Z])
    small = small.at[:, LANES:LANES + SSM_HEADS].set(wi[:, OFF_DT:OFF_MG])
    c = lambda w: w.astype(BF16)
    w1r = cmp_w1[l].reshape(2, L_CMP, HEAD_DIM, CMP_HID)
    w1_pair = jnp.zeros((2, L_CMP, 2 * HEAD_DIM, 2 * CMP_HID), F32)
    w1_pair = w1_pair.at[:, :, 0:HEAD_DIM, 0:CMP_HID].set(w1r).at[:, :, HEAD_DIM:, CMP_HID:].set(w1r)
    pe_pair = jnp.tile(cmp_pe[l], (1, 1, 2))
    return dict(ada=c(w_ada[l]), q=c(wi[:, 0:OFF_KV]), kv=c(wi[:, OFF_KV:OFF_G]), small=c(small),
                q_t=c(wi[:, 0:OFF_KV].T), kvrows_t=c(wi[:, OFF_KV:OFF_KV + 4 * N_KV * HEAD_DIM].T),
                kvwin_t=c(wi[:, OFF_KV + 4 * N_KV * HEAD_DIM:OFF_G].T),
                w1_pair=c(w1_pair), pe_pair=pe_pair,
                z=c(wi[:, OFF_Z:OFF_XBC]), xbc=c(wi[:, OFF_XBC:OFF_DT]), mg=c(wi[:, OFF_MG:P_IN]),
                cmp_w1=c(cmp_w1[l]), cmp_w2=c(cmp_w2[l]), a=c(w_a[l]), b=c(w_b[l]), o=c(w_o[l]),
                gu=c(w_gu[l]), down=c(w_down[l]))


def _prompt_layer(x, mod, l, Wl, P, tables, batch, T, Cp):
    sh1, sc1, g1, sh2, sc2, g2 = [m.reshape(batch, 1, D_MODEL) for m in jnp.split(mod, 6, axis=-1)]
    tm = 1024 if T % 1024 == 0 else 512
    gd = N_KV * HEAD_DIM
    h = _norm_mod(x, P['norm1_g'][l], sc1, sh1, tm=512, rows_per_batch=T, name=f"norm1_p{l}")
    q_t = _mm_t(h, Wl['q_t'], tm=tm, tn=512, rows_per_batch=T, name=f"proj_q_p{l}", out_dtype=BF16,
                scale=SCALE * LOG2E)
    q = q_t.transpose(0, 2, 1).reshape(batch * T, Q_COLS)
    Cp['kv_out'] = _mm_t(h, Wl['kvrows_t'], tm=tm, tn=512, rows_per_batch=T, name=f"proj_kv_p{l}", layer=l,
                         out_buf=Cp.get('kv_out'))
    win_t = _mm_t(h, Wl['kvwin_t'], tm=tm, tn=512, rows_per_batch=T, name=f"proj_win_p{l}")

    def slot_rows(slot):
        src, s0 = (Cp['kv_out'][l], slot * gd) if slot < 4 else (win_t, (slot - 4) * gd)
        return src[:, s0:s0 + gd, :].reshape(batch, N_KV, HEAD_DIM, T).transpose(0, 3, 1, 2)

    small = _mm(h, Wl['small'], tm=tm, tn=2 * LANES, name=f"proj_small_p{l}")
    z = _mm(h, Wl['z'], tm=tm, tn=512, name=f"proj_z_p{l}")
    xbc = _proj_conv(h, Wl['xbc'], P['ssm_conv_w'][l], P['ssm_conv_b'][l], tm=512, tn=1024, rows_per_batch=T,
                     name=f"proj_xbc_p{l}")
    gates = _mm(h, Wl['mg'], tm=tm, tn=512, name=f"proj_mg_p{l}", epilogue=jax.nn.sigmoid)

    o_nsa = _prompt_attention(q, slot_rows, small, l, Wl, P, tables, batch, T)

    y, ssm_state = _ssd_prompt(xbc, small, z, P['dt_bias'][l], P['a_log'][l], P['d_skip'][l], P['ssm_norm_g'][l],
                               batch=batch, name=f"ssd_p{l}")

    u = _merge(o_nsa, y, Wl['a'], Wl['b'], gates, tm=tm, tn=512, name=f"merge_p{l}")
    x = _mm(u, Wl['o'], tm=tm, tn=512, name=f"proj_o_p{l}", rows_per_batch=T,
            epilogue=lambda acc, xr, gr: xr + gr * acc, extras=[('full', x), ('batch', g1)])

    h2 = _norm_mod(x, P['norm2_g'][l], sc2, sh2, tm=512, rows_per_batch=T, name=f"norm2_p{l}")
    act = _ffn_up(h2, Wl['gu'], P['ffn_conv_w'][l], P['ffn_conv_b'][l], tm=512, tn=D_FF // 2,
                  rows_per_batch=T, name=f"ffn_up_p{l}")
    h2_last = h2.reshape(batch, T, D_MODEL)[:, T - SUBLANES:].reshape(batch * SUBLANES, D_MODEL)
    gate_last = _mm(h2_last, Wl['gu'][:, 0:D_FF], tm=batch * SUBLANES, tn=D_FF // 2, name=f"ffn_state_p{l}")
    ffn_state = gate_last.reshape(batch, SUBLANES, D_FF)[:, SUBLANES - (FFN_CONV - 1):]
    x = _mm(act, Wl['down'], tm=tm, tn=512, name=f"ffn_down_p{l}", rows_per_batch=T,
            epilogue=lambda acc, xr, gr: xr + gr * acc, extras=[('full', x), ('batch', g2)])

    wlen = min(WINDOW, T)
    win_state = win_t[:, :, T - wlen:].reshape(batch, 2, N_KV, HEAD_DIM, wlen).transpose(0, 4, 1, 2, 3)
    h_last = h.reshape(batch, T, D_MODEL)[:, T - SUBLANES:].reshape(batch * SUBLANES, D_MODEL)
    xbc_last = _mm(h_last, Wl['xbc'], tm=batch * SUBLANES, tn=1024, name=f"conv_state_p{l}")
    conv_state = xbc_last.reshape(batch, SUBLANES, CONV_DIM)[:, SUBLANES - (SSM_CONV - 1):]
    return x, (win_state, conv_state, ssm_state, ffn_state)


def _prompt_attention(q, slot_rows, small, l, Wl, P, tables, batch, T):
    nq = T // TQ
    n_cmp = T // L_CMP

    def cmp_in(slot):
        xb = slot_rows(slot).reshape(batch, n_cmp, L_CMP, N_KV, HEAD_DIM).transpose(0, 3, 1, 2, 4)
        return xb.reshape(batch * N_KV * n_cmp, L_CMP * HEAD_DIM)

    ctm = min(512, batch * N_KV * n_cmp)
    kc = _compress(cmp_in(0), P['cmp_pe'][l, 0], Wl['cmp_w1'][0], Wl['cmp_w2'][0], tm=ctm, name=f"cmp_k_p{l}")
    vc = _compress(cmp_in(1), P['cmp_pe'][l, 1], Wl['cmp_w1'][1], Wl['cmp_w2'][1], tm=ctm, name=f"cmp_v_p{l}")

    def even_odd(c):
        c = c.reshape(batch, N_KV, n_cmp // 2, 2, HEAD_DIM).transpose(0, 1, 3, 2, 4)
        return c.reshape(batch, N_KV, n_cmp, HEAD_DIM)

    kc = even_odd(kc.reshape(batch, N_KV, n_cmp, HEAD_DIM)).astype(BF16)
    vct = even_odd(vc.reshape(batch, N_KV, n_cmp, HEAD_DIM)).astype(BF16).transpose(0, 1, 3, 2)

    def k_tiles(slot, front, back, aug):
        k = slot_rows(slot).astype(BF16).reshape(batch, nq, TQ, N_KV, HEAD_DIM).transpose(0, 3, 1, 2, 4)
        k = jnp.pad(k, ((0, 0), (0, 0), (front, back), (0, 0), (0, LANES - HEAD_DIM)))
        if aug:
            kt = jnp.arange(front + nq + back) - front
            blk_col = AUG_ROW0 + 2 * (kt % FAR_TILES)[:, None] + (jnp.arange(TQ) // L_SEL)[None, :]
            lane = jnp.arange(LANES)[None, None, :]
            ones = (lane == blk_col[..., None]) | ((lane >= AUG_ROW0 + SUBLANES) & (lane < AUG_ROW0 + SUBLANES + 3))
            k = jnp.where(ones[None, None], jnp.ones((), BF16), k)
        return k

    def vt_tiles(slot, front, back):
        v = slot_rows(slot).astype(BF16).reshape(batch, nq, TQ, N_KV, HEAD_DIM).transpose(0, 3, 1, 4, 2)
        v = jnp.pad(v, ((0, 0), (0, 0), (front, back), (0, BF16_ROWS), (0, 0)))
        return v.at[:, :, :, HEAD_DIM, :].set(1.0)

    q2 = q.reshape(batch, nq, TQ, N_KV, Q_PER_KV, HEAD_DIM).transpose(0, 3, 1, 5, 4, 2)
    q2 = q2.reshape(batch, N_KV, nq, HEAD_DIM, WIDTH)
    ng = small[:, 0:NSA_GATE_COLS].reshape(batch, nq, TQ, 3, N_KV, Q_PER_KV).transpose(0, 4, 1, 3, 5, 2)
    ng = ng.reshape(batch, N_KV, nq, 3, WIDTH)
    o_t = _nsa_prompt(q2, k_tiles(2, 1, 2 * FAR_TILES, True), vt_tiles(3, 1, 2 * FAR_TILES),
                      k_tiles(4, WIN_TILES - 1, 0, False), vt_tiles(5, WIN_TILES - 1, 0),
                      kc, vct, ng, tables, name=f"nsa_p{l}")
    return o_t.transpose(0, 2, 1).reshape(batch * T, Q_COLS)


def _pool_compress_body(x_ref, pe_ref, w1_ref, w2_ref, o_ref, rows_scr, *, pages, page):
    rows = pages * page // L_CMP
    for s in range(2):
        outs = []
        for pair in range(N_KV // 2):
            for p in range(pages):
                rows_scr[p * page:(p + 1) * page, :] = x_ref[p, s, 2 * pair:2 * pair + 2].reshape(
                    2 * HEAD_DIM, page).T
            xs = [(rows_scr[pl.ds(l, rows, stride=L_CMP), :] + pe_ref[s, l:l + 1, :]).astype(BF16)
                  for l in range(L_CMP)]
            w1 = w1_ref[s].reshape(L_CMP * 2 * HEAD_DIM, 2 * CMP_HID)
            hid = _silu(_dot(jnp.concatenate(xs, axis=1), w1)).astype(BF16)
            for gg in range(2):
                outs.append(_dot(hid[:, gg * CMP_HID:(gg + 1) * CMP_HID], w2_ref[s]))
        o_ref[:, s * N_KV * HEAD_DIM:(s + 1) * N_KV * HEAD_DIM] = jnp.concatenate(outs, axis=1)


def _pool_compress(cache_t, l, pe2, w1pair, w2, *, pages_per_step, name):
    n_pool, page = cache_t.shape[1], cache_t.shape[5]
    rows_out = pages_per_step * page // L_CMP
    half = 2 * N_KV * HEAD_DIM
    return pl.pallas_call(
        functools.partial(_pool_compress_body, pages=pages_per_step, page=page),
        grid=(n_pool // pages_per_step,),
        in_specs=[pl.BlockSpec((None, pages_per_step, 2, N_KV, HEAD_DIM, page), lambda i: (l, i, 0, 0, 0, 0)),
                  pl.BlockSpec(pe2.shape, lambda i: (0, 0, 0)),
                  pl.BlockSpec(w1pair.shape, lambda i: (0, 0, 0, 0)),
                  pl.BlockSpec(w2.shape, lambda i: (0, 0, 0))],
        out_specs=pl.BlockSpec((rows_out, half), lambda i: (i, 0)),
        out_shape=jax.ShapeDtypeStruct((n_pool * page // L_CMP, half), F32),
        scratch_shapes=[pltpu.VMEM((pages_per_step * page, 2 * HEAD_DIM), F32)],
        compiler_params=_cparams(("parallel",)),
        name=name,
    )(cache_t, pe2, w1pair, w2)


def _softmax_lanes(s):
    m = jnp.max(s, axis=1, keepdims=True)
    e = jnp.exp(s - m)
    return m, e, jnp.sum(e, axis=1, keepdims=True)


def _sum3(lhs_bf16, x):
    hi, mid, lo = _split3(x)
    return _dot(lhs_bf16, hi) + _dot(lhs_bf16, mid) + _dot(lhs_bf16, lo)


def _nsa_decode_body(pt_ref, q_ref, gate_ref, kvnew_ref, kcn_ref, wincol_ref, *refs,
                     n_pages, n_blocks, top_n, win_len, has_prev):
    sel_pages = refs[0:n_pages]
    kc_pages = refs[n_pages:2 * n_pages]
    rest = refs[2 * n_pages:]
    win_ref, bsel_ref, bwin_ref, bcmp_ref, e_ref, pair_ref, rsum_ref, rexp_ref = rest[0:8]
    o_ref, wout_ref, kc_scr, s_scr = rest[8 + int(has_prev):]
    gd = N_KV * HEAD_DIM

    last = lax.broadcasted_iota(jnp.int32, (gd, win_len), 1) == win_len - 1
    for s in range(2):
        shifted = pltpu.roll(win_ref[s].reshape(gd, win_len), win_len - 1, axis=1)
        wout_ref[s] = jnp.where(last, wincol_ref[s * gd:(s + 1) * gd, :], shifted).reshape(
            N_KV, HEAD_DIM, win_len)
    page = sel_pages[0].shape[3]
    blocks_per_page = kc_pages[0].shape[0]
    n_past_blocks = blocks_per_page * n_pages

    @pl.when(pl.program_id(0) == 0)
    def _():
        kc_scr[...] = jnp.zeros(kc_scr.shape, F32)

    q = q_ref[...]
    q_f = q.astype(F32)
    lane0 = lax.broadcasted_iota(jnp.int32, (N_HEADS, LANES), 1) == 0
    rb = lambda v: v.astype(BF16).astype(F32)

    def new_scores(slot):
        k_new = rb(kvnew_ref[:, slot * gd:(slot + 1) * gd])
        return jnp.where(lane0, jnp.sum(q_f * k_new, axis=1, keepdims=True), 0.0)

    def new_value(e_tile, slot):
        return rb(e_tile[:, 0:1]) * rb(kvnew_ref[:, slot * gd:(slot + 1) * gd])

    for p in range(n_pages):
        kc_scr[p * blocks_per_page:(p + 1) * blocks_per_page, :] = kc_pages[p][...]
    kc_scr[n_past_blocks:n_past_blocks + SUBLANES, :] = kcn_ref[...]
    kcv = kc_scr[...]
    m_c, e_c, den_c = _softmax_lanes(_dot_nt(q, kcv[:, 0:gd].astype(BF16)) + bcmp_ref[...])
    p_c = e_c * ((m_c > 0.5 * NEG_INF).astype(F32) / den_c)
    o_cmp = _dot(p_c.astype(BF16), kcv[:, gd:2 * gd].astype(BF16))
    imp = _sum3(rsum_ref[...], _sum3_rhs(p_c, pair_ref[...]))

    lane_i = lax.broadcasted_iota(jnp.int32, (SUBLANES, LANES), 1)
    forced = (lane_i == 0) | (lane_i == n_blocks - 1)
    score = jnp.where(lane_i < n_blocks, jnp.where(forced, 1e30, imp), -1.0)
    score_t = score.T
    ii = lax.broadcasted_iota(jnp.int32, (LANES, LANES), 0)
    jj = lax.broadcasted_iota(jnp.int32, (LANES, LANES), 1)
    sel_rows = []
    for g in range(N_KV):
        col = jnp.broadcast_to(score_t[:, g:g + 1], (LANES, LANES))
        rowv = jnp.broadcast_to(score[g:g + 1, :], (LANES, LANES))
        ahead = (col > rowv) | ((col == rowv) & (ii < jj))
        rank = jnp.sum(ahead.astype(F32), axis=0, keepdims=True)
        sel_rows.append((rank < float(top_n)).astype(F32))
    sel = jnp.concatenate(sel_rows + [jnp.zeros((SUBLANES - N_KV, LANES), F32)], axis=0)
    sel = jnp.where(lane_i < n_blocks, sel, 0.0)
    sel_h = _dot(rexp_ref[...], sel.astype(BF16))
    sel_keys = _dot(sel_h.astype(BF16), e_ref[...])

    for p in range(n_pages):
        s_scr[:, p * page:(p + 1) * page] = _dot(q, sel_pages[p][0].reshape(gd, page).astype(BF16))
    s_scr[:, n_pages * page:n_pages * page + LANES] = new_scores(2)
    s = jnp.where(sel_keys > 0.5, s_scr[...] + bsel_ref[...], NEG_INF)
    _, e_s, den_s = _softmax_lanes(s)
    e_b = e_s.astype(BF16)
    o_sel = new_value(e_s[:, n_pages * page:n_pages * page + LANES], 3)
    for p in range(n_pages):
        o_sel = o_sel + _dot_nt(e_b[:, p * page:(p + 1) * page], sel_pages[p][1].reshape(gd, page).astype(BF16))
    o_sel = o_sel * (1.0 / den_s)

    s_w = jnp.concatenate([_dot(q, win_ref[0].reshape(gd, win_len).astype(BF16)), new_scores(4)], axis=1)
    _, e_w, den_w = _softmax_lanes(s_w + bwin_ref[...])
    o_win = (_dot_nt(e_w[:, 0:win_len].astype(BF16), win_ref[1].reshape(gd, win_len).astype(BF16))
             + new_value(e_w[:, win_len:win_len + LANES], 5)) * (1.0 / den_w)

    g = jax.nn.sigmoid(gate_ref[...])
    o = g[:, 0:1] * o_cmp + g[:, 1:2] * o_sel + g[:, 2:3] * o_win
    own = (lax.broadcasted_iota(jnp.int32, (N_HEADS, gd), 1) // HEAD_DIM
           == lax.broadcasted_iota(jnp.int32, (N_HEADS, gd), 0) // Q_PER_KV)
    o = jnp.where(own, o, 0.0)
    acc = o[:, 0:HEAD_DIM]
    for gi in range(1, N_KV):
        acc = acc + o[:, gi * HEAD_DIM:(gi + 1) * HEAD_DIM]
    o_ref[...] = acc


def _sum3_rhs(x, rhs_bf16):
    hi, mid, lo = _split3(x)
    return _dot(hi, rhs_bf16) + _dot(mid, rhs_bf16) + _dot(lo, rhs_bf16)


def _nsa_decode(page_table, q_bd, gates_t, kvnew, kcn, cache_t, kcpool, win_t, win_out, l, tabs, *, name):
    B, n_pages = page_table.shape
    page = cache_t.shape[5]
    gd = N_KV * HEAD_DIM
    win_len = win_t.shape[5]
    bsel, bwin, bcmp, e_mat, pair, rsum, rexp, n_blocks = tabs
    blocks_per_page = kcpool.shape[1]
    const = lambda a: pl.BlockSpec(a.shape, lambda b, pt: (0,) * a.ndim)
    sel_specs = [pl.BlockSpec((None, None, 2, N_KV, HEAD_DIM, page), lambda b, pt, p=p: (l, pt[b, p], 1, 0, 0, 0))
                 for p in range(n_pages)]
    kc_specs = [pl.BlockSpec((None, blocks_per_page, 2 * gd), lambda b, pt, p=p: (pt[b, p], 0, 0))
                for p in range(n_pages)]
    win_spec = pl.BlockSpec((None, None, 2, N_KV, HEAD_DIM, win_len), lambda b, pt: (l, b, 0, 0, 0, 0))
    ops = [page_table, q_bd, gates_t, kvnew, kcn, kvnew[:, 0, 4 * gd:].reshape(B, 2 * gd, 1),
           *([cache_t] * n_pages), *([kcpool] * n_pages), win_t, bsel, bwin, bcmp, e_mat, pair, rsum, rexp]
    in_specs = ([pl.BlockSpec((None, N_HEADS, gd), lambda b, pt: (b, 0, 0)),
                 pl.BlockSpec((None, N_HEADS, 3), lambda b, pt: (b, 0, 0)),
                 pl.BlockSpec((None, 1, kvnew.shape[2]), lambda b, pt: (b, 0, 0)),
                 pl.BlockSpec((None, SUBLANES, 2 * gd), lambda b, pt: (b, 0, 0)),
                 pl.BlockSpec((None, 2 * gd, 1), lambda b, pt: (b, 0, 0))]
                + sel_specs + kc_specs
                + [win_spec, const(bsel), const(bwin), const(bcmp), const(e_mat), const(pair), const(rsum),
                   const(rexp)])
    aliases = {}
    if win_out is not None:
        aliases = {len(ops): 1}
        ops.append(win_out)
        in_specs.append(pl.BlockSpec(memory_space=pl.ANY))
    grid_spec = pltpu.PrefetchScalarGridSpec(
        num_scalar_prefetch=1,
        grid=(B,),
        in_specs=in_specs,
        out_specs=[pl.BlockSpec((None, N_HEADS, HEAD_DIM), lambda b, pt: (b, 0, 0)), win_spec],
        scratch_shapes=[pltpu.VMEM((LANES, 2 * gd), F32),
                        pltpu.VMEM((N_HEADS, n_pages * page + LANES), F32)])
    return pl.pallas_call(
        functools.partial(_nsa_decode_body, n_pages=n_pages, n_blocks=n_blocks,
                          top_n=min(TOP_N, n_blocks), win_len=win_len, has_prev=win_out is not None),
        grid_spec=grid_spec,
        out_shape=[jax.ShapeDtypeStruct((B, N_HEADS, HEAD_DIM), F32), jax.ShapeDtypeStruct(win_t.shape, F32)],
        input_output_aliases=aliases,
        compiler_params=_cparams(("arbitrary",)),
        name=name,
    )(*ops)


def _decode_tables(rel_bias, past_len, n_pages, win_len):
    q_pos = past_len
    n_blocks = (past_len + 1 + L_SEL - 1) // L_SEL
    n_cmp = n_blocks * (L_SEL // L_CMP)
    by_dist = rel_bias[_t5_bucket(jnp.arange(q_pos + 1))].T
    keys = (n_pages + 1) * LANES
    kpos = jnp.arange(keys)
    bsel = jnp.where((kpos <= q_pos)[None], by_dist[:, jnp.clip(q_pos - kpos, 0)], NEG_INF)
    wl = jnp.arange(win_len + LANES)
    wdist = win_len - wl
    bwin = jnp.where((wdist >= 0)[None], by_dist[:, jnp.clip(wdist, 0, q_pos)], NEG_INF)
    c = jnp.arange(LANES)
    cdist = q_pos - (c * L_CMP + L_CMP - 1)
    bcmp = jnp.where(((cdist >= 0) & (c < n_cmp))[None], by_dist[:, jnp.clip(cdist, 0)], NEG_INF)
    e_mat = (jnp.arange(LANES)[:, None] == (kpos // L_SEL)[None, :]).astype(BF16)
    pair = ((c[:, None] // (L_SEL // L_CMP) == c[None, :]) & (c[:, None] < n_cmp)).astype(BF16)
    heads = jnp.arange(N_HEADS)
    rsum = (jnp.arange(SUBLANES)[:, None] == (heads // Q_PER_KV)[None, :]).astype(BF16)
    return bsel, bwin, bcmp, e_mat, pair, rsum, rsum.T, n_blocks


def _conv_step_body(x_ref, p0_ref, p1_ref, p2_ref, cw_ref, cb_ref, dt_ref, dtb_ref, alog_ref,
                    xact_ref, dto_ref, dec_ref):
    y = cb_ref[...]
    for k, ref in enumerate((p0_ref, p1_ref, p2_ref, x_ref)):
        y = y + cw_ref[k:k + 1, :] * ref[...]
    xact_ref[...] = _silu(y)
    x = dt_ref[:, 0:SSM_HEADS] + dtb_ref[...]
    dt = jnp.maximum(x, 0.0) + jnp.log1p(jnp.exp(-jnp.abs(x)))
    dto_ref[...] = dt
    dec_ref[...] = jnp.exp(dt * (-jnp.exp(alog_ref[...])))


def _conv_step(xbc, prev, conv_w, conv_b, dtg, dt_bias, a_log, *, name):
    B = xbc.shape[0]
    full = lambda a: pl.BlockSpec(a.shape, lambda i: (0,) * a.ndim)
    ops = [xbc, prev[:, 0], prev[:, 1], prev[:, 2], conv_w, conv_b.reshape(1, -1), dtg,
           dt_bias.reshape(1, -1), a_log.reshape(1, -1)]
    in_specs = [full(a) for a in ops]
    in_specs[6] = pl.BlockSpec((B, LANES), lambda i: (0, 1))
    return pl.pallas_call(
        _conv_step_body,
        grid=(1,),
        in_specs=in_specs,
        out_specs=[pl.BlockSpec((B, CONV_DIM), lambda i: (0, 0)), pl.BlockSpec((B, SSM_HEADS), lambda i: (0, 0)),
                   pl.BlockSpec((B, SSM_HEADS), lambda i: (0, 0))],
        out_shape=[jax.ShapeDtypeStruct((B, CONV_DIM), F32), jax.ShapeDtypeStruct((B, SSM_HEADS), F32),
                   jax.ShapeDtypeStruct((B, SSM_HEADS), F32)],
        compiler_params=_cparams(("arbitrary",)),
        name=name,
    )(*ops)


def _ssd_step_body(dts_ref, decs_ref, h0_ref, xs_ref, xst_ref, bm_ref, cm_ref, z_ref, dtc_ref, dskip_ref, ng_ref,
                   *refs, has_prev):
    y_ref, st_ref, yoff_scr = refs[int(has_prev):]
    b = pl.program_id(0)
    heads_per_group = SSM_HEADS // SSM_GROUPS
    bm = bm_ref[...]
    cm = cm_ref[...]
    rb = lambda v: v.astype(BF16).astype(F32)
    cb = jnp.sum(rb(cm) * rb(bm), axis=-1, keepdims=True)
    cm8 = jnp.concatenate([cm, jnp.zeros((SUBLANES - SSM_GROUPS, D_STATE), F32)], axis=0).astype(BF16)
    for h in range(SSM_HEADS):
        g = h // heads_per_group
        h_prev = h0_ref[h]
        dec = decs_ref[b, h]
        y_off = _dot_nt(cm8, h_prev.astype(BF16))
        yoff_scr[h:h + 1, :] = y_off[g:g + 1, :] * dec
        xdt = xst_ref[:, h:h + 1] * dts_ref[b, h]
        st_ref[h] = h_prev * dec + xdt * bm[g:g + 1, :]
    xs = xs_ref[...]
    cbh = jnp.concatenate([jnp.broadcast_to(cb[g:g + 1, :], (heads_per_group, 1)) for g in range(SSM_GROUPS)], axis=0)
    y = (cbh * dtc_ref[...]) * xs + yoff_scr[...] + dskip_ref[...] * xs
    y = y * _silu(z_ref[...])
    for g in range(SSM_GROUPS):
        yg = y[g * heads_per_group:(g + 1) * heads_per_group, :]
        ms = jnp.sum(jnp.sum(yg * yg, axis=1, keepdims=True), axis=0, keepdims=True) * (1.0 / (heads_per_group * SSM_HEADDIM))
        y_ref[g * heads_per_group:(g + 1) * heads_per_group, :] = (
            yg * lax.rsqrt(ms + EPS) * ng_ref[g * heads_per_group:(g + 1) * heads_per_group, :]).astype(y_ref.dtype)


def _ssd_step(state, state_out, l, xact, dt, dec, z, d_skip, norm_g, *, name):
    B = xact.shape[0]
    H, Pd, N = SSM_HEADS, SSM_HEADDIM, D_STATE
    xs = xact[:, 0:D_INNER].reshape(B, H, Pd)
    bm = xact[:, D_INNER:D_INNER + SSM_GROUPS * N].reshape(B, SSM_GROUPS, N)
    cm = xact[:, D_INNER + SSM_GROUPS * N:].reshape(B, SSM_GROUPS, N)
    per_b = lambda shape: pl.BlockSpec((None,) + shape, lambda b, *_: (b,) + (0,) * len(shape))
    const = lambda shape: pl.BlockSpec(shape, lambda b, *_: (0,) * len(shape))
    state_spec = pl.BlockSpec((None, None, H, Pd, N), lambda b, *_: (l, b, 0, 0, 0))
    ops = [dt, dec, state, xs, xs.transpose(0, 2, 1), bm, cm, z.reshape(B, H, Pd), dt.reshape(B, H, 1),
           d_skip.reshape(H, 1), norm_g.reshape(H, Pd)]
    in_specs = [state_spec, per_b((H, Pd)), per_b((Pd, H)), per_b((SSM_GROUPS, N)), per_b((SSM_GROUPS, N)),
                per_b((H, Pd)), per_b((H, 1)), const((H, 1)), const((H, Pd))]
    aliases = {}
    if state_out is not None:
        aliases = {len(ops): 1}
        ops.append(state_out)
        in_specs.append(pl.BlockSpec(memory_space=pl.ANY))
    grid_spec = pltpu.PrefetchScalarGridSpec(
        num_scalar_prefetch=2,
        grid=(B,),
        in_specs=in_specs,
        out_specs=[per_b((H, Pd)), state_spec],
        scratch_shapes=[pltpu.VMEM((H, Pd), F32)])
    return pl.pallas_call(
        functools.partial(_ssd_step_body, has_prev=state_out is not None),
        grid_spec=grid_spec,
        out_shape=[jax.ShapeDtypeStruct((B, H, Pd), BF16), jax.ShapeDtypeStruct(state.shape, F32)],
        input_output_aliases=aliases,
        compiler_params=_cparams(("arbitrary",)),
        name=name,
    )(*ops)


def _ffn_up_step_body(h_ref, wg_ref, wu_ref, p0_ref, p1_ref, cw_ref, cb_ref, act_ref, gate_ref):
    h = h_ref[...]
    gate = _dot(h, wg_ref[...])
    up = _dot(h, wu_ref[...])
    y = cb_ref[...] + cw_ref[0:1, :] * p0_ref[...] + cw_ref[1:2, :] * p1_ref[...] + cw_ref[2:3, :] * gate
    gate_ref[...] = gate
    act_ref[...] = (_silu(y) * up).astype(act_ref.dtype)


def _ffn_up_step(h, w_gu, prev, conv_w, conv_b, *, tn, name):
    B, K = h.shape
    nj = D_FF // tn
    col = lambda r: pl.BlockSpec((r, tn), lambda j: (0, j))
    return pl.pallas_call(
        _ffn_up_step_body,
        grid=(nj,),
        in_specs=[pl.BlockSpec((B, K), lambda j: (0, 0)), col(K),
                  pl.BlockSpec((K, tn), lambda j, nj=nj: (0, j + nj)),
                  col(B), col(B), col(FFN_CONV), col(1)],
        out_specs=[col(B), col(B)],
        out_shape=[jax.ShapeDtypeStruct((B, D_FF), BF16), jax.ShapeDtypeStruct((B, D_FF), F32)],
        compiler_params=_cparams(("parallel",)),
        name=name,
    )(h, w_gu, w_gu, prev[:, 0], prev[:, 1], conv_w, conv_b.reshape(1, D_FF))


def _pages_per_step(n_pool, limit=32):
    return max(p for p in range(2, limit + 1, 2) if n_pool % p == 0)


def _sample_layer(x, mod, l, Wl, P, C):
    B = x.shape[0]
    gd = N_KV * HEAD_DIM
    sh1, sc1, g1, sh2, sc2, g2 = jnp.split(mod, 6, axis=-1)
    h = _norm_mod(x, P['norm1_g'][l], sc1, sh1, tm=B, rows_per_batch=1, name=f"norm1_s{l}")
    q = _mm(h, Wl['q'], tm=B, tn=512, name=f"proj_q_s{l}", out_dtype=BF16, epilogue=lambda acc: acc * SCALE)
    kv = _mm(h, Wl['kv'], tm=B, tn=512, name=f"proj_kv_s{l}")
    small = _mm(h, Wl['small'], tm=B, tn=2 * LANES, name=f"proj_small_s{l}")
    z = _mm(h, Wl['z'], tm=B, tn=512, name=f"proj_z_s{l}")
    xbc = _mm(h, Wl['xbc'], tm=B, tn=512, name=f"proj_xbc_s{l}")
    gates = _mm(h, Wl['mg'], tm=B, tn=512, name=f"proj_mg_s{l}", epilogue=jax.nn.sigmoid)

    cache_kv, page_table = C['cache_kv'], C['page_table']
    n_pool, page = cache_kv.shape[1], cache_kv.shape[2]
    cache_t = cache_kv.transpose(0, 1, 3, 4, 5, 2)
    win_t = C['cache_win_kv'].transpose(0, 1, 3, 4, 5, 2)
    kcpool = _pool_compress(cache_t, l, Wl['pe_pair'], Wl['w1_pair'], Wl['cmp_w2'],
                            pages_per_step=_pages_per_step(n_pool), name=f"cmp_pool_s{l}")
    kcpool = kcpool.reshape(n_pool, page // L_CMP, 2 * gd)

    def new_blocks(slot):
        first = jnp.pad(kv[:, slot * gd:(slot + 1) * gd].reshape(B, N_KV, 1, HEAD_DIM),
                        ((0, 0), (0, 0), (0, 1), (0, (L_CMP - 1) * HEAD_DIM)))
        rows = B * N_KV * 2
        out = _compress(first.reshape(rows, L_CMP * HEAD_DIM), P['cmp_pe'][l, slot], Wl['cmp_w1'][slot],
                        Wl['cmp_w2'][slot], tm=512 if rows % 512 == 0 else rows, name=f"cmp{slot}_new_s{l}")
        return out.reshape(B, N_KV, 2, HEAD_DIM).transpose(0, 2, 1, 3).reshape(B, 2, gd)

    kcn = jnp.pad(jnp.concatenate([new_blocks(0), new_blocks(1)], axis=-1), ((0, 0), (0, SUBLANES - 2), (0, 0)))
    qh = q.reshape(B, N_KV, Q_PER_KV, 1, HEAD_DIM)
    q_bd = (qh * jnp.eye(N_KV, dtype=BF16)[None, :, None, :, None]).reshape(B, N_HEADS, gd)
    gates_t = small[:, 0:NSA_GATE_COLS].reshape(B, 3, N_HEADS).transpose(0, 2, 1)
    o_nsa, C['win_out'] = _nsa_decode(page_table, q_bd, gates_t, kv.reshape(B, 1, KV_COLS), kcn, cache_t, kcpool,
                                      win_t, C.get('win_out'), l, C['tables'], name=f"nsa_s{l}")
    o_nsa = o_nsa.reshape(B, Q_COLS)
    new_rows = kv[:, 0:4 * gd].reshape(B, 1, 4, N_KV, HEAD_DIM)

    conv_prev = C['state_ssm_conv'][l]
    xact, dt, dec = _conv_step(xbc, conv_prev, P['ssm_conv_w'][l], P['ssm_conv_b'][l], small,
                               P['dt_bias'][l], P['a_log'][l], name=f"conv_s{l}")
    conv_state = jnp.concatenate([conv_prev[:, 1:], xbc[:, None]], axis=1)
    y, C['ssm_out'] = _ssd_step(C['state_ssm'], C.get('ssm_out'), l, xact, dt, dec, z, P['d_skip'][l],
                                P['ssm_norm_g'][l], name=f"ssd_s{l}")

    u = _merge(o_nsa.astype(BF16), y.reshape(B, D_INNER), Wl['a'], Wl['b'], gates, tm=B, tn=512, name=f"merge_s{l}")
    x = _mm(u, Wl['o'], tm=B, tn=512, name=f"proj_o_s{l}",
            epilogue=lambda acc, xr, gr: xr + gr * acc, extras=[('full', x), ('full', g1)])

    h2 = _norm_mod(x, P['norm2_g'][l], sc2, sh2, tm=B, rows_per_batch=1, name=f"norm2_s{l}")
    ffn_prev = C['state_ffn_conv'][l]
    act, gate_raw = _ffn_up_step(h2, Wl['gu'], ffn_prev, P['ffn_conv_w'][l], P['ffn_conv_b'][l],
                                 tn=D_FF // 2, name=f"ffn_up_s{l}")
    ffn_state = jnp.concatenate([ffn_prev[:, 1:], gate_raw[:, None]], axis=1)
    x = _mm(act, Wl['down'], tm=B, tn=512, name=f"ffn_down_s{l}",
            epilogue=lambda acc, xr, gr: xr + gr * acc, extras=[('full', x), ('full', g2)])
    return x, (new_rows, conv_state, ffn_state)


def kernel(x_prompt, x_sample, cache_kv, cache_win_kv, state_ssm_conv, state_ssm, state_ffn_conv, page_table, c_prompt, c_sample, w_ada, b_ada, norm1_g, norm2_g, final_g, w_in, rel_bias, cmp_pe, cmp_w1, cmp_w2, ssm_conv_w, ssm_conv_b, dt_bias, a_log, d_skip, ssm_norm_g, w_a, w_b, w_o, w_gu, ffn_conv_w, ffn_conv_b, w_down):
    P = dict(norm1_g=norm1_g, norm2_g=norm2_g, rel_bias=rel_bias, cmp_pe=cmp_pe.reshape(DEPTH, 2, -1),
             ssm_conv_w=ssm_conv_w, ssm_conv_b=ssm_conv_b, dt_bias=dt_bias, a_log=a_log, d_skip=d_skip,
             ssm_norm_g=ssm_norm_g, ffn_conv_w=ffn_conv_w, ffn_conv_b=ffn_conv_b)
    Bp, T, _ = x_prompt.shape
    Bs = x_sample.shape[0]
    past_len = page_table.shape[1] * cache_kv.shape[2]
    tables = _bias_tables(rel_bias * LOG2E)
    n_c = Bp + Bs
    c_rows = -(-n_c // SUBLANES) * SUBLANES
    c_all = jnp.zeros((c_rows, D_MODEL), F32).at[:Bp].set(c_prompt).at[Bp:n_c].set(c_sample).astype(BF16)
    xp = x_prompt.reshape(Bp * T, D_MODEL)
    xs = x_sample.reshape(Bs, D_MODEL)
    C = dict(cache_kv=cache_kv, page_table=page_table, cache_win_kv=cache_win_kv, state_ssm_conv=state_ssm_conv,
             state_ssm=state_ssm, state_ffn_conv=state_ffn_conv,
             tables=_decode_tables(rel_bias, past_len, page_table.shape[1], cache_win_kv.shape[2]))
    Cp = {}
    st_p, st_s = [], []
    for l in range(DEPTH):
        Wl = _layer_weights(l, w_ada, w_in, cmp_pe, cmp_w1, cmp_w2, w_a, w_b, w_o, w_gu, w_down)
        mod = _mm(c_all, Wl['ada'], tm=c_rows, tn=512, name=f"ada{l}",
                  epilogue=lambda acc, b: acc + b, extras=[('col', b_ada[l].reshape(1, -1))])
        xp, sp = _prompt_layer(xp, mod[:Bp], l, Wl, P, tables, Bp, T, Cp)
        st_p.append(sp)
        xs, ss = _sample_layer(xs, mod[Bp:n_c], l, Wl, P, C)
        st_s.append(ss)
    y_prompt = _final_norm(xp, final_g, tm=512, name="final_norm_p").reshape(Bp, T, D_MODEL)
    y_sample = _final_norm(xs, final_g, tm=Bs, name="final_norm_s").reshape(Bs, 1, D_MODEL)
    win_p, conv_p, ssm_p, ffn_p = [jnp.stack([s[i] for s in st_p]) for i in range(4)]
    kv_p = Cp['kv_out'].reshape(DEPTH, Bp, 4, N_KV, HEAD_DIM, T).transpose(0, 1, 5, 2, 3, 4)
    kv_s, conv_s, ffn_s = [jnp.stack([s[i] for s in st_s]) for i in range(3)]
    win_s = C['win_out'].transpose(0, 1, 5, 2, 3, 4)
    ssm_s = C['ssm_out']
    return (y_prompt, y_sample, kv_p, win_p, conv_p, ssm_p, ffn_p, kv_s, win_s, conv_s, ssm_s, ffn_s)
```

```python
import functools
import math

import jax
import jax.numpy as jnp
from jax import lax
from jax.experimental import pallas as pl
from jax.experimental.pallas import tpu as pltpu

D_MODEL = 1024
N_HEADS = 16
N_KV = 4
HEAD_DIM = 64
Q_PER_KV = N_HEADS // N_KV
L_CMP = 32
L_SEL = 64
TOP_N = 16
WINDOW = 512
CMP_HID = 128
Q_BLOCK = 128
SCALE = HEAD_DIM ** -0.5
N_BUCKETS = 32
MAX_DISTANCE = 128
D_INNER = 2 * D_MODEL
SSM_HEADDIM = 64
SSM_HEADS = D_INNER // SSM_HEADDIM
SSM_GROUPS = 4
D_STATE = 128
SSM_CONV = 4
SSM_CHUNK = 128
CONV_DIM = D_INNER + 2 * SSM_GROUPS * D_STATE
D_FF = 2816
FFN_CONV = 3
EPS = 1e-6
NEG_INF = -1e30
DEPTH = 2

Q_COLS = N_HEADS * HEAD_DIM
KV_COLS = 6 * N_KV * HEAD_DIM
NSA_GATE_COLS = 3 * N_HEADS
OFF_KV = Q_COLS
OFF_G = OFF_KV + KV_COLS
OFF_Z = OFF_G + NSA_GATE_COLS
OFF_XBC = OFF_Z + D_INNER
OFF_DT = OFF_XBC + CONV_DIM
OFF_MG = OFF_DT + SSM_HEADS
P_IN = OFF_MG + 2 * D_MODEL

LOG2E = math.log2(math.e)

LANES = 128
SUBLANES = 8
BF16_ROWS = 16
TQ = 128
WIDTH = Q_PER_KV * TQ
FAR_TILES = 4
WIN_TILES = WINDOW // TQ + 1
AUG_ROW0 = HEAD_DIM
VMEM_LIMIT = 48 * 1024 * 1024

BF16 = jnp.bfloat16
F32 = jnp.float32


def _cparams(sem):
    return pltpu.CompilerParams(dimension_semantics=sem, vmem_limit_bytes=VMEM_LIMIT)


def _silu(x):
    return x * jax.nn.sigmoid(x)


def _dot(a, b):
    return jnp.dot(a, b, preferred_element_type=F32)


def _dot_nt(a, b):
    return lax.dot_general(a, b, (((1,), (1,)), ((), ())), preferred_element_type=F32)


def _dot_tn(a, b):
    return lax.dot_general(a, b, (((0,), (0,)), ((), ())), preferred_element_type=F32)


def _split3(x):
    hi = x.astype(BF16)
    r1 = x - hi.astype(F32)
    mid = r1.astype(BF16)
    lo = (r1 - mid.astype(F32)).astype(BF16)
    return hi, mid, lo


def _mm_body(a_ref, w_ref, *refs, epilogue, n_extra):
    acc = _dot(a_ref[...], w_ref[...])
    extras = [r[...] for r in refs[:n_extra]]
    o_ref = refs[n_extra]
    o_ref[...] = epilogue(acc, *extras).astype(o_ref.dtype)


def _mm(a, w, *, tm, tn, name, out_dtype=F32, epilogue=None, extras=(), rows_per_batch=None):
    M, K = a.shape
    N = w.shape[1]
    assert M % tm == 0 and N % tn == 0, (M, N, tm, tn)
    in_specs = [pl.BlockSpec((tm, K), lambda i, j: (i, 0)),
                pl.BlockSpec((K, tn), lambda i, j: (0, j))]
    ops = [a, w]
    for kind, arr in extras:
        if kind == 'col':
            in_specs.append(pl.BlockSpec((1, tn), lambda i, j: (0, j)))
        elif kind == 'full':
            in_specs.append(pl.BlockSpec((tm, tn), lambda i, j: (i, j)))
        else:
            assert kind == 'batch' and rows_per_batch % tm == 0
            tpb = rows_per_batch // tm
            in_specs.append(pl.BlockSpec((None, 1, tn), lambda i, j, tpb=tpb: (i // tpb, 0, j)))
        ops.append(arr)
    if epilogue is None:
        epilogue = lambda acc: acc
    return pl.pallas_call(
        functools.partial(_mm_body, epilogue=epilogue, n_extra=len(extras)),
        grid=(M // tm, N // tn),
        in_specs=in_specs,
        out_specs=pl.BlockSpec((tm, tn), lambda i, j: (i, j)),
        out_shape=jax.ShapeDtypeStruct((M, N), out_dtype),
        compiler_params=_cparams(("parallel", "parallel")),
        name=name,
    )(*ops)


def _mm_t_body(wt_ref, a_ref, *refs, scale):
    o_ref = refs[-1]
    o_ref[...] = (_dot_nt(wt_ref[...], a_ref[...]) * scale).astype(o_ref.dtype)


def _mm_t(a, wt, *, tm, tn, rows_per_batch, name, out_dtype=F32, scale=1.0, layer=None, out_buf=None):
    M, K = a.shape
    N = wt.shape[0]
    T = rows_per_batch
    assert T % tm == 0 and N % tn == 0, (T, N, tm, tn)
    tpb = T // tm
    if layer is None:
        shape = (M // T, N, T)
        out_spec = pl.BlockSpec((None, tn, tm), lambda i, j: (i // tpb, j, i % tpb))
    else:
        shape = (DEPTH, M // T, N, T)
        out_spec = pl.BlockSpec((None, None, tn, tm), lambda i, j: (layer, i // tpb, j, i % tpb))
    ops = [wt, a]
    in_specs = [pl.BlockSpec((tn, K), lambda i, j: (j, 0)), pl.BlockSpec((tm, K), lambda i, j: (i, 0))]
    aliases = {}
    if out_buf is not None:
        aliases = {len(ops): 0}
        ops.append(out_buf)
        in_specs.append(pl.BlockSpec(memory_space=pl.ANY))
    return pl.pallas_call(
        functools.partial(_mm_t_body, scale=scale),
        grid=(M // tm, N // tn),
        in_specs=in_specs,
        out_specs=out_spec,
        out_shape=jax.ShapeDtypeStruct(shape, out_dtype),
        input_output_aliases=aliases,
        compiler_params=_cparams(("parallel", "parallel")),
        name=name,
    )(*ops)


def _norm_mod_body(x_ref, g_ref, sc_ref, sh_ref, o_ref):
    x = x_ref[...]
    y = x * lax.rsqrt(jnp.mean(x * x, axis=-1, keepdims=True) + EPS) * g_ref[...]
    o_ref[...] = (y * (1.0 + sc_ref[...]) + sh_ref[...]).astype(o_ref.dtype)


def _norm_mod(x, g, sc, sh, *, tm, rows_per_batch, name):
    M, D = x.shape
    if sc.ndim == 3:
        tpb = rows_per_batch // tm
        mod_spec = pl.BlockSpec((None, 1, D), lambda i: (i // tpb, 0, 0))
    else:
        mod_spec = pl.BlockSpec((tm, D), lambda i: (i, 0))
    return pl.pallas_call(
        _norm_mod_body,
        grid=(M // tm,),
        in_specs=[pl.BlockSpec((tm, D), lambda i: (i, 0)), pl.BlockSpec((1, D), lambda i: (0, 0)),
                  mod_spec, mod_spec],
        out_specs=pl.BlockSpec((tm, D), lambda i: (i, 0)),
        out_shape=jax.ShapeDtypeStruct((M, D), BF16),
        compiler_params=_cparams(("parallel",)),
        name=name,
    )(x, g.reshape(1, D), sc, sh)


def _final_norm_body(x_ref, g_ref, o_ref):
    x = x_ref[...]
    o_ref[...] = x * lax.rsqrt(jnp.mean(x * x, axis=-1, keepdims=True) + EPS) * g_ref[...]


def _final_norm(x, g, *, tm, name):
    M, D = x.shape
    return pl.pallas_call(
        _final_norm_body,
        grid=(M // tm,),
        in_specs=[pl.BlockSpec((tm, D), lambda i: (i, 0)), pl.BlockSpec((1, D), lambda i: (0, 0))],
        out_specs=pl.BlockSpec((tm, D), lambda i: (i, 0)),
        out_shape=jax.ShapeDtypeStruct((M, D), F32),
        compiler_params=_cparams(("parallel",)),
        name=name,
    )(x, g.reshape(1, D))


def _merge_body(o_ref, y_ref, wa_ref, wb_ref, ga_ref, gb_ref, u_ref):
    pa = _dot(o_ref[...], wa_ref[...])
    pb = _dot(y_ref[...], wb_ref[...])
    u_ref[...] = (ga_ref[...] * pa + gb_ref[...] * pb).astype(u_ref.dtype)


def _merge(o_nsa, y, w_a, w_b, gates, *, tm, tn, name):
    M = o_nsa.shape[0]
    N = w_a.shape[1]
    nj = N // tn
    return pl.pallas_call(
        _merge_body,
        grid=(M // tm, nj),
        in_specs=[pl.BlockSpec((tm, o_nsa.shape[1]), lambda i, j: (i, 0)),
                  pl.BlockSpec((tm, y.shape[1]), lambda i, j: (i, 0)),
                  pl.BlockSpec((w_a.shape[0], tn), lambda i, j: (0, j)),
                  pl.BlockSpec((w_b.shape[0], tn), lambda i, j: (0, j)),
                  pl.BlockSpec((tm, tn), lambda i, j: (i, j)),
                  pl.BlockSpec((tm, tn), lambda i, j, nj=nj: (i, j + nj))],
        out_specs=pl.BlockSpec((tm, tn), lambda i, j: (i, j)),
        out_shape=jax.ShapeDtypeStruct((M, N), BF16),
        compiler_params=_cparams(("parallel", "parallel")),
        name=name,
    )(o_nsa, y, w_a, w_b, gates, gates)


def _ffn_up_body(h_ref, wg_ref, wu_ref, cw_ref, cb_ref, o_ref, pad_ref, *, tm, tiles_per_batch):
    i = pl.program_id(1)
    h = h_ref[...]
    gate = _dot(h, wg_ref[...])
    up = _dot(h, wu_ref[...])

    @pl.when(i % tiles_per_batch == 0)
    def _():
        pad_ref[0:SUBLANES, :] = jnp.zeros((SUBLANES, pad_ref.shape[1]), F32)

    pad_ref[SUBLANES:SUBLANES + tm, :] = gate
    y = cb_ref[...]
    for k in range(FFN_CONV):
        off = SUBLANES - (FFN_CONV - 1) + k
        y = y + cw_ref[k:k + 1, :] * pad_ref[off:off + tm, :]
    pad_ref[0:SUBLANES, :] = pad_ref[tm:tm + SUBLANES, :]
    o_ref[...] = (_silu(y) * up).astype(o_ref.dtype)


def _ffn_up(h, w_gu, conv_w, conv_b, *, tm, tn, rows_per_batch, name):
    M, K = h.shape
    nj = D_FF // tn
    return pl.pallas_call(
        functools.partial(_ffn_up_body, tm=tm, tiles_per_batch=rows_per_batch // tm),
        grid=(nj, M // tm),
        in_specs=[pl.BlockSpec((tm, K), lambda j, i: (i, 0)),
                  pl.BlockSpec((K, tn), lambda j, i: (0, j)),
                  pl.BlockSpec((K, tn), lambda j, i, nj=nj: (0, j + nj)),
                  pl.BlockSpec((FFN_CONV, tn), lambda j, i: (0, j)),
                  pl.BlockSpec((1, tn), lambda j, i: (0, j))],
        out_specs=pl.BlockSpec((tm, tn), lambda j, i: (i, j)),
        out_shape=jax.ShapeDtypeStruct((M, D_FF), BF16),
        scratch_shapes=[pltpu.VMEM((tm + SUBLANES, tn), F32)],
        compiler_params=_cparams(("arbitrary", "arbitrary")),
        name=name,
    )(h, w_gu, w_gu, conv_w, conv_b.reshape(1, D_FF))


def _proj_conv_body(h_ref, w_ref, cw_ref, cb_ref, o_ref, pad_ref, *, tm, taps, tiles_per_batch):
    i = pl.program_id(1)

    @pl.when(i % tiles_per_batch == 0)
    def _():
        pad_ref[0:SUBLANES, :] = jnp.zeros((SUBLANES, pad_ref.shape[1]), F32)

    pad_ref[SUBLANES:SUBLANES + tm, :] = _dot(h_ref[...], w_ref[...])
    y = cb_ref[...]
    for k in range(taps):
        off = SUBLANES - (taps - 1) + k
        y = y + cw_ref[k:k + 1, :] * pad_ref[off:off + tm, :]
    pad_ref[0:SUBLANES, :] = pad_ref[tm:tm + SUBLANES, :]
    o_ref[...] = _silu(y)


def _proj_conv(h, w, conv_w, conv_b, *, tm, tn, rows_per_batch, name):
    M, K = h.shape
    N = w.shape[1]
    taps = conv_w.shape[0]
    return pl.pallas_call(
        functools.partial(_proj_conv_body, tm=tm, taps=taps, tiles_per_batch=rows_per_batch // tm),
        grid=(N // tn, M // tm),
        in_specs=[pl.BlockSpec((tm, K), lambda j, i: (i, 0)),
                  pl.BlockSpec((K, tn), lambda j, i: (0, j)),
                  pl.BlockSpec((taps, tn), lambda j, i: (0, j)),
                  pl.BlockSpec((1, tn), lambda j, i: (0, j))],
        out_specs=pl.BlockSpec((tm, tn), lambda j, i: (i, j)),
        out_shape=jax.ShapeDtypeStruct((M, N), F32),
        scratch_shapes=[pltpu.VMEM((tm + SUBLANES, tn), F32)],
        compiler_params=_cparams(("arbitrary", "arbitrary")),
        name=name,
    )(h, w, conv_w, conv_b.reshape(1, N))


def _compress_body(x_ref, pe_ref, w1_ref, w2_ref, o_ref):
    xb = (x_ref[...] + pe_ref[...]).astype(BF16)
    hid = _silu(_dot(xb, w1_ref[...]))
    o_ref[...] = _dot(hid.astype(BF16), w2_ref[...])


def _compress(x, pe, w1, w2, *, tm, name):
    rows, kdim = x.shape
    return pl.pallas_call(
        _compress_body,
        grid=(rows // tm,),
        in_specs=[pl.BlockSpec((tm, kdim), lambda i: (i, 0)), pl.BlockSpec((1, kdim), lambda i: (0, 0)),
                  pl.BlockSpec((kdim, CMP_HID), lambda i: (0, 0)),
                  pl.BlockSpec((CMP_HID, HEAD_DIM), lambda i: (0, 0))],
        out_specs=pl.BlockSpec((tm, HEAD_DIM), lambda i: (i, 0)),
        out_shape=jax.ShapeDtypeStruct((rows, HEAD_DIM), F32),
        compiler_params=_cparams(("parallel",)),
        name=name,
    )(x, pe.reshape(1, kdim), w1, w2)


def _attend(s, vt):
    m = jnp.max(s, axis=0, keepdims=True)
    pv = _dot(vt, jnp.exp2(s - m).astype(BF16))
    return m, pv[HEAD_DIM:HEAD_DIM + 1, :], pv[0:HEAD_DIM, :]


def _lane_concat(tiles3):
    return jnp.concatenate([tiles3[t] for t in range(tiles3.shape[0])], axis=1)


SEL_TILES = 4

def _nsa_select_body(q_ref, kc_ref, vct_ref, band_ref, ocmp_ref, mask_ref, *, n_blocks, n_cmp, top_n):
    step = pl.program_id(2)
    half = n_cmp // 2
    row = lax.broadcasted_iota(jnp.int32, (n_cmp, 16), 0)
    cidx = jnp.where(row < half, 2 * row, 2 * (row - half) + 1)
    col16 = lax.broadcasted_iota(jnp.int32, (n_cmp, 16), 1)
    b_hi, b_mid, b_lo = _split3(band_ref[...])
    imps = []
    for t in range(SEL_TILES):
        qi = SEL_TILES * step + t
        crel = cidx - (TQ // L_CMP) * qi
        cat = jnp.where(crel <= -5, 8, jnp.where(crel >= 4, 9, crel + 4))
        place = (cat == col16).astype(BF16)
        bias_c = _dot(place, b_hi) + _dot(place, b_mid) + _dot(place, b_lo)
        s = _dot(kc_ref[...], q_ref[t]) + bias_c
        m = jnp.max(s, axis=0, keepdims=True)
        e = jnp.exp2(s - m)
        denom = jnp.sum(e, axis=0, keepdims=True)
        anyvalid = (m > 0.5 * NEG_INF).astype(F32)
        p = e * (anyvalid / denom)
        ocmp_ref[t] = _dot(vct_ref[...], p.astype(BF16))
        psum = p[:, 0:TQ]
        for r in range(1, Q_PER_KV):
            psum = psum + p[:, r * TQ:(r + 1) * TQ]
        imps.append(psum[0:half, :] + psum[half:n_cmp, :])
    imp = jnp.concatenate(imps, axis=1)

    wide = SEL_TILES * TQ
    blk = lax.broadcasted_iota(jnp.int32, (n_blocks, wide), 0)
    blk_f = blk.astype(F32)
    qpos = step * wide + lax.broadcasted_iota(jnp.int32, (n_blocks, wide), 1)
    forced = (blk == 0) | (blk == lax.shift_right_logical(qpos, 6))
    valid = blk * L_SEL <= qpos
    score = jnp.where(forced, 1e30, jnp.where(valid, imp, -1.0))
    picked = jnp.zeros((n_blocks, wide), jnp.bool_)
    for _ in range(top_n):
        mx = jnp.max(score, axis=0, keepdims=True)
        first = jnp.min(jnp.where(score == mx, blk_f, float(n_blocks)), axis=0, keepdims=True)
        hit = blk_f == first
        picked = picked | hit
        score = jnp.where(hit, -2.0, score)
    seladd = jnp.where(picked, 0.0, NEG_INF)
    for t in range(SEL_TILES):
        mask_ref[t] = seladd[:, t * TQ:(t + 1) * TQ]


def _nsa_select(q2, kc, vct, band, *, name):
    B, G, nq, D, _ = q2.shape
    n_cmp = kc.shape[2]
    n_blocks = n_cmp // (L_SEL // L_CMP)
    assert nq % SEL_TILES == 0
    tiles = lambda shape: pl.BlockSpec((None, None, SEL_TILES) + shape, lambda b, g, i: (b, g, i, 0, 0))
    return pl.pallas_call(
        functools.partial(_nsa_select_body, n_blocks=n_blocks, n_cmp=n_cmp, top_n=min(TOP_N, n_blocks)),
        grid=(B, G, nq // SEL_TILES),
        in_specs=[tiles((D, WIDTH)),
                  pl.BlockSpec((None, None, n_cmp, D), lambda b, g, i: (b, g, 0, 0)),
                  pl.BlockSpec((None, None, D, n_cmp), lambda b, g, i: (b, g, 0, 0)),
                  pl.BlockSpec((None, 16, WIDTH), lambda b, g, i: (g, 0, 0))],
        out_specs=[tiles((D, WIDTH)), tiles((n_blocks, TQ))],
        out_shape=[jax.ShapeDtypeStruct((B, G, nq, D, WIDTH), F32),
                   jax.ShapeDtypeStruct((B, G, nq, n_blocks, TQ), F32)],
        compiler_params=_cparams(("parallel", "parallel", "parallel")),
        name=name,
    )(q2, kc, vct, band)


def _nsa_prompt_body(q_ref, ksel_ref, vsel_ref, kwin_ref, vwin_ref, ocmp_ref, mask_ref, gate_ref,
                     tzw_ref, tzn_ref, caug_ref, o_ref,
                     qaug_ref, m_ref, l_ref, acc_ref, sel0_ref, selfar_ref, sa_scr, sb_scr, *, n_blocks):
    qi = pl.program_id(2)
    n_grp_rows = n_blocks // SUBLANES

    @pl.when(qi == 0)
    def _():
        selfar_ref[:, SUBLANES:2 * SUBLANES, :] = jnp.broadcast_to(
            caug_ref[...][None], (selfar_ref.shape[0], SUBLANES, WIDTH))
        selfar_ref[n_grp_rows:, 0:SUBLANES, :] = jnp.full(
            (selfar_ref.shape[0] - n_grp_rows, SUBLANES, WIDTH), NEG_INF, F32)
        sel0_ref[0:SUBLANES, :] = jnp.zeros((SUBLANES, WIDTH), F32)

    qaug_ref[0:HEAD_DIM, :] = q_ref[...]
    qaug_ref[HEAD_DIM:, :] = jnp.zeros((qaug_ref.shape[0] - HEAD_DIM, WIDTH), BF16)

    kw = kwin_ref[pl.ds(qi, WIN_TILES)].reshape(WIN_TILES * TQ, LANES)
    m_w, l_w, a_w = _attend(_dot(kw, qaug_ref[...]) + tzw_ref[...], _lane_concat(vwin_ref[pl.ds(qi, WIN_TILES)]))
    o_win = a_w * (1.0 / l_w)

    seladd = mask_ref[...]
    blk = lax.broadcasted_iota(jnp.int32, (n_blocks, TQ), 0)
    selfar = jnp.where(blk < 2 * (qi - 1), seladd, NEG_INF)
    sel0_ref[SUBLANES:, :] = jnp.concatenate([seladd] * Q_PER_KV, axis=1)
    selfar_ref[0:n_grp_rows, 0:SUBLANES, :] = jnp.concatenate([selfar] * Q_PER_KV, axis=1).reshape(
        n_grp_rows, SUBLANES, WIDTH)

    near_mask = jnp.concatenate(
        [jnp.broadcast_to(sel0_ref[pl.ds(SUBLANES + 2 * (qi - 1) + i, 1), :], (L_SEL, WIDTH)) for i in range(4)],
        axis=0)
    kn = ksel_ref[pl.ds(qi, 2)].reshape(2 * TQ, LANES)
    m_n, l_n, a_n = _attend(_dot(kn, qaug_ref[...]) + tzn_ref[...] + near_mask, _lane_concat(vsel_ref[pl.ds(qi, 2)]))
    m_ref[...] = m_n
    l_ref[...] = l_n
    acc_ref[...] = a_n

    def far_scores(gi):
        qaug_ref[AUG_ROW0:AUG_ROW0 + BF16_ROWS, :] = selfar_ref[gi].astype(BF16)
        kg = ksel_ref[pl.ds(1 + FAR_TILES * gi, FAR_TILES)].reshape(FAR_TILES * TQ, LANES)
        return _dot(kg, qaug_ref[...])

    def far_update(s, gi):
        m_g, l_g, a_g = _attend(s, _lane_concat(vsel_ref[pl.ds(1 + FAR_TILES * gi, FAR_TILES)]))
        m_old = m_ref[...]
        m_new = jnp.maximum(m_old, m_g)
        a_old = jnp.exp2(m_old - m_new)
        a_grp = jnp.exp2(m_g - m_new)
        l_ref[...] = l_ref[...] * a_old + l_g * a_grp
        acc_ref[...] = acc_ref[...] * a_old + a_g * a_grp
        m_ref[...] = m_new

    sa_scr[...] = far_scores(0)

    def far_body(j, carry):
        sb_scr[...] = far_scores(2 * j + 1)
        far_update(sa_scr[...], 2 * j)
        sa_scr[...] = far_scores(2 * j + 2)
        far_update(sb_scr[...], 2 * j + 1)
        return carry

    n_far = jnp.maximum(qi - 1, 0)
    n_groups = lax.shift_right_logical(n_far + (FAR_TILES - 1), 2)
    lax.fori_loop(0, lax.shift_right_logical(n_groups + 1, 1), far_body, 0)
    o_sel = acc_ref[...] * (1.0 / l_ref[...])

    g = jax.nn.sigmoid(gate_ref[...])
    o = g[0:1, :] * ocmp_ref[...] + g[1:2, :] * o_sel + g[2:3, :] * o_win
    for r in range(Q_PER_KV):
        o_ref[r * HEAD_DIM:(r + 1) * HEAD_DIM, :] = o[:, r * TQ:(r + 1) * TQ].astype(o_ref.dtype)


def _nsa_prompt(q2, ksel, vsel, kwin, vwin, kc, vct, gates, tables, *, name):
    tzw, tzn, band, caug = tables
    B, G, nq, D, _ = q2.shape
    n_blocks = kc.shape[2] // (L_SEL // L_CMP)
    assert n_blocks % SUBLANES == 0 and nq % FAR_TILES == 0
    T = nq * TQ
    o_cmp, masks = _nsa_select(q2, kc, vct, band, name=name + "_select")
    whole = lambda arr: pl.BlockSpec((None, None) + arr.shape[2:], lambda b, g, i: (b, g, 0, 0, 0))
    tile = lambda shape: pl.BlockSpec((None, None, None) + shape, lambda b, g, i: (b, g, i, 0, 0))
    return pl.pallas_call(
        functools.partial(_nsa_prompt_body, n_blocks=n_blocks),
        grid=(B, G, nq),
        in_specs=[tile((D, WIDTH)),
                  whole(ksel), whole(vsel), whole(kwin), whole(vwin),
                  tile((D, WIDTH)), tile((n_blocks, TQ)), tile((3, WIDTH)),
                  pl.BlockSpec((None, None, WIN_TILES * TQ, WIDTH),
                               lambda b, g, i: (g, jnp.minimum(i, WIN_TILES - 1), 0, 0)),
                  pl.BlockSpec((None, None, 2 * TQ, WIDTH), lambda b, g, i: (g, jnp.minimum(i, 1), 0, 0)),
                  pl.BlockSpec((None, SUBLANES, WIDTH), lambda b, g, i: (g, 0, 0))],
        out_specs=pl.BlockSpec((None, Q_PER_KV * D, TQ), lambda b, g, i: (b, g, i)),
        out_shape=jax.ShapeDtypeStruct((B, G * Q_PER_KV * D, T), BF16),
        scratch_shapes=[pltpu.VMEM((LANES, WIDTH), BF16),
                        pltpu.VMEM((1, WIDTH), F32), pltpu.VMEM((1, WIDTH), F32),
                        pltpu.VMEM((D, WIDTH), F32),
                        pltpu.VMEM((n_blocks + SUBLANES, WIDTH), F32),
                        pltpu.VMEM((nq // FAR_TILES + 2, 2 * SUBLANES, WIDTH), F32),
                        pltpu.VMEM((FAR_TILES * TQ, WIDTH), F32), pltpu.VMEM((FAR_TILES * TQ, WIDTH), F32)],
        compiler_params=_cparams(("arbitrary", "arbitrary", "arbitrary")),
        name=name,
    )(q2, ksel, vsel, kwin, vwin, o_cmp, masks, gates, tzw, tzn, caug)


SSD_QUAD = 4

def _ssd_body(xbc_ref, dt_ref, z_ref, dtb_ref, alog_ref, dskip_ref, ng_ref,
              exp_h_ref, exp_p_ref, y_ref, st_ref, x_scr, colh_scr, colp_scr, dtp_scr, y_scr):
    c = pl.program_id(1)
    L = SSM_CHUNK
    P = SSM_HEADDIM
    heads_per_group = SSM_HEADS // SSM_GROUPS

    @pl.when(c == 0)
    def _():
        st_ref[...] = jnp.zeros(st_ref.shape, F32)

    xbc = xbc_ref[...]
    x_scr[...] = xbc[:, 0:D_INNER]
    bmat = [xbc[:, D_INNER + g * D_STATE:D_INNER + (g + 1) * D_STATE].astype(BF16) for g in range(SSM_GROUPS)]
    cmat = [xbc[:, D_INNER + (SSM_GROUPS + g) * D_STATE:D_INNER + (SSM_GROUPS + g + 1) * D_STATE].astype(BF16)
            for g in range(SSM_GROUPS)]

    x = dt_ref[:, 0:SSM_HEADS] + dtb_ref[...]
    dt = jnp.maximum(x, 0.0) + jnp.log1p(jnp.exp(-jnp.abs(x)))
    a = dt * (-jnp.exp(alog_ref[...]))
    tri = lax.broadcasted_iota(jnp.int32, (L, L), 0) >= lax.broadcasted_iota(jnp.int32, (L, L), 1)
    tri_b = tri.astype(BF16)
    a_hi, a_mid, a_lo = _split3(a)
    a_cs = _dot(tri_b, a_hi) + _dot(tri_b, a_mid) + _dot(tri_b, a_lo)
    acs_parts = jnp.concatenate(_split3(a_cs), axis=1)
    dt_parts = jnp.concatenate(_split3(dt), axis=1)
    colh_scr[...] = _dot(acs_parts, exp_h_ref[...])
    colp_scr[...] = _dot(acs_parts, exp_p_ref[...])
    dtp_scr[...] = _dot(dt_parts, exp_p_ref[...])
    acs_t = a_cs.T
    dt_t = dt.T
    tri_q = jnp.concatenate([tri] * SSD_QUAD, axis=1)

    for qd in range(SSM_HEADS // SSD_QUAD):
        g = (qd * SSD_QUAD) // heads_per_group
        h0 = qd * SSD_QUAD
        b_g, c_g = bmat[g], cmat[g]
        cb = _dot_nt(c_g, b_g)
        col = colh_scr[:, h0 * LANES:(h0 + SSD_QUAD) * LANES]
        row = jnp.concatenate([acs_t[h0 + i:h0 + i + 1, :] for i in range(SSD_QUAD)], axis=1)
        dtrow = jnp.concatenate([dt_t[h0 + i:h0 + i + 1, :] for i in range(SSD_QUAD)], axis=1)
        decay = jnp.exp(jnp.where(tri_q, col - row, NEG_INF))
        w = (jnp.concatenate([cb] * SSD_QUAD, axis=1) * decay * dtrow).astype(BF16)
        xq = x_scr[:, h0 * P:(h0 + SSD_QUAD) * P]
        xq_b = xq.astype(BF16)
        y_diag = jnp.concatenate([_dot(w[:, i * LANES:(i + 1) * LANES], xq_b[:, i * P:(i + 1) * P])
                                  for i in range(SSD_QUAD)], axis=1)
        h_prev = st_ref[h0:h0 + SSD_QUAD].reshape(SSD_QUAD * P, D_STATE)
        colp = colp_scr[:, h0 * P:(h0 + SSD_QUAD) * P]
        y_off = _dot_nt(c_g, h_prev.astype(BF16)) * jnp.exp(colp)
        y_scr[:, h0 * P:(h0 + SSD_QUAD) * P] = y_diag + y_off
        decay_end = jnp.exp(colp[L - 1:L, :] - colp)
        xw = (decay_end * dtp_scr[:, h0 * P:(h0 + SSD_QUAD) * P]) * xq
        st = _dot_tn(xw.astype(BF16), b_g)
        chunk_decay = jnp.exp(col[L - 1:L, :])
        cd = jnp.concatenate([jnp.broadcast_to(chunk_decay[:, i * LANES:(i + 1) * LANES], (P, D_STATE))
                              for i in range(SSD_QUAD)], axis=0)
        st_ref[h0:h0 + SSD_QUAD] = (h_prev * cd + st).reshape(SSD_QUAD, P, D_STATE)

    group_w = D_INNER // SSM_GROUPS
    y = y_scr[...] + dskip_ref[...] * x_scr[...]
    y = y * _silu(z_ref[...])
    for g in range(SSM_GROUPS):
        yg = y[:, g * group_w:(g + 1) * group_w]
        inv = lax.rsqrt(jnp.mean(yg * yg, axis=-1, keepdims=True) + EPS)
        y_ref[:, g * group_w:(g + 1) * group_w] = (
            yg * inv * ng_ref[:, g * group_w:(g + 1) * group_w]).astype(y_ref.dtype)


def _ssd_prompt(xbc, dtg, z, dt_bias, a_log, d_skip, norm_g, *, batch, name):
    M = xbc.shape[0]
    L = SSM_CHUNK
    nc = M // batch // L
    heads = jnp.arange(SSM_HEADS)

    def expand(lanes):
        e = (heads[:, None, None] == heads[None, :, None])
        e = jnp.broadcast_to(e, (SSM_HEADS, SSM_HEADS, lanes)).reshape(SSM_HEADS, SSM_HEADS * lanes)
        return jnp.concatenate([e] * 3, axis=0).astype(BF16)

    row = lambda v: v.reshape(1, -1)
    const2 = lambda shape: pl.BlockSpec(shape, lambda b, c: (0, 0))
    return pl.pallas_call(
        _ssd_body,
        grid=(batch, nc),
        in_specs=[pl.BlockSpec((L, CONV_DIM), lambda b, c: (b * nc + c, 0)),
                  pl.BlockSpec((L, LANES), lambda b, c: (b * nc + c, 1)),
                  pl.BlockSpec((L, D_INNER), lambda b, c: (b * nc + c, 0)),
                  const2((1, SSM_HEADS)), const2((1, SSM_HEADS)),
                  const2((1, D_INNER)), const2((1, D_INNER)),
                  const2((3 * SSM_HEADS, SSM_HEADS * LANES)), const2((3 * SSM_HEADS, D_INNER))],
        out_specs=[pl.BlockSpec((L, D_INNER), lambda b, c: (b * nc + c, 0)),
                   pl.BlockSpec((None, SSM_HEADS, SSM_HEADDIM, D_STATE), lambda b, c: (b, 0, 0, 0))],
        out_shape=[jax.ShapeDtypeStruct((M, D_INNER), BF16),
                   jax.ShapeDtypeStruct((batch, SSM_HEADS, SSM_HEADDIM, D_STATE), F32)],
        scratch_shapes=[pltpu.VMEM((L, D_INNER), F32),
                        pltpu.VMEM((L, SSM_HEADS * LANES), F32),
                        pltpu.VMEM((L, D_INNER), F32),
                        pltpu.VMEM((L, D_INNER), F32),
                        pltpu.VMEM((L, D_INNER), F32)],
        compiler_params=_cparams(("arbitrary", "arbitrary")),
        name=name,
    )(xbc, dtg, z, row(dt_bias), row(a_log),
      row(jnp.repeat(d_skip, SSM_HEADDIM)), row(norm_g), expand(LANES), expand(SSM_HEADDIM))


def _t5_bucket(dist):
    n = jnp.maximum(dist, 0)
    max_exact = N_BUCKETS // 2
    nf = jnp.maximum(n, max_exact).astype(F32)
    large = max_exact + (jnp.log(nf / max_exact) / math.log(MAX_DISTANCE / max_exact)
                         * (N_BUCKETS - max_exact)).astype(jnp.int32)
    return jnp.where(n < max_exact, n, jnp.minimum(large, N_BUCKETS - 1))


def _bias_tables(rel_bias):
    by_dist = rel_bias[_t5_bucket(jnp.arange((WIN_TILES + 1) * TQ))]
    qry = jnp.arange(TQ)[None, :]
    delta = (jnp.arange(WIN_TILES - 1, -1, -1) * TQ)[:, None]
    dist = delta - (TQ - 1) + jnp.arange(2 * TQ - 1)[None, :]
    ok = (dist >= 0) & (dist <= WINDOW)
    vals = jnp.where(ok[..., None], by_dist[jnp.clip(dist, 0)], NEG_INF).transpose(0, 2, 1)
    hank = jnp.tile(vals, (1, 1, TQ + 1))[..., :TQ * 2 * TQ].reshape(WIN_TILES, N_HEADS, TQ, 2 * TQ)
    tz = hank[:, :, ::-1, :TQ]
    tz = tz.reshape(WIN_TILES, N_KV, Q_PER_KV, TQ, TQ).transpose(1, 0, 3, 2, 4)
    tz = tz.reshape(N_KV, WIN_TILES, TQ, WIDTH)
    tile = jnp.arange(WIN_TILES)[None, :, None, None]

    def variants(tiles, n_var):
        n_tiles = tiles.shape[1]
        v = jnp.arange(n_var)[:, None, None, None]
        masked = tile[:, :n_tiles] < (n_tiles - 1 - v)
        out = jnp.where(masked[None], NEG_INF, tiles[:, None])
        return out.reshape(N_KV, n_var, n_tiles * TQ, WIDTH)

    tzw = variants(tz, WIN_TILES)
    tzn = variants(tz[:, WIN_TILES - 2:], 2)
    far = rel_bias[N_BUCKETS - 1]
    rel = jnp.arange(8)[:, None] - 4
    cdist = qry - L_CMP * rel - (L_CMP - 1)
    band = jnp.where((cdist >= 0)[..., None], by_dist[jnp.clip(cdist, 0)], NEG_INF)
    far_rows = jnp.broadcast_to(far[None, None, :], (1, TQ, N_HEADS))
    band = jnp.concatenate([band, far_rows, jnp.full((1, TQ, N_HEADS), NEG_INF, F32),
                            jnp.zeros((6, TQ, N_HEADS), F32)], axis=0)
    band = band.reshape(16, TQ, N_KV, Q_PER_KV).transpose(2, 0, 3, 1).reshape(N_KV, 16, WIDTH)
    parts = jnp.stack([p.astype(F32) for p in _split3(far)] + [jnp.zeros_like(far)] * (SUBLANES - 3))
    caug = jnp.broadcast_to(parts.reshape(SUBLANES, N_KV, Q_PER_KV, 1), (SUBLANES, N_KV, Q_PER_KV, TQ))
    caug = caug.transpose(1, 0, 2, 3).reshape(N_KV, SUBLANES, WIDTH)
    return tzw, tzn, band, caug


def _layer_weights(l, w_ada, w_in, cmp_pe, cmp_w1, cmp_w2, w_a, w_b, w_o, w_gu, w_down):
    wi = w_in[l]
    small = jnp.zeros((D_MODEL, 2 * LANES), F32)
    small = small.at[:, 0:NSA_GATE_COLS].set(wi[:, OFF_G:OFF_Z])
    small = small.at[:, LANES:LANES + SSM_HEADS].set(wi[:, OFF_DT:OFF_MG])
    c = lambda w: w.astype(BF16)
    w1r = cmp_w1[l].reshape(2, L_CMP, HEAD_DIM, CMP_HID)
    w1_pair = jnp.zeros((2, L_CMP, 2 * HEAD_DIM, 2 * CMP_HID), F32)
    w1_pair = w1_pair.at[:, :, 0:HEAD_DIM, 0:CMP_HID].set(w1r).at[:, :, HEAD_DIM:, CMP_HID:].set(w1r)
    pe_pair = jnp.tile(cmp_pe[l], (1, 1, 2))
    return dict(ada=c(w_ada[l]), q=c(wi[:, 0:OFF_KV]), kv=c(wi[:, OFF_KV:OFF_G]), small=c(small),
                q_t=c(wi[:, 0:OFF_KV].T), kvrows_t=c(wi[:, OFF_KV:OFF_KV + 4 * N_KV * HEAD_DIM].T),
                kvwin_t=c(wi[:, OFF_KV + 4 * N_KV * HEAD_DIM:OFF_G].T),
                w1_pair=c(w1_pair), pe_pair=pe_pair,
                z=c(wi[:, OFF_Z:OFF_XBC]), xbc=c(wi[:, OFF_XBC:OFF_DT]), mg=c(wi[:, OFF_MG:P_IN]),
                cmp_w1=c(cmp_w1[l]), cmp_w2=c(cmp_w2[l]), a=c(w_a[l]), b=c(w_b[l]), o=c(w_o[l]),
                gu=c(w_gu[l]), down=c(w_down[l]))


def _prompt_layer(x, mod, l, Wl, P, tables, batch, T, Cp):
    sh1, sc1, g1, sh2, sc2, g2 = [m.reshape(batch, 1, D_MODEL) for m in jnp.split(mod, 6, axis=-1)]
    tm = 1024 if T % 1024 == 0 else 512
    gd = N_KV * HEAD_DIM
    h = _norm_mod(x, P['norm1_g'][l], sc1, sh1, tm=512, rows_per_batch=T, name=f"norm1_p{l}")
    q_t = _mm_t(h, Wl['q_t'], tm=tm, tn=512, rows_per_batch=T, name=f"proj_q_p{l}", out_dtype=BF16,
                scale=SCALE * LOG2E)
    q = q_t.transpose(0, 2, 1).reshape(batch * T, Q_COLS)
    Cp['kv_out'] = _mm_t(h, Wl['kvrows_t'], tm=tm, tn=512, rows_per_batch=T, name=f"proj_kv_p{l}", layer=l,
                         out_buf=Cp.get('kv_out'))
    win_t = _mm_t(h, Wl['kvwin_t'], tm=tm, tn=512, rows_per_batch=T, name=f"proj_win_p{l}")

    def slot_rows(slot):
        src, s0 = (Cp['kv_out'][l], slot * gd) if slot < 4 else (win_t, (slot - 4) * gd)
        return src[:, s0:s0 + gd, :].reshape(batch, N_KV, HEAD_DIM, T).transpose(0, 3, 1, 2)

    small = _mm(h, Wl['small'], tm=tm, tn=2 * LANES, name=f"proj_small_p{l}")
    z = _mm(h, Wl['z'], tm=tm, tn=512, name=f"proj_z_p{l}")
    xbc = _proj_conv(h, Wl['xbc'], P['ssm_conv_w'][l], P['ssm_conv_b'][l], tm=512, tn=1024, rows_per_batch=T,
                     name=f"proj_xbc_p{l}")
    gates = _mm(h, Wl['mg'], tm=tm, tn=512, name=f"proj_mg_p{l}", epilogue=jax.nn.sigmoid)

    o_nsa = _prompt_attention(q, slot_rows, small, l, Wl, P, tables, batch, T)

    y, ssm_state = _ssd_prompt(xbc, small, z, P['dt_bias'][l], P['a_log'][l], P['d_skip'][l], P['ssm_norm_g'][l],
                               batch=batch, name=f"ssd_p{l}")

    u = _merge(o_nsa, y, Wl['a'], Wl['b'], gates, tm=tm, tn=512, name=f"merge_p{l}")
    x = _mm(u, Wl['o'], tm=tm, tn=512, name=f"proj_o_p{l}", rows_per_batch=T,
            epilogue=lambda acc, xr, gr: xr + gr * acc, extras=[('full', x), ('batch', g1)])

    h2 = _norm_mod(x, P['norm2_g'][l], sc2, sh2, tm=512, rows_per_batch=T, name=f"norm2_p{l}")
    act = _ffn_up(h2, Wl['gu'], P['ffn_conv_w'][l], P['ffn_conv_b'][l], tm=512, tn=D_FF // 2,
                  rows_per_batch=T, name=f"ffn_up_p{l}")
    h2_last = h2.reshape(batch, T, D_MODEL)[:, T - SUBLANES:].reshape(batch * SUBLANES, D_MODEL)
    gate_last = _mm(h2_last, Wl['gu'][:, 0:D_FF], tm=batch * SUBLANES, tn=D_FF // 2, name=f"ffn_state_p{l}")
    ffn_state = gate_last.reshape(batch, SUBLANES, D_FF)[:, SUBLANES - (FFN_CONV - 1):]
    x = _mm(act, Wl['down'], tm=tm, tn=512, name=f"ffn_down_p{l}", rows_per_batch=T,
            epilogue=lambda acc, xr, gr: xr + gr * acc, extras=[('full', x), ('batch', g2)])

    wlen = min(WINDOW, T)
    win_state = win_t[:, :, T - wlen:].reshape(batch, 2, N_KV, HEAD_DIM, wlen).transpose(0, 4, 1, 2, 3)
    h_last = h.reshape(batch, T, D_MODEL)[:, T - SUBLANES:].reshape(batch * SUBLANES, D_MODEL)
    xbc_last = _mm(h_last, Wl['xbc'], tm=batch * SUBLANES, tn=1024, name=f"conv_state_p{l}")
    conv_state = xbc_last.reshape(batch, SUBLANES, CONV_DIM)[:, SUBLANES - (SSM_CONV - 1):]
    return x, (win_state, conv_state, ssm_state, ffn_state)


def _prompt_attention(q, slot_rows, small, l, Wl, P, tables, batch, T):
    nq = T // TQ
    n_cmp = T // L_CMP

    def cmp_in(slot):
        xb = slot_rows(slot).reshape(batch, n_cmp, L_CMP, N_KV, HEAD_DIM).transpose(0, 3, 1, 2, 4)
        return xb.reshape(batch * N_KV * n_cmp, L_CMP * HEAD_DIM)

    ctm = min(512, batch * N_KV * n_cmp)
    kc = _compress(cmp_in(0), P['cmp_pe'][l, 0], Wl['cmp_w1'][0], Wl['cmp_w2'][0], tm=ctm, name=f"cmp_k_p{l}")
    vc = _compress(cmp_in(1), P['cmp_pe'][l, 1], Wl['cmp_w1'][1], Wl['cmp_w2'][1], tm=ctm, name=f"cmp_v_p{l}")

    def even_odd(c):
        c = c.reshape(batch, N_KV, n_cmp // 2, 2, HEAD_DIM).transpose(0, 1, 3, 2, 4)
        return c.reshape(batch, N_KV, n_cmp, HEAD_DIM)

    kc = even_odd(kc.reshape(batch, N_KV, n_cmp, HEAD_DIM)).astype(BF16)
    vct = even_odd(vc.reshape(batch, N_KV, n_cmp, HEAD_DIM)).astype(BF16).transpose(0, 1, 3, 2)

    def k_tiles(slot, front, back, aug):
        k = slot_rows(slot).astype(BF16).reshape(batch, nq, TQ, N_KV, HEAD_DIM).transpose(0, 3, 1, 2, 4)
        k = jnp.pad(k, ((0, 0), (0, 0), (front, back), (0, 0), (0, LANES - HEAD_DIM)))
        if aug:
            kt = jnp.arange(front + nq + back) - front
            blk_col = AUG_ROW0 + 2 * (kt % FAR_TILES)[:, None] + (jnp.arange(TQ) // L_SEL)[None, :]
            lane = jnp.arange(LANES)[None, None, :]
            ones = (lane == blk_col[..., None]) | ((lane >= AUG_ROW0 + SUBLANES) & (lane < AUG_ROW0 + SUBLANES + 3))
            k = jnp.where(ones[None, None], jnp.ones((), BF16), k)
        return k

    def vt_tiles(slot, front, back):
        v = slot_rows(slot).astype(BF16).reshape(batch, nq, TQ, N_KV, HEAD_DIM).transpose(0, 3, 1, 4, 2)
        v = jnp.pad(v, ((0, 0), (0, 0), (front, back), (0, BF16_ROWS), (0, 0)))
        return v.at[:, :, :, HEAD_DIM, :].set(1.0)

    q2 = q.reshape(batch, nq, TQ, N_KV, Q_PER_KV, HEAD_DIM).transpose(0, 3, 1, 5, 4, 2)
    q2 = q2.reshape(batch, N_KV, nq, HEAD_DIM, WIDTH)
    ng = small[:, 0:NSA_GATE_COLS].reshape(batch, nq, TQ, 3, N_KV, Q_PER_KV).transpose(0, 4, 1, 3, 5, 2)
    ng = ng.reshape(batch, N_KV, nq, 3, WIDTH)
    o_t = _nsa_prompt(q2, k_tiles(2, 1, 2 * FAR_TILES, True), vt_tiles(3, 1, 2 * FAR_TILES),
                      k_tiles(4, WIN_TILES - 1, 0, False), vt_tiles(5, WIN_TILES - 1, 0),
                      kc, vct, ng, tables, name=f"nsa_p{l}")
    return o_t.transpose(0, 2, 1).reshape(batch * T, Q_COLS)


def _pool_compress_body(pt_ref, *refs, pages, page):
    x_refs = refs[:pages]
    pe_ref, w1_ref, w2_ref, o_ref, rows_scr = refs[pages:]
    rows = pages * page // L_CMP
    for s in range(2):
        outs = []
        for pair in range(N_KV // 2):
            for p in range(pages):
                rows_scr[p * page:(p + 1) * page, :] = x_refs[p][s, 2 * pair:2 * pair + 2].reshape(
                    2 * HEAD_DIM, page).T
            xs = [(rows_scr[pl.ds(l, rows, stride=L_CMP), :] + pe_ref[s, l:l + 1, :]).astype(BF16)
                  for l in range(L_CMP)]
            w1 = w1_ref[s].reshape(L_CMP * 2 * HEAD_DIM, 2 * CMP_HID)
            hid = _silu(_dot(jnp.concatenate(xs, axis=1), w1)).astype(BF16)
            for gg in range(2):
                outs.append(_dot(hid[:, gg * CMP_HID:(gg + 1) * CMP_HID], w2_ref[s]))
        o_ref[:, s * N_KV * HEAD_DIM:(s + 1) * N_KV * HEAD_DIM] = jnp.concatenate(outs, axis=1)


def _pool_compress(cache_t, l, page_table, pe2, w1pair, w2, *, pages_per_step, name):
    page = cache_t.shape[5]
    n_used = page_table.size
    rows_out = pages_per_step * page // L_CMP
    half = 2 * N_KV * HEAD_DIM
    const = lambda a: pl.BlockSpec(a.shape, lambda i, pt: (0,) * a.ndim)
    page_specs = [pl.BlockSpec((None, None, 2, N_KV, HEAD_DIM, page),
                               lambda i, pt, k=k: (l, pt[i * pages_per_step + k], 0, 0, 0, 0))
                  for k in range(pages_per_step)]
    grid_spec = pltpu.PrefetchScalarGridSpec(
        num_scalar_prefetch=1,
        grid=(n_used // pages_per_step,),
        in_specs=page_specs + [const(pe2), const(w1pair), const(w2)],
        out_specs=pl.BlockSpec((rows_out, half), lambda i, pt: (i, 0)),
        scratch_shapes=[pltpu.VMEM((pages_per_step * page, 2 * HEAD_DIM), F32)])
    return pl.pallas_call(
        functools.partial(_pool_compress_body, pages=pages_per_step, page=page),
        grid_spec=grid_spec,
        out_shape=jax.ShapeDtypeStruct((n_used * page // L_CMP, half), F32),
        compiler_params=_cparams(("arbitrary",)),
        name=name,
    )(page_table.reshape(n_used), *([cache_t] * pages_per_step), pe2, w1pair, w2)


def _softmax_lanes(s):
    m = jnp.max(s, axis=1, keepdims=True)
    e = jnp.exp(s - m)
    return m, e, jnp.sum(e, axis=1, keepdims=True)


def _sum3(lhs_bf16, x):
    hi, mid, lo = _split3(x)
    return _dot(lhs_bf16, hi) + _dot(lhs_bf16, mid) + _dot(lhs_bf16, lo)


def _nsa_decode_body(pt_ref, q_ref, gate_ref, kvnew_ref, kcn_ref, wincol_ref, *refs,
                     n_pages, n_blocks, top_n, win_len, has_prev):
    sel_pages = refs[0:n_pages]
    rest = refs[n_pages:]
    kc_ref, win_ref, bsel_ref, bwin_ref, bcmp_ref, e_ref, pair_ref, rsum_ref, rexp_ref = rest[0:9]
    o_ref, wout_ref, kc_scr, s_scr = rest[9 + int(has_prev):]
    gd = N_KV * HEAD_DIM

    last = lax.broadcasted_iota(jnp.int32, (gd, win_len), 1) == win_len - 1
    for s in range(2):
        shifted = pltpu.roll(win_ref[s].reshape(gd, win_len), win_len - 1, axis=1)
        wout_ref[s] = jnp.where(last, wincol_ref[s * gd:(s + 1) * gd, :], shifted).reshape(
            N_KV, HEAD_DIM, win_len)
    page = sel_pages[0].shape[3]
    n_past_blocks = kc_ref.shape[0]

    @pl.when(pl.program_id(0) == 0)
    def _():
        kc_scr[...] = jnp.zeros(kc_scr.shape, F32)

    q = q_ref[...]
    q_f = q.astype(F32)
    lane0 = lax.broadcasted_iota(jnp.int32, (N_HEADS, LANES), 1) == 0
    rb = lambda v: v.astype(BF16).astype(F32)

    def new_scores(slot):
        k_new = rb(kvnew_ref[:, slot * gd:(slot + 1) * gd])
        return jnp.where(lane0, jnp.sum(q_f * k_new, axis=1, keepdims=True), 0.0)

    def new_value(e_tile, slot):
        return rb(e_tile[:, 0:1]) * rb(kvnew_ref[:, slot * gd:(slot + 1) * gd])

    kc_scr[0:n_past_blocks, :] = kc_ref[...]
    kc_scr[n_past_blocks:n_past_blocks + SUBLANES, :] = kcn_ref[...]
    kcv = kc_scr[...]
    m_c, e_c, den_c = _softmax_lanes(_dot_nt(q, kcv[:, 0:gd].astype(BF16)) + bcmp_ref[...])
    p_c = e_c * ((m_c > 0.5 * NEG_INF).astype(F32) / den_c)
    o_cmp = _dot(p_c.astype(BF16), kcv[:, gd:2 * gd].astype(BF16))
    imp = _sum3(rsum_ref[...], _sum3_rhs(p_c, pair_ref[...]))

    lane_i = lax.broadcasted_iota(jnp.int32, (SUBLANES, LANES), 1)
    forced = (lane_i == 0) | (lane_i == n_blocks - 1)
    score = jnp.where(lane_i < n_blocks, jnp.where(forced, 1e30, imp), -1.0)
    score_t = score.T
    ii = lax.broadcasted_iota(jnp.int32, (LANES, LANES), 0)
    jj = lax.broadcasted_iota(jnp.int32, (LANES, LANES), 1)
    sel_rows = []
    for g in range(N_KV):
        col = jnp.broadcast_to(score_t[:, g:g + 1], (LANES, LANES))
        rowv = jnp.broadcast_to(score[g:g + 1, :], (LANES, LANES))
        ahead = (col > rowv) | ((col == rowv) & (ii < jj))
        rank = jnp.sum(ahead.astype(F32), axis=0, keepdims=True)
        sel_rows.append((rank < float(top_n)).astype(F32))
    sel = jnp.concatenate(sel_rows + [jnp.zeros((SUBLANES - N_KV, LANES), F32)], axis=0)
    sel = jnp.where(lane_i < n_blocks, sel, 0.0)
    sel_h = _dot(rexp_ref[...], sel.astype(BF16))
    sel_keys = _dot(sel_h.astype(BF16), e_ref[...])

    for p in range(n_pages):
        s_scr[:, p * page:(p + 1) * page] = _dot(q, sel_pages[p][0].reshape(gd, page).astype(BF16))
    s_scr[:, n_pages * page:n_pages * page + LANES] = new_scores(2)
    s = jnp.where(sel_keys > 0.5, s_scr[...] + bsel_ref[...], NEG_INF)
    _, e_s, den_s = _softmax_lanes(s)
    e_b = e_s.astype(BF16)
    o_sel = new_value(e_s[:, n_pages * page:n_pages * page + LANES], 3)
    for p in range(n_pages):
        o_sel = o_sel + _dot_nt(e_b[:, p * page:(p + 1) * page], sel_pages[p][1].reshape(gd, page).astype(BF16))
    o_sel = o_sel * (1.0 / den_s)

    s_w = jnp.concatenate([_dot(q, win_ref[0].reshape(gd, win_len).astype(BF16)), new_scores(4)], axis=1)
    _, e_w, den_w = _softmax_lanes(s_w + bwin_ref[...])
    o_win = (_dot_nt(e_w[:, 0:win_len].astype(BF16), win_ref[1].reshape(gd, win_len).astype(BF16))
             + new_value(e_w[:, win_len:win_len + LANES], 5)) * (1.0 / den_w)

    g = jax.nn.sigmoid(gate_ref[...])
    o = g[:, 0:1] * o_cmp + g[:, 1:2] * o_sel + g[:, 2:3] * o_win
    own = (lax.broadcasted_iota(jnp.int32, (N_HEADS, gd), 1) // HEAD_DIM
           == lax.broadcasted_iota(jnp.int32, (N_HEADS, gd), 0) // Q_PER_KV)
    o = jnp.where(own, o, 0.0)
    acc = o[:, 0:HEAD_DIM]
    for gi in range(1, N_KV):
        acc = acc + o[:, gi * HEAD_DIM:(gi + 1) * HEAD_DIM]
    o_ref[...] = acc


def _sum3_rhs(x, rhs_bf16):
    hi, mid, lo = _split3(x)
    return _dot(hi, rhs_bf16) + _dot(mid, rhs_bf16) + _dot(lo, rhs_bf16)


def _nsa_decode(page_table, q_bd, gates_t, kvnew, kcn, cache_t, kcpool, win_t, win_out, l, tabs, *, name):
    B, n_pages = page_table.shape
    page = cache_t.shape[5]
    gd = N_KV * HEAD_DIM
    win_len = win_t.shape[5]
    bsel, bwin, bcmp, e_mat, pair, rsum, rexp, n_blocks = tabs
    const = lambda a: pl.BlockSpec(a.shape, lambda b, pt: (0,) * a.ndim)
    sel_specs = [pl.BlockSpec((None, None, 2, N_KV, HEAD_DIM, page), lambda b, pt, p=p: (l, pt[b, p], 1, 0, 0, 0))
                 for p in range(n_pages)]
    kc_specs = [pl.BlockSpec((None, kcpool.shape[1], 2 * gd), lambda b, pt: (b, 0, 0))]
    win_spec = pl.BlockSpec((None, None, 2, N_KV, HEAD_DIM, win_len), lambda b, pt: (l, b, 0, 0, 0, 0))
    ops = [page_table, q_bd, gates_t, kvnew, kcn, kvnew[:, 0, 4 * gd:].reshape(B, 2 * gd, 1),
           *([cache_t] * n_pages), kcpool, win_t, bsel, bwin, bcmp, e_mat, pair, rsum, rexp]
    in_specs = ([pl.BlockSpec((None, N_HEADS, gd), lambda b, pt: (b, 0, 0)),
                 pl.BlockSpec((None, N_HEADS, 3), lambda b, pt: (b, 0, 0)),
                 pl.BlockSpec((None, 1, kvnew.shape[2]), lambda b, pt: (b, 0, 0)),
                 pl.BlockSpec((None, SUBLANES, 2 * gd), lambda b, pt: (b, 0, 0)),
                 pl.BlockSpec((None, 2 * gd, 1), lambda b, pt: (b, 0, 0))]
                + sel_specs + kc_specs
                + [win_spec, const(bsel), const(bwin), const(bcmp), const(e_mat), const(pair), const(rsum),
                   const(rexp)])
    aliases = {}
    if win_out is not None:
        aliases = {len(ops): 1}
        ops.append(win_out)
        in_specs.append(pl.BlockSpec(memory_space=pl.ANY))
    grid_spec = pltpu.PrefetchScalarGridSpec(
        num_scalar_prefetch=1,
        grid=(B,),
        in_specs=in_specs,
        out_specs=[pl.BlockSpec((None, N_HEADS, HEAD_DIM), lambda b, pt: (b, 0, 0)), win_spec],
        scratch_shapes=[pltpu.VMEM((LANES, 2 * gd), F32),
                        pltpu.VMEM((N_HEADS, n_pages * page + LANES), F32)])
    return pl.pallas_call(
        functools.partial(_nsa_decode_body, n_pages=n_pages, n_blocks=n_blocks,
                          top_n=min(TOP_N, n_blocks), win_len=win_len, has_prev=win_out is not None),
        grid_spec=grid_spec,
        out_shape=[jax.ShapeDtypeStruct((B, N_HEADS, HEAD_DIM), F32), jax.ShapeDtypeStruct(win_t.shape, F32)],
        input_output_aliases=aliases,
        compiler_params=_cparams(("arbitrary",)),
        name=name,
    )(*ops)


def _decode_tables(rel_bias, past_len, n_pages, win_len):
    q_pos = past_len
    n_blocks = (past_len + 1 + L_SEL - 1) // L_SEL
    n_cmp = n_blocks * (L_SEL // L_CMP)
    by_dist = rel_bias[_t5_bucket(jnp.arange(q_pos + 1))].T
    keys = (n_pages + 1) * LANES
    kpos = jnp.arange(keys)
    bsel = jnp.where((kpos <= q_pos)[None], by_dist[:, jnp.clip(q_pos - kpos, 0)], NEG_INF)
    wl = jnp.arange(win_len + LANES)
    wdist = win_len - wl
    bwin = jnp.where((wdist >= 0)[None], by_dist[:, jnp.clip(wdist, 0, q_pos)], NEG_INF)
    c = jnp.arange(LANES)
    cdist = q_pos - (c * L_CMP + L_CMP - 1)
    bcmp = jnp.where(((cdist >= 0) & (c < n_cmp))[None], by_dist[:, jnp.clip(cdist, 0)], NEG_INF)
    e_mat = (jnp.arange(LANES)[:, None] == (kpos // L_SEL)[None, :]).astype(BF16)
    pair = ((c[:, None] // (L_SEL // L_CMP) == c[None, :]) & (c[:, None] < n_cmp)).astype(BF16)
    heads = jnp.arange(N_HEADS)
    rsum = (jnp.arange(SUBLANES)[:, None] == (heads // Q_PER_KV)[None, :]).astype(BF16)
    return bsel, bwin, bcmp, e_mat, pair, rsum, rsum.T, n_blocks


def _conv_step_body(x_ref, p0_ref, p1_ref, p2_ref, cw_ref, cb_ref, dt_ref, dtb_ref, alog_ref,
                    xact_ref, dto_ref, dec_ref):
    y = cb_ref[...]
    for k, ref in enumerate((p0_ref, p1_ref, p2_ref, x_ref)):
        y = y + cw_ref[k:k + 1, :] * ref[...]
    xact_ref[...] = _silu(y)
    x = dt_ref[:, 0:SSM_HEADS] + dtb_ref[...]
    dt = jnp.maximum(x, 0.0) + jnp.log1p(jnp.exp(-jnp.abs(x)))
    dto_ref[...] = dt
    dec_ref[...] = jnp.exp(dt * (-jnp.exp(alog_ref[...])))


def _conv_step(xbc, prev, conv_w, conv_b, dtg, dt_bias, a_log, *, name):
    B = xbc.shape[0]
    full = lambda a: pl.BlockSpec(a.shape, lambda i: (0,) * a.ndim)
    ops = [xbc, prev[:, 0], prev[:, 1], prev[:, 2], conv_w, conv_b.reshape(1, -1), dtg,
           dt_bias.reshape(1, -1), a_log.reshape(1, -1)]
    in_specs = [full(a) for a in ops]
    in_specs[6] = pl.BlockSpec((B, LANES), lambda i: (0, 1))
    return pl.pallas_call(
        _conv_step_body,
        grid=(1,),
        in_specs=in_specs,
        out_specs=[pl.BlockSpec((B, CONV_DIM), lambda i: (0, 0)), pl.BlockSpec((B, SSM_HEADS), lambda i: (0, 0)),
                   pl.BlockSpec((B, SSM_HEADS), lambda i: (0, 0))],
        out_shape=[jax.ShapeDtypeStruct((B, CONV_DIM), F32), jax.ShapeDtypeStruct((B, SSM_HEADS), F32),
                   jax.ShapeDtypeStruct((B, SSM_HEADS), F32)],
        compiler_params=_cparams(("arbitrary",)),
        name=name,
    )(*ops)


def _ssd_step_body(dts_ref, decs_ref, h0_ref, xs_ref, xst_ref, bm_ref, cm_ref, z_ref, dtc_ref, dskip_ref, ng_ref,
                   *refs, has_prev):
    y_ref, st_ref, yoff_scr = refs[int(has_prev):]
    b = pl.program_id(0)
    heads_per_group = SSM_HEADS // SSM_GROUPS
    bm = bm_ref[...]
    cm = cm_ref[...]
    rb = lambda v: v.astype(BF16).astype(F32)
    cb = jnp.sum(rb(cm) * rb(bm), axis=-1, keepdims=True)
    cm8 = jnp.concatenate([cm, jnp.zeros((SUBLANES - SSM_GROUPS, D_STATE), F32)], axis=0).astype(BF16)
    for h in range(SSM_HEADS):
        g = h // heads_per_group
        h_prev = h0_ref[h]
        dec = decs_ref[b, h]
        y_off = _dot_nt(cm8, h_prev.astype(BF16))
        yoff_scr[h:h + 1, :] = y_off[g:g + 1, :] * dec
        xdt = xst_ref[:, h:h + 1] * dts_ref[b, h]
        st_ref[h] = h_prev * dec + xdt * bm[g:g + 1, :]
    xs = xs_ref[...]
    cbh = jnp.concatenate([jnp.broadcast_to(cb[g:g + 1, :], (heads_per_group, 1)) for g in range(SSM_GROUPS)], axis=0)
    y = (cbh * dtc_ref[...]) * xs + yoff_scr[...] + dskip_ref[...] * xs
    y = y * _silu(z_ref[...])
    for g in range(SSM_GROUPS):
        yg = y[g * heads_per_group:(g + 1) * heads_per_group, :]
        ms = jnp.sum(jnp.sum(yg * yg, axis=1, keepdims=True), axis=0, keepdims=True) * (1.0 / (heads_per_group * SSM_HEADDIM))
        y_ref[g * heads_per_group:(g + 1) * heads_per_group, :] = (
            yg * lax.rsqrt(ms + EPS) * ng_ref[g * heads_per_group:(g + 1) * heads_per_group, :]).astype(y_ref.dtype)


def _ssd_step(state, state_out, l, xact, dt, dec, z, d_skip, norm_g, *, name):
    B = xact.shape[0]
    H, Pd, N = SSM_HEADS, SSM_HEADDIM, D_STATE
    xs = xact[:, 0:D_INNER].reshape(B, H, Pd)
    bm = xact[:, D_INNER:D_INNER + SSM_GROUPS * N].reshape(B, SSM_GROUPS, N)
    cm = xact[:, D_INNER + SSM_GROUPS * N:].reshape(B, SSM_GROUPS, N)
    per_b = lambda shape: pl.BlockSpec((None,) + shape, lambda b, *_: (b,) + (0,) * len(shape))
    const = lambda shape: pl.BlockSpec(shape, lambda b, *_: (0,) * len(shape))
    state_spec = pl.BlockSpec((None, None, H, Pd, N), lambda b, *_: (l, b, 0, 0, 0))
    ops = [dt, dec, state, xs, xs.transpose(0, 2, 1), bm, cm, z.reshape(B, H, Pd), dt.reshape(B, H, 1),
           d_skip.reshape(H, 1), norm_g.reshape(H, Pd)]
    in_specs = [state_spec, per_b((H, Pd)), per_b((Pd, H)), per_b((SSM_GROUPS, N)), per_b((SSM_GROUPS, N)),
                per_b((H, Pd)), per_b((H, 1)), const((H, 1)), const((H, Pd))]
    aliases = {}
    if state_out is not None:
        aliases = {len(ops): 1}
        ops.append(state_out)
        in_specs.append(pl.BlockSpec(memory_space=pl.ANY))
    grid_spec = pltpu.PrefetchScalarGridSpec(
        num_scalar_prefetch=2,
        grid=(B,),
        in_specs=in_specs,
        out_specs=[per_b((H, Pd)), state_spec],
        scratch_shapes=[pltpu.VMEM((H, Pd), F32)])
    return pl.pallas_call(
        functools.partial(_ssd_step_body, has_prev=state_out is not None),
        grid_spec=grid_spec,
        out_shape=[jax.ShapeDtypeStruct((B, H, Pd), BF16), jax.ShapeDtypeStruct(state.shape, F32)],
        input_output_aliases=aliases,
        compiler_params=_cparams(("arbitrary",)),
        name=name,
    )(*ops)


def _ffn_up_step_body(h_ref, wg_ref, wu_ref, p0_ref, p1_ref, cw_ref, cb_ref, act_ref, gate_ref):
    h = h_ref[...]
    gate = _dot(h, wg_ref[...])
    up = _dot(h, wu_ref[...])
    y = cb_ref[...] + cw_ref[0:1, :] * p0_ref[...] + cw_ref[1:2, :] * p1_ref[...] + cw_ref[2:3, :] * gate
    gate_ref[...] = gate
    act_ref[...] = (_silu(y) * up).astype(act_ref.dtype)


def _ffn_up_step(h, w_gu, prev, conv_w, conv_b, *, tn, name):
    B, K = h.shape
    nj = D_FF // tn
    col = lambda r: pl.BlockSpec((r, tn), lambda j: (0, j))
    return pl.pallas_call(
        _ffn_up_step_body,
        grid=(nj,),
        in_specs=[pl.BlockSpec((B, K), lambda j: (0, 0)), col(K),
                  pl.BlockSpec((K, tn), lambda j, nj=nj: (0, j + nj)),
                  col(B), col(B), col(FFN_CONV), col(1)],
        out_specs=[col(B), col(B)],
        out_shape=[jax.ShapeDtypeStruct((B, D_FF), BF16), jax.ShapeDtypeStruct((B, D_FF), F32)],
        compiler_params=_cparams(("parallel",)),
        name=name,
    )(h, w_gu, w_gu, prev[:, 0], prev[:, 1], conv_w, conv_b.reshape(1, D_FF))


def _pages_per_step(n_pool, limit=32):
    return max(p for p in range(2, limit + 1, 2) if n_pool % p == 0)


def _sample_layer(x, mod, l, Wl, P, C):
    B = x.shape[0]
    gd = N_KV * HEAD_DIM
    sh1, sc1, g1, sh2, sc2, g2 = jnp.split(mod, 6, axis=-1)
    h = _norm_mod(x, P['norm1_g'][l], sc1, sh1, tm=B, rows_per_batch=1, name=f"norm1_s{l}")
    q = _mm(h, Wl['q'], tm=B, tn=512, name=f"proj_q_s{l}", out_dtype=BF16, epilogue=lambda acc: acc * SCALE)
    kv = _mm(h, Wl['kv'], tm=B, tn=512, name=f"proj_kv_s{l}")
    small = _mm(h, Wl['small'], tm=B, tn=2 * LANES, name=f"proj_small_s{l}")
    z = _mm(h, Wl['z'], tm=B, tn=512, name=f"proj_z_s{l}")
    xbc = _mm(h, Wl['xbc'], tm=B, tn=512, name=f"proj_xbc_s{l}")
    gates = _mm(h, Wl['mg'], tm=B, tn=512, name=f"proj_mg_s{l}", epilogue=jax.nn.sigmoid)

    cache_kv, page_table = C['cache_kv'], C['page_table']
    n_pool, page = cache_kv.shape[1], cache_kv.shape[2]
    cache_t = cache_kv.transpose(0, 1, 3, 4, 5, 2)
    win_t = C['cache_win_kv'].transpose(0, 1, 3, 4, 5, 2)
    kcpool = _pool_compress(cache_t, l, page_table, Wl['pe_pair'], Wl['w1_pair'], Wl['cmp_w2'],
                            pages_per_step=_pages_per_step(page_table.size), name=f"cmp_pool_s{l}")
    kcpool = kcpool.reshape(B, page_table.shape[1] * (page // L_CMP), 2 * gd)

    def new_blocks(slot):
        first = jnp.pad(kv[:, slot * gd:(slot + 1) * gd].reshape(B, N_KV, 1, HEAD_DIM),
                        ((0, 0), (0, 0), (0, 1), (0, (L_CMP - 1) * HEAD_DIM)))
        rows = B * N_KV * 2
        out = _compress(first.reshape(rows, L_CMP * HEAD_DIM), P['cmp_pe'][l, slot], Wl['cmp_w1'][slot],
                        Wl['cmp_w2'][slot], tm=512 if rows % 512 == 0 else rows, name=f"cmp{slot}_new_s{l}")
        return out.reshape(B, N_KV, 2, HEAD_DIM).transpose(0, 2, 1, 3).reshape(B, 2, gd)

    kcn = jnp.pad(jnp.concatenate([new_blocks(0), new_blocks(1)], axis=-1), ((0, 0), (0, SUBLANES - 2), (0, 0)))
    qh = q.reshape(B, N_KV, Q_PER_KV, 1, HEAD_DIM)
    q_bd = (qh * jnp.eye(N_KV, dtype=BF16)[None, :, None, :, None]).reshape(B, N_HEADS, gd)
    gates_t = small[:, 0:NSA_GATE_COLS].reshape(B, 3, N_HEADS).transpose(0, 2, 1)
    o_nsa, C['win_out'] = _nsa_decode(page_table, q_bd, gates_t, kv.reshape(B, 1, KV_COLS), kcn, cache_t, kcpool,
                                      win_t, C.get('win_out'), l, C['tables'], name=f"nsa_s{l}")
    o_nsa = o_nsa.reshape(B, Q_COLS)
    new_rows = kv[:, 0:4 * gd].reshape(B, 1, 4, N_KV, HEAD_DIM)

    conv_prev = C['state_ssm_conv'][l]
    xact, dt, dec = _conv_step(xbc, conv_prev, P['ssm_conv_w'][l], P['ssm_conv_b'][l], small,
                               P['dt_bias'][l], P['a_log'][l], name=f"conv_s{l}")
    conv_state = jnp.concatenate([conv_prev[:, 1:], xbc[:, None]], axis=1)
    y, C['ssm_out'] = _ssd_step(C['state_ssm'], C.get('ssm_out'), l, xact, dt, dec, z, P['d_skip'][l],
                                P['ssm_norm_g'][l], name=f"ssd_s{l}")

    u = _merge(o_nsa.astype(BF16), y.reshape(B, D_INNER), Wl['a'], Wl['b'], gates, tm=B, tn=512, name=f"merge_s{l}")
    x = _mm(u, Wl['o'], tm=B, tn=512, name=f"proj_o_s{l}",
            epilogue=lambda acc, xr, gr: xr + gr * acc, extras=[('full', x), ('full', g1)])

    h2 = _norm_mod(x, P['norm2_g'][l], sc2, sh2, tm=B, rows_per_batch=1, name=f"norm2_s{l}")
    ffn_prev = C['state_ffn_conv'][l]
    act, gate_raw = _ffn_up_step(h2, Wl['gu'], ffn_prev, P['ffn_conv_w'][l], P['ffn_conv_b'][l],
                                 tn=D_FF // 2, name=f"ffn_up_s{l}")
    ffn_state = jnp.concatenate([ffn_prev[:, 1:], gate_raw[:, None]], axis=1)
    x = _mm(act, Wl['down'], tm=B, tn=512, name=f"ffn_down_s{l}",
            epilogue=lambda acc, xr, gr: xr + gr * acc, extras=[('full', x), ('full', g2)])
    return x, (new_rows, conv_state, ffn_state)


def kernel(x_prompt, x_sample, cache_kv, cache_win_kv, state_ssm_conv, state_ssm, state_ffn_conv, page_table, c_prompt, c_sample, w_ada, b_ada, norm1_g, norm2_g, final_g, w_in, rel_bias, cmp_pe, cmp_w1, cmp_w2, ssm_conv_w, ssm_conv_b, dt_bias, a_log, d_skip, ssm_norm_g, w_a, w_b, w_o, w_gu, ffn_conv_w, ffn_conv_b, w_down):
    P = dict(norm1_g=norm1_g, norm2_g=norm2_g, rel_bias=rel_bias, cmp_pe=cmp_pe.reshape(DEPTH, 2, -1),
             ssm_conv_w=ssm_conv_w, ssm_conv_b=ssm_conv_b, dt_bias=dt_bias, a_log=a_log, d_skip=d_skip,
             ssm_norm_g=ssm_norm_g, ffn_conv_w=ffn_conv_w, ffn_conv_b=ffn_conv_b)
    Bp, T, _ = x_prompt.shape
    Bs = x_sample.shape[0]
    past_len = page_table.shape[1] * cache_kv.shape[2]
    tables = _bias_tables(rel_bias * LOG2E)
    n_c = Bp + Bs
    c_rows = -(-n_c // SUBLANES) * SUBLANES
    c_all = jnp.zeros((c_rows, D_MODEL), F32).at[:Bp].set(c_prompt).at[Bp:n_c].set(c_sample).astype(BF16)
    xp = x_prompt.reshape(Bp * T, D_MODEL)
    xs = x_sample.reshape(Bs, D_MODEL)
    C = dict(cache_kv=cache_kv, page_table=page_table, cache_win_kv=cache_win_kv, state_ssm_conv=state_ssm_conv,
             state_ssm=state_ssm, state_ffn_conv=state_ffn_conv,
             tables=_decode_tables(rel_bias, past_len, page_table.shape[1], cache_win_kv.shape[2]))
    Cp = {}
    st_p, st_s = [], []
    for l in range(DEPTH):
        Wl = _layer_weights(l, w_ada, w_in, cmp_pe, cmp_w1, cmp_w2, w_a, w_b, w_o, w_gu, w_down)
        mod = _mm(c_all, Wl['ada'], tm=c_rows, tn=512, name=f"ada{l}",
                  epilogue=lambda acc, b: acc + b, extras=[('col', b_ada[l].reshape(1, -1))])
        xp, sp = _prompt_layer(xp, mod[:Bp], l, Wl, P, tables, Bp, T, Cp)
        st_p.append(sp)
        xs, ss = _sample_layer(xs, mod[Bp:n_c], l, Wl, P, C)
        st_s.append(ss)
    y_prompt = _final_norm(xp, final_g, tm=512, name="final_norm_p").reshape(Bp, T, D_MODEL)
    y_sample = _final_norm(xs, final_g, tm=Bs, name="final_norm_s").reshape(Bs, 1, D_MODEL)
    win_p, conv_p, ssm_p, ffn_p = [jnp.stack([s[i] for s in st_p]) for i in range(4)]
    kv_p = Cp['kv_out'].reshape(DEPTH, Bp, 4, N_KV, HEAD_DIM, T).transpose(0, 1, 5, 2, 3, 4)
    kv_s, conv_s, ffn_s = [jnp.stack([s[i] for s in st_s]) for i in range(3)]
    win_s = C['win_out'].transpose(0, 1, 5, 2, 3, 4)
    ssm_s = C['ssm_out']
    return (y_prompt, y_sample, kv_p, win_p, conv_p, ssm_p, ffn_p, kv_s, win_s, conv_s, ssm_s, ffn_s)
```
